```python
import math
import jax, jax.numpy as jnp
from jax import lax
import numpy as np

D_MODEL = 1024
BATCH = 4
SEQ = 4096
DEPTH = 4
DEC_BATCH = 32
DEC_SEQ = 1
PAST_LEN = 8192
PAGE_SIZE = 128

N_MIXERS = 3
N_A = (DEPTH + 2) // N_MIXERS
N_B = (DEPTH + 1) // N_MIXERS
N_C = DEPTH // N_MIXERS

A_HEADS = 16
A_KV_HEADS = 4
A_HEAD_DIM = D_MODEL // A_HEADS
A_GROUP = A_HEADS // A_KV_HEADS
CMP_BLOCK = 32
CMP_STRIDE = 16
CMP_RATIO = CMP_BLOCK // CMP_STRIDE
CMP_HIDDEN = 2 * A_HEAD_DIM
SEL_BLOCK = 64
SEL_RATIO = SEL_BLOCK // CMP_STRIDE
SEL_TOPK = 16
WINDOW = 512
Q_BLOCK = 64
A_ATT_WIDTH = A_HEADS * A_HEAD_DIM
A_KV_WIDTH = 2 * A_KV_HEADS * A_HEAD_DIM
A_COLS = A_ATT_WIDTH + 3 * A_KV_WIDTH + 3 * A_HEADS + A_ATT_WIDTH

S5_WIDTH = D_MODEL
S5_GROUP_CH = 16
S5_GROUPS = S5_WIDTH // S5_GROUP_CH
S5_STATE = 64

C_HEAD_DIM = 64
C_HEADS = D_MODEL // C_HEAD_DIM
C_LORA_W = 64
C_LORA_A = 64
C_N_MIX = 6
C_COLS = 4 * D_MODEL + C_LORA_W + C_LORA_A
C_LN_EPS = 64e-5

LN_EPS = 1e-5
DEEPNORM_ALPHA = (2 * DEPTH) ** 0.25
DEEPNORM_BETA = (8 * DEPTH) ** -0.25

kernel_name = 'hybrid_nsa_s5_rwkv7_deepnorm_step'


def alibi_slopes():
    return 2.0 ** (-8.0 * jnp.arange(1, A_HEADS + 1, dtype=jnp.float32) / A_HEADS)


def layer_norm(x, g, b):
    xf = x.astype(jnp.float32)
    mu = jnp.mean(xf, axis=-1, keepdims=True)
    var = jnp.mean(jnp.square(xf - mu), axis=-1, keepdims=True)
    return ((xf - mu) * lax.rsqrt(var + LN_EPS) * g + b).astype(x.dtype)


def masked_softmax(s, mask):
    s = jnp.where(mask, s.astype(jnp.float32), -jnp.inf)
    m = jnp.max(s, axis=-1, keepdims=True)
    e = jnp.where(mask, jnp.exp(s - jnp.where(jnp.isfinite(m), m, 0.0)), 0.0)
    return e / jnp.maximum(jnp.sum(e, axis=-1, keepdims=True), 1e-30)


def compress_kv(kv, pe, w1, w2):
    bsz, seq_len = kv.shape[:2]
    n_cmp = (seq_len - CMP_BLOCK) // CMP_STRIDE + 1
    n_chunk = seq_len // CMP_STRIDE
    chunks = kv[:, : n_chunk * CMP_STRIDE].reshape(bsz, n_chunk, CMP_STRIDE, 2, A_KV_HEADS, A_HEAD_DIM)
    w1r = w1.reshape(2, CMP_RATIO, CMP_STRIDE, A_HEAD_DIM, CMP_HIDDEN)
    pe_term = jnp.einsum('crsd,crsdh->ch', pe.reshape(2, CMP_RATIO, CMP_STRIDE, A_HEAD_DIM), w1r)
    h = pe_term[None, :, None, None, :]
    for r in range(CMP_RATIO):
        h = h + jnp.einsum('bnscgd,csdh->bcgnh', chunks[:, r:r + n_cmp], w1r[:, r])
    out = jnp.einsum('bcgnh,chd->bcgnd', jax.nn.silu(h), w2)
    return out[:, 0], out[:, 1]


def nsa_mixer(x, past_cmp, past_sel, past_win, w_in, cmp_pe, cmp_w1, cmp_w2, w_out):
    bsz, n_new, _ = x.shape
    n_past = past_cmp.shape[1]
    n_buf = past_win.shape[1]
    cuts = [int(c) for c in np.cumsum([A_ATT_WIDTH, A_KV_WIDTH, A_KV_WIDTH, A_KV_WIDTH, 3 * A_HEADS])]
    q, kv_c, kv_s, kv_w, gate, z = jnp.split(x @ w_in, cuts, axis=-1)
    kv_shape = (bsz, n_new, 2, A_KV_HEADS, A_HEAD_DIM)
    kv_c, kv_s, kv_w = kv_c.reshape(kv_shape), kv_s.reshape(kv_shape), kv_w.reshape(kv_shape)
    q = q.reshape(bsz, n_new, A_KV_HEADS, A_GROUP, A_HEAD_DIM)
    gate = jax.nn.sigmoid(gate.reshape(bsz, n_new, 3, A_KV_HEADS, A_GROUP).astype(jnp.float32))

    cmp_all = jnp.concatenate([past_cmp, kv_c], axis=1)
    sel_all = jnp.concatenate([past_sel, kv_s], axis=1)
    win_all = jnp.concatenate([past_win, kv_w], axis=1)
    seq_len = n_past + n_new

    k_cmp, v_cmp = compress_kv(cmp_all, cmp_pe, cmp_w1, cmp_w2)
    n_cmp = k_cmp.shape[2]
    cmp_end = jnp.arange(n_cmp) * CMP_STRIDE + CMP_BLOCK - 1

    n_sel = -(-seq_len // SEL_BLOCK)
    sel_pad = jnp.pad(sel_all, ((0, 0), (0, n_sel * SEL_BLOCK - seq_len), (0, 0), (0, 0), (0, 0)))
    sel_blocks = sel_pad.reshape(bsz, n_sel, SEL_BLOCK, 2, A_KV_HEADS, A_HEAD_DIM).transpose(0, 4, 1, 2, 3, 5)
    k_top = min(SEL_TOPK, n_sel)
    win_pad = jnp.pad(win_all, ((0, 0), (WINDOW, 0), (0, 0), (0, 0), (0, 0)))
    slopes = alibi_slopes().reshape(A_KV_HEADS, A_GROUP)[None, :, :, None, None]
    scale = A_HEAD_DIM ** -0.5
    q_blk = Q_BLOCK if n_new % Q_BLOCK == 0 else n_new
    gather_blocks = jax.vmap(jax.vmap(lambda blocks, idx: blocks[idx]))
    span = SEL_RATIO + CMP_RATIO - 1

    def block_fn(start):
        qf = lax.dynamic_slice_in_dim(q, start, q_blk, axis=1).transpose(0, 2, 3, 1, 4) * scale
        g = lax.dynamic_slice_in_dim(gate, start, q_blk, axis=1).transpose(0, 2, 3, 4, 1)[..., None]
        t = n_past + start + jnp.arange(q_blk)
        dist_c = t[:, None] - cmp_end[None, :]
        s_c = jnp.einsum('bgrqd,bgnd->bgrqn', qf, k_cmp) - slopes * dist_c.astype(jnp.float32)
        p_c = masked_softmax(s_c, dist_c >= 0)
        o_c = jnp.einsum('bgrqn,bgnd->bgrqd', p_c, v_cmp)
        p_grp = jnp.pad(p_c.sum(axis=2), ((0, 0), (0, 0), (0, 0), (CMP_RATIO - 1, SEL_RATIO * n_sel - n_cmp)))
        p_slc = sum(p_grp[..., o:o + SEL_RATIO * n_sel:SEL_RATIO] for o in range(span))
        blk = jnp.arange(n_sel)[None, :]
        cur = (t // SEL_BLOCK)[:, None]
        forced = (blk == 0) | (blk == cur) | (blk == cur - 1)
        score = jnp.where(forced, jnp.inf, jnp.where(blk <= cur, p_slc, -jnp.inf))
        top_score, top_idx = lax.top_k(score, k_top)
        picked = gather_blocks(sel_blocks, top_idx)
        n_key = k_top * SEL_BLOCK
        k_sel = picked[..., 0, :].reshape(bsz, A_KV_HEADS, q_blk, n_key, A_HEAD_DIM)
        v_sel = picked[..., 1, :].reshape(bsz, A_KV_HEADS, q_blk, n_key, A_HEAD_DIM)
        k_pos = (top_idx[..., None] * SEL_BLOCK + jnp.arange(SEL_BLOCK)).reshape(bsz, A_KV_HEADS, q_blk, n_key)
        ok_s = jnp.repeat(top_score > -jnp.inf, SEL_BLOCK, axis=-1) & (k_pos <= t[None, None, :, None])
        dist_s = (t[None, None, :, None] - k_pos).astype(jnp.float32)[:, :, None]
        s_s = jnp.einsum('bgrqd,bgqkd->bgrqk', qf, k_sel) - slopes * dist_s
        p_s = masked_softmax(s_s, ok_s[:, :, None])
        o_s = jnp.einsum('bgrqk,bgqkd->bgrqd', p_s, v_sel)
        band = lax.dynamic_slice_in_dim(win_pad, start + n_buf, WINDOW + q_blk, axis=1)
        b_pos = n_past + start - WINDOW + jnp.arange(WINDOW + q_blk)
        dist_w = t[:, None] - b_pos[None, :]
        ok_w = (b_pos[None, :] >= 0) & (dist_w >= 0) & (dist_w <= WINDOW)
        s_w = jnp.einsum('bgrqd,bkgd->bgrqk', qf, band[:, :, 0]) - slopes * dist_w.astype(jnp.float32)
        p_w = masked_softmax(s_w, ok_w)
        o_w = jnp.einsum('bgrqk,bkgd->bgrqd', p_w, band[:, :, 1])
        o = g[:, 0] * o_c + g[:, 1] * o_s + g[:, 2] * o_w
        return o.transpose(0, 3, 1, 2, 4).reshape(bsz, q_blk, A_ATT_WIDTH)

    starts = jnp.arange(n_new // q_blk) * q_blk
    o = lax.map(block_fn, starts)
    o = jnp.moveaxis(o, 0, 1).reshape(bsz, n_new, A_ATT_WIDTH)
    y = (o * jax.nn.silu(z)) @ w_out
    n_keep = min(WINDOW, seq_len)
    return y, kv_c, kv_s, win_all[:, win_all.shape[1] - n_keep:]


def complex_affine_combine(e1, e2):
    a1r, a1i, b1r, b1i = e1
    a2r, a2i, b2r, b2i = e2
    return (a2r * a1r - a2i * a1i, a2r * a1i + a2i * a1r,
            a2r * b1r - a2i * b1i + b2r, a2r * b1i + a2i * b1r + b2i)


def s5_mixer(x, h0_re, h0_im, w_in, log_dt, a_re, a_im, b_re, b_im, c_re, c_im, d_skip, w_glu, b_glu, w_out):
    f32 = jnp.float32
    bsz, n_new, _ = x.shape
    u, z = jnp.split(x @ w_in, 2, axis=-1)
    ug = u.reshape(bsz, n_new, S5_GROUPS, S5_GROUP_CH).astype(f32)
    dt = jnp.exp(log_dt.astype(f32))[:, None]
    ar, ai = a_re.astype(f32), a_im.astype(f32)
    mag = jnp.exp(dt * ar)
    abar_re, abar_im = mag * jnp.cos(dt * ai), mag * jnp.sin(dt * ai)
    den = ar * ar + ai * ai
    num_re, num_im = abar_re - 1.0, abar_im
    zoh_re = (num_re * ar + num_im * ai) / den
    zoh_im = (num_im * ar - num_re * ai) / den
    br, bi = b_re.astype(f32), b_im.astype(f32)
    bbar_re = zoh_re[..., None] * br - zoh_im[..., None] * bi
    bbar_im = zoh_re[..., None] * bi + zoh_im[..., None] * br
    bu_re = jnp.einsum('gpc,btgc->btgp', bbar_re, ug)
    bu_im = jnp.einsum('gpc,btgc->btgp', bbar_im, ug)
    a_seq_re = jnp.broadcast_to(abar_re, bu_re.shape)
    a_seq_im = jnp.broadcast_to(abar_im, bu_im.shape)
    acc_re, acc_im, h_re, h_im = lax.associative_scan(
        complex_affine_combine, (a_seq_re, a_seq_im, bu_re, bu_im), axis=1)
    h0r, h0i = h0_re.astype(f32)[:, None], h0_im.astype(f32)[:, None]
    h_re = h_re + acc_re * h0r - acc_im * h0i
    h_im = h_im + acc_re * h0i + acc_im * h0r
    y = jnp.einsum('gcp,btgp->btgc', c_re.astype(f32), h_re) - jnp.einsum('gcp,btgp->btgc', c_im.astype(f32), h_im)
    y = y.reshape(bsz, n_new, S5_WIDTH) + d_skip * u
    y = jax.nn.gelu(y)
    y = y * jax.nn.sigmoid(y @ w_glu + b_glu)
    return (y * jax.nn.silu(z)) @ w_out, h_re[:, -1], h_im[:, -1]


def wkv7_scan(s0, r, w, k, v, a, b):
    def step(state, inp):
        r_t, w_t, k_t, v_t, a_t, b_t = inp
        sa = jnp.einsum('bhij,bhj->bhi', state, a_t)
        state = state * w_t[:, :, None, :] + sa[..., None] * b_t[:, :, None, :] + v_t[..., None] * k_t[:, :, None, :]
        return state, jnp.einsum('bhij,bhj->bhi', state, r_t)
    xs = tuple(jnp.moveaxis(u_, 1, 0) for u_ in (r, w, k, v, a, b))
    s_last, ys = lax.scan(step, s0, xs)
    return jnp.moveaxis(ys, 0, 1), s_last


def rwkv7_mixer(x, s0, x_last, mu, w_in, w0, w2, a0, a2, k_k, k_a, r_k, lnx_g, lnx_b, w_out):
    f32 = jnp.float32
    bsz, n_new, _ = x.shape
    x_prev = jnp.concatenate([x_last[:, None].astype(x.dtype), x[:, :-1]], axis=1)
    xm = x[:, :, None, :] + (x_prev - x)[:, :, None, :] * mu
    widths = (D_MODEL, C_LORA_W, D_MODEL, D_MODEL, C_LORA_A, D_MODEL)
    offs = [int(o) for o in np.cumsum((0,) + widths)]
    r, w_lo, k, v, a_lo, z = [xm[:, :, m] @ w_in[:, offs[m]:offs[m + 1]] for m in range(C_N_MIX)]
    w_log = -jax.nn.softplus(-(w0 + jnp.tanh(w_lo) @ w2)) - 0.5
    decay = jnp.exp(-jnp.exp(w_log.astype(f32)))
    a = jax.nn.sigmoid(a0 + a_lo @ a2)

    def heads(t_):
        return t_.reshape(bsz, n_new, C_HEADS, C_HEAD_DIM).astype(f32)

    kk = heads(k * k_k)
    kk = kk / jnp.maximum(jnp.sqrt(jnp.sum(kk * kk, axis=-1, keepdims=True)), 1e-12)
    k = k * (1.0 + (a - 1.0) * k_a)
    rh, kh, vh, ah = heads(r), heads(k), heads(v), heads(a)
    y, s_last = wkv7_scan(s0.astype(f32), rh, heads(decay), kh, vh, -kk, kk * ah)
    mean = jnp.mean(y, axis=-1, keepdims=True)
    var = jnp.mean(jnp.square(y - mean), axis=-1, keepdims=True)
    y = (y - mean) * lax.rsqrt(var + C_LN_EPS) * lnx_g.reshape(C_HEADS, C_HEAD_DIM) + lnx_b.reshape(C_HEADS, C_HEAD_DIM)
    y = y + jnp.sum(rh * kh * r_k, axis=-1, keepdims=True) * vh
    y = y.reshape(bsz, n_new, D_MODEL)
    return (y * jax.nn.silu(z)) @ w_out, s_last, x[:, -1]


def run_trunk(x, past_cmp, past_sel, past_win, s5_re, s5_im, wkv, shift, ln_g, ln_b, a_par, b_par, c_par):
    new_cmp, new_sel, new_win, new_re, new_im, new_wkv, new_shift = [], [], [], [], [], [], []
    for i in range(DEPTH):
        j = i // N_MIXERS
        kind = i % N_MIXERS
        if kind == 0:
            y, kc, ks, kw = nsa_mixer(x, past_cmp[j], past_sel[j], past_win[j], *[p[j] for p in a_par])
            new_cmp.append(kc)
            new_sel.append(ks)
            new_win.append(kw)
        elif kind == 1:
            y, hr, hi = s5_mixer(x, s5_re[j], s5_im[j], *[p[j] for p in b_par])
            new_re.append(hr)
            new_im.append(hi)
        else:
            y, st, xl = rwkv7_mixer(x, wkv[j], shift[j], *[p[j] for p in c_par])
            new_wkv.append(st)
            new_shift.append(xl)
        x = layer_norm(DEEPNORM_ALPHA * x + y, ln_g[i], ln_b[i])
    return (x, jnp.stack(new_cmp), jnp.stack(new_sel), jnp.stack(new_win), jnp.stack(new_re),
            jnp.stack(new_im), jnp.stack(new_wkv), jnp.stack(new_shift))


def setup_inputs(seed: int = 0) -> dict:
    key = jax.random.key(seed)
    keys = iter(jax.random.split(key, 64))
    f32 = jnp.float32

    def nrm(shape, scale):
        return jax.random.normal(next(keys), shape, f32) * scale

    d = D_MODEL
    inv = d ** -0.5
    n_pages = PAST_LEN // PAGE_SIZE
    n_used = DEC_BATCH * n_pages
    n_phys = n_used + n_used // 4
    page_table = jax.random.permutation(next(keys), n_phys)[:n_used].reshape(DEC_BATCH, n_pages).astype(jnp.int32)
    kv_tail = (2, A_KV_HEADS, A_HEAD_DIM)
    n_buf = min(WINDOW, PAST_LEN)
    s5_im_base = math.pi * jnp.arange(S5_STATE, dtype=f32)
    w0_base = -6.0 + 5.0 * (jnp.arange(d, dtype=f32) / (d - 1)) ** 0.9
    return {
        'x_prompt': nrm((BATCH, SEQ, d), 1.0),
        'x_sample': nrm((DEC_BATCH, DEC_SEQ, d), 1.0),
        'cache_cmp_kv': nrm((N_A, n_phys, PAGE_SIZE) + kv_tail, 1.0),
        'cache_sel_kv': nrm((N_A, n_phys, PAGE_SIZE) + kv_tail, 1.0),
        'cache_win_kv': nrm((N_A, DEC_BATCH, n_buf) + kv_tail, 1.0),
        'state_s5_re': nrm((N_B, DEC_BATCH, S5_GROUPS, S5_STATE), 0.3),
        'state_s5_im': nrm((N_B, DEC_BATCH, S5_GROUPS, S5_STATE), 0.3),
        'state_wkv': nrm((N_C, DEC_BATCH, C_HEADS, C_HEAD_DIM, C_HEAD_DIM), 0.3),
        'state_shift': nrm((N_C, DEC_BATCH, d), 1.0),
        'page_table': page_table,
        'ln_g': 1.0 + nrm((DEPTH, d), 0.02),
        'ln_b': nrm((DEPTH, d), 0.02),
        'a_w_in': nrm((N_A, d, A_COLS), inv),
        'a_cmp_pe': nrm((N_A, 2, CMP_BLOCK, A_HEAD_DIM), 0.5),
        'a_cmp_w1': nrm((N_A, 2, CMP_BLOCK * A_HEAD_DIM, CMP_HIDDEN), (CMP_BLOCK * A_HEAD_DIM) ** -0.5),
        'a_cmp_w2': nrm((N_A, 2, CMP_HIDDEN, A_HEAD_DIM), 2.0 * CMP_HIDDEN ** -0.5),
        'a_w_out': nrm((N_A, A_ATT_WIDTH, d), DEEPNORM_BETA * A_ATT_WIDTH ** -0.5),
        'b_w_in': nrm((N_B, d, 2 * S5_WIDTH), inv),
        'b_log_dt': jax.random.uniform(next(keys), (N_B, S5_GROUPS), f32, math.log(1e-3), math.log(1e-1)),
        'b_a_re': -0.5 * (1.0 + nrm((N_B, S5_GROUPS, S5_STATE), 0.01)),
        'b_a_im': s5_im_base + nrm((N_B, S5_GROUPS, S5_STATE), 0.01),
        'b_b_re': nrm((N_B, S5_GROUPS, S5_STATE, S5_GROUP_CH), S5_GROUP_CH ** -0.5),
        'b_b_im': nrm((N_B, S5_GROUPS, S5_STATE, S5_GROUP_CH), S5_GROUP_CH ** -0.5),
        'b_c_re': nrm((N_B, S5_GROUPS, S5_GROUP_CH, S5_STATE), S5_STATE ** -0.5),
        'b_c_im': nrm((N_B, S5_GROUPS, S5_GROUP_CH, S5_STATE), S5_STATE ** -0.5),
        'b_d': nrm((N_B, S5_WIDTH), 1.0),
        'b_w_glu': nrm((N_B, S5_WIDTH, S5_WIDTH), S5_WIDTH ** -0.5),
        'b_b_glu': nrm((N_B, S5_WIDTH), 0.02),
        'b_w_out': nrm((N_B, S5_WIDTH, d), DEEPNORM_BETA * S5_WIDTH ** -0.5),
        'c_mu': jax.random.uniform(next(keys), (N_C, C_N_MIX, d), f32),
        'c_w_in': nrm((N_C, d, C_COLS), inv),
        'c_w0': w0_base + nrm((N_C, d), 0.1),
        'c_w2': nrm((N_C, C_LORA_W, d), 0.1),
        'c_a0': nrm((N_C, d), 0.1),
        'c_a2': nrm((N_C, C_LORA_A, d), 0.1),
        'c_k_k': 0.85 + nrm((N_C, d), 0.02),
        'c_k_a': 1.0 + nrm((N_C, d), 0.02),
        'c_r_k': nrm((N_C, C_HEADS, C_HEAD_DIM), 0.1),
        'c_lnx_g': 1.0 + nrm((N_C, d), 0.02),
        'c_lnx_b': nrm((N_C, d), 0.02),
        'c_w_out': nrm((N_C, d, d), DEEPNORM_BETA * inv),
    }


def reference(x_prompt, x_sample, cache_cmp_kv, cache_sel_kv, cache_win_kv, state_s5_re, state_s5_im,
              state_wkv, state_shift, page_table, ln_g, ln_b,
              a_w_in, a_cmp_pe, a_cmp_w1, a_cmp_w2, a_w_out,
              b_w_in, b_log_dt, b_a_re, b_a_im, b_b_re, b_b_im, b_c_re, b_c_im, b_d, b_w_glu, b_b_glu, b_w_out,
              c_mu, c_w_in, c_w0, c_w2, c_a0, c_a2, c_k_k, c_k_a, c_r_k, c_lnx_g, c_lnx_b, c_w_out):
    a_par = (a_w_in, a_cmp_pe, a_cmp_w1, a_cmp_w2, a_w_out)
    b_par = (b_w_in, b_log_dt, b_a_re, b_a_im, b_b_re, b_b_im, b_c_re, b_c_im, b_d, b_w_glu, b_b_glu, b_w_out)
    c_par = (c_mu, c_w_in, c_w0, c_w2, c_a0, c_a2, c_k_k, c_k_a, c_r_k, c_lnx_g, c_lnx_b, c_w_out)
    kv_tail = (2, A_KV_HEADS, A_HEAD_DIM)

    n_pr = x_prompt.shape[0]
    empty_kv = jnp.zeros((N_A, n_pr, 0) + kv_tail, x_prompt.dtype)
    zero_s5 = jnp.zeros((N_B, n_pr, S5_GROUPS, S5_STATE), jnp.float32)
    zero_wkv = jnp.zeros((N_C, n_pr, C_HEADS, C_HEAD_DIM, C_HEAD_DIM), jnp.float32)
    zero_shift = jnp.zeros((N_C, n_pr, D_MODEL), x_prompt.dtype)
    (y_prompt, p_cmp, p_sel, p_win, p_re, p_im, p_wkv, p_shift) = run_trunk(
        x_prompt, empty_kv, empty_kv, empty_kv, zero_s5, zero_s5, zero_wkv, zero_shift,
        ln_g, ln_b, a_par, b_par, c_par)

    n_dec = x_sample.shape[0]

    def gather_pages(pool):
        return pool[page_table].reshape((n_dec, -1) + kv_tail)

    past_cmp = [gather_pages(cache_cmp_kv[j]) for j in range(N_A)]
    past_sel = [gather_pages(cache_sel_kv[j]) for j in range(N_A)]
    (y_sample, s_cmp, s_sel, s_win, s_re, s_im, s_wkv, s_shift) = run_trunk(
        x_sample, past_cmp, past_sel, cache_win_kv, state_s5_re, state_s5_im, state_wkv, state_shift,
        ln_g, ln_b, a_par, b_par, c_par)
    return (y_prompt, y_sample, p_cmp, s_cmp, p_sel, s_sel, p_win, s_win,
            p_re, s_re, p_im, s_im, p_wkv, s_wkv, p_shift, s_shift)
```

```python
import functools
import math

import numpy as np
import jax
import jax.numpy as jnp
from jax import lax
from jax.experimental import pallas as pl
from jax.experimental.pallas import tpu as pltpu

F32 = jnp.float32
BF16 = jnp.bfloat16
HIGHEST = lax.Precision.HIGHEST

D_MODEL = 1024
DEPTH = 4
N_MIXERS = 3
PAGE_SIZE = 128

A_HEADS = 16
A_KV_HEADS = 4
A_HEAD_DIM = 64
A_GROUP = 4
CMP_BLOCK = 32
CMP_STRIDE = 16
CMP_RATIO = 2
CMP_HIDDEN = 128
SEL_BLOCK = 64
SEL_RATIO = 4
SEL_TOPK = 16
WINDOW = 512
A_KV_WIDTH = 2 * A_KV_HEADS * A_HEAD_DIM

S5_GROUP_CH = 16
S5_GROUPS = 64
S5_STATE = 64
S5_SETS = 4

C_HEAD_DIM = 64
C_HEADS = 16
C_LN_EPS = 64e-5
WKV_CHUNK = 64

LN_EPS = 1e-5
DEEPNORM_ALPHA = (2 * DEPTH) ** 0.25

NEG_BIG = -1e30
VMEM_LIMIT = 56 * 1024 * 1024


def _cparams(*sem):
    return pltpu.CompilerParams(dimension_semantics=sem, vmem_limit_bytes=VMEM_LIMIT)


def _dot(a, b):
    return jnp.dot(a, b, preferred_element_type=F32)


def _dot_nt(a, b):
    return lax.dot_general(a, b, (((1,), (1,)), ((), ())), preferred_element_type=F32)


def _dot_hi(a, b):
    return jnp.dot(a, b, preferred_element_type=F32, precision=HIGHEST)


def _dot_nt_hi(a, b):
    return lax.dot_general(a, b, (((1,), (1,)), ((), ())), preferred_element_type=F32, precision=HIGHEST)


def _split3(x):
    hi = x.astype(BF16)
    r1 = x - hi.astype(F32)
    mid = r1.astype(BF16)
    lo = (r1 - mid.astype(F32)).astype(BF16)
    return hi, mid, lo


def _dot3(x, m01):
    hi, mid, lo = _split3(x)
    return _dot(hi, m01) + _dot(mid, m01) + _dot(lo, m01)


def _silu(z):
    return z * jax.nn.sigmoid(z)


def _proj_kernel(x_ref, *refs, n_out):
    xb = x_ref[...].astype(BF16)
    for w_ref, o_ref in zip(refs[:n_out], refs[n_out:]):
        o_ref[...] = _dot(xb, w_ref[...])


def _multi_proj(x, ws, tm=256):
    m, k = x.shape
    tm = min(tm, m)
    n_out = len(ws)
    return pl.pallas_call(
        functools.partial(_proj_kernel, n_out=n_out),
        grid=(m // tm,),
        in_specs=[pl.BlockSpec((tm, k), lambda i: (i, 0))]
        + [pl.BlockSpec(w.shape, lambda i: (0, 0)) for w in ws],
        out_specs=[pl.BlockSpec((tm, w.shape[1]), lambda i: (i, 0)) for w in ws],
        out_shape=[jax.ShapeDtypeStruct((m, w.shape[1]), F32) for w in ws],
        compiler_params=_cparams("parallel"),
        name="multi_proj",
    )(x, *ws)


def _deepnorm(x, y, g, b):
    v = DEEPNORM_ALPHA * x + y
    mu = jnp.mean(v, axis=-1, keepdims=True)
    var = jnp.mean(jnp.square(v - mu), axis=-1, keepdims=True)
    return (v - mu) * lax.rsqrt(var + LN_EPS) * g + b


def _out_ln_kernel(o_ref, z_ref, x_ref, w_ref, g_ref, b_ref, out_ref):
    gated = (o_ref[...] * _silu(z_ref[...])).astype(BF16)
    y = _dot(gated, w_ref[...])
    out_ref[...] = _deepnorm(x_ref[...], y, g_ref[...], b_ref[...])


def _row_spec(tm, n):
    return pl.BlockSpec((tm, n), lambda i: (i, 0))


def _full_spec(shape):
    return pl.BlockSpec(shape, lambda i: (0,) * len(shape))


def _out_ln(o, z, x, w_out, g, b, tm=256):
    m, d = x.shape
    tm = min(tm, m)
    return pl.pallas_call(
        _out_ln_kernel,
        grid=(m // tm,),
        in_specs=[_row_spec(tm, d), _row_spec(tm, d), _row_spec(tm, d),
                  _full_spec(w_out.shape), _full_spec((1, d)), _full_spec((1, d))],
        out_specs=_row_spec(tm, d),
        out_shape=jax.ShapeDtypeStruct((m, d), F32),
        compiler_params=_cparams("parallel"),
        name="out_ln",
    )(o, z, x, w_out, g.reshape(1, d), b.reshape(1, d))


def _chunk_proj(x_ref, n_chunk, w1_ref):
    acc = [[None] * A_KV_HEADS for _ in range(2)]
    row_stride = CMP_STRIDE * 4
    for c in range(2):
        for s in range(CMP_STRIDE):
            w_r0 = w1_ref[c, pl.ds(s * 64, 64), :].astype(BF16)
            w_r1 = w1_ref[c, pl.ds((CMP_STRIDE + s) * 64, 64), :].astype(BF16)
            for pair in range(2):
                xs = x_ref[pl.ds(s * 4 + c * 2 + pair, n_chunk, stride=row_stride), :].astype(BF16)
                for half in range(2):
                    g = pair * 2 + half
                    xg = xs[:, half * 64:(half + 1) * 64]
                    t0 = _dot(xg, w_r0)
                    t1 = _dot(xg, w_r1)
                    if acc[c][g] is None:
                        acc[c][g] = (t0, t1)
                    else:
                        acc[c][g] = (acc[c][g][0] + t0, acc[c][g][1] + t1)
    return acc


def _compress_finish(acc, pe_ref, w1_ref, w2_ref, k_ref, v_ref, n_chunk):
    for c, o_ref in ((0, k_ref), (1, v_ref)):
        pe_term = _dot_hi(pe_ref[c], w1_ref[c])
        w2 = w2_ref[c].astype(BF16)
        for g in range(A_KV_HEADS):
            r0, r1 = acc[c][g]
            h = r0 + pltpu.roll(r1, n_chunk - 1, 0) + pe_term
            o_ref[g] = _dot(_silu(h).astype(BF16), w2)


def _compress_kernel(x_ref, pe_ref, w1_ref, w2_ref, k_ref, v_ref, *, n_chunk):
    acc = _chunk_proj(x_ref, n_chunk, w1_ref)
    _compress_finish(acc, pe_ref, w1_ref, w2_ref, k_ref, v_ref, n_chunk)


def _compress_prompt(kv_c, pe, w1, w2):
    bsz, t, _ = kv_c.shape
    n_chunk = t // CMP_STRIDE
    out = jax.ShapeDtypeStruct((bsz, A_KV_HEADS, n_chunk, A_HEAD_DIM), F32)
    out_spec = pl.BlockSpec((None, A_KV_HEADS, n_chunk, A_HEAD_DIM), lambda b: (b, 0, 0, 0))
    return pl.pallas_call(
        functools.partial(_compress_kernel, n_chunk=n_chunk),
        grid=(bsz,),
        in_specs=[pl.BlockSpec((None, t * 4, 128), lambda b: (b, 0, 0)),
                  _full_spec(pe.shape), _full_spec(w1.shape), _full_spec(w2.shape)],
        out_specs=[out_spec, out_spec],
        out_shape=[out, out],
        compiler_params=_cparams("parallel"),
        name="compress_prompt",
    )(kv_c.reshape(bsz, t * 4, 128), pe, w1, w2)


def _softmax_rows(s, valid):
    s = jnp.where(valid, s, NEG_BIG)
    m = jnp.max(s, axis=-1, keepdims=True)
    e = jnp.where(valid, jnp.exp(s - m), 0.0)
    return e / jnp.maximum(jnp.sum(e, axis=-1, keepdims=True), 1e-30)


def _topk_mask(score_t, n_keep):
    n_blk = score_t.shape[0]
    blk = lax.broadcasted_iota(jnp.int32, score_t.shape, 0)
    rank = jnp.zeros(score_t.shape, F32)
    for i in range(n_blk):
        row = score_t[i:i + 1, :]
        tie_ahead = jnp.where(blk > i, 1.0, 0.0)
        rank = rank + jnp.where(row > score_t, 1.0, jnp.where(row == score_t, tie_ahead, 0.0))
    return jnp.where(rank < n_keep, 1.0, 0.0)


def _nsa_prompt_kernel(q_ref, gate_ref, slope_ref, kc_ref, vc_ref, ks_ref, vs_ref, kw_ref, vw_ref,
                       sel_map_ref, o_ref, mask_ref, m_ref, l_ref, acc_ref, *, qb, kc, t_len):
    i = pl.program_id(2)
    t0 = i * qb
    rows = A_GROUP * qb
    n_cmp_pad = kc_ref.shape[0]
    n_cmp = n_cmp_pad - 1
    n_sel = t_len // SEL_BLOCK

    q = q_ref[...] * (A_HEAD_DIM ** -0.5)
    qg = jnp.concatenate([q[:, r * 64:(r + 1) * 64] for r in range(A_GROUP)], axis=0).astype(BF16)
    slope = slope_ref[...]
    t_row = t0 + lax.rem(lax.broadcasted_iota(jnp.int32, (rows, 1), 0), qb)

    n_idx = lax.broadcasted_iota(jnp.int32, (rows, n_cmp_pad), 1)
    dist_c = t_row - (n_idx * CMP_STRIDE + (CMP_BLOCK - 1))
    valid_c = (dist_c >= 0) & (n_idx < n_cmp)
    s_c = _dot_nt(qg, kc_ref[...].astype(BF16)) - slope * dist_c.astype(F32)
    p_c = _softmax_rows(s_c, valid_c)
    o_c = _dot(p_c.astype(BF16), vc_ref[...].astype(BF16))

    p_grp = p_c[0:qb] + p_c[qb:2 * qb] + p_c[2 * qb:3 * qb] + p_c[3 * qb:4 * qb]
    sel_map = sel_map_ref[...]
    hi, mid, lo = _split3(p_grp)
    p_slc_t = _dot_nt(sel_map, hi) + _dot_nt(sel_map, mid) + _dot_nt(sel_map, lo)
    blk = lax.broadcasted_iota(jnp.int32, (n_sel, qb), 0)
    cur = (t0 + lax.broadcasted_iota(jnp.int32, (n_sel, qb), 1)) // SEL_BLOCK
    forced = (blk == 0) | (blk == cur) | (blk == cur - 1)
    score_t = jnp.where(forced, 1e30, jnp.where(blk <= cur, p_slc_t, -1.0))
    keep_t = jnp.where(blk <= cur, _topk_mask(score_t, SEL_TOPK), 0.0)
    keep = keep_t.T.astype(BF16)
    key_blk = lax.broadcasted_iota(jnp.int32, (n_sel, t_len), 1) // SEL_BLOCK
    expand = jnp.where(key_blk == lax.broadcasted_iota(jnp.int32, (n_sel, t_len), 0), 1.0, 0.0).astype(BF16)
    mask_ref[...] = _dot(keep, expand)

    m_ref[...] = jnp.full(m_ref.shape, NEG_BIG, F32)
    l_ref[...] = jnp.zeros(l_ref.shape, F32)
    acc_ref[...] = jnp.zeros(acc_ref.shape, F32)
    col = lax.broadcasted_iota(jnp.int32, (rows, kc), 1)

    def chunk_body(c, carry):
        k0 = pl.multiple_of(c * kc, kc)
        k_blk = ks_ref[pl.ds(k0, kc), :]
        v_blk = vs_ref[pl.ds(k0, kc), :]
        dist = t_row - (k0 + col)
        sel = mask_ref[:, pl.ds(k0, kc)]
        sel = jnp.concatenate([sel] * A_GROUP, axis=0)
        valid = (sel > 0.5) & (dist >= 0)
        s = _dot_nt(qg, k_blk) - slope * dist.astype(F32)
        s = jnp.where(valid, s, NEG_BIG)
        m_old = m_ref[...]
        m_new = jnp.maximum(m_old, jnp.max(s, axis=-1, keepdims=True))
        alpha = jnp.exp(m_old - m_new)
        e = jnp.where(valid, jnp.exp(s - m_new), 0.0)
        l_ref[...] = alpha * l_ref[...] + jnp.sum(e, axis=-1, keepdims=True)
        acc_ref[...] = alpha * acc_ref[...] + _dot(e.astype(BF16), v_blk)
        m_ref[...] = m_new
        return carry

    n_chunks = (t0 + qb + kc - 1) // kc
    lax.fori_loop(0, n_chunks, chunk_body, 0)
    o_s = acc_ref[...] / jnp.maximum(l_ref[...], 1e-30)

    n_win = min(WINDOW + qb, t_len)
    w0 = pl.multiple_of(jnp.maximum(t0 + qb - n_win, 0), 8)
    k_win = kw_ref[pl.ds(w0, n_win), :]
    v_win = vw_ref[pl.ds(w0, n_win), :]
    dist_w = t_row - (w0 + lax.broadcasted_iota(jnp.int32, (rows, n_win), 1))
    valid_w = (dist_w >= 0) & (dist_w <= WINDOW)
    s_w = _dot_nt(qg, k_win) - slope * dist_w.astype(F32)
    p_w = _softmax_rows(s_w, valid_w)
    o_w = _dot(p_w.astype(BF16), v_win)

    gate = jax.nn.sigmoid(gate_ref[...])
    for r in range(A_GROUP):
        rs = slice(r * qb, (r + 1) * qb)
        o_ref[:, r * 64:(r + 1) * 64] = (gate[:, r:r + 1] * o_c[rs]
                                         + gate[:, A_GROUP + r:A_GROUP + r + 1] * o_s[rs]
                                         + gate[:, 2 * A_GROUP + r:2 * A_GROUP + r + 1] * o_w[rs])


def _head_major(kv, dtype):
    bsz, t, _ = kv.shape
    kv = kv.reshape(bsz, t, 2, A_KV_HEADS, A_HEAD_DIM).transpose(2, 0, 3, 1, 4).astype(dtype)
    return kv[0], kv[1]


def _group_gates(gate):
    bsz, t, _ = gate.shape
    return gate.reshape(bsz, t, 3, A_KV_HEADS, A_GROUP).transpose(0, 3, 1, 2, 4).reshape(bsz, A_KV_HEADS, t, 12)


def _sel_map(n_sel, n_cmp_pad, n_cmp):
    j = np.arange(n_sel)[:, None]
    n = np.arange(n_cmp_pad)[None, :]
    m = (n >= SEL_RATIO * j - (CMP_RATIO - 1)) & (n <= SEL_RATIO * j + SEL_RATIO - 1) & (n < n_cmp)
    return jnp.asarray(m, BF16)


def _slope_rows(slopes, qb):
    return jnp.repeat(slopes.reshape(A_KV_HEADS, A_GROUP), qb, axis=1)[..., None]


def _nsa_prompt_attn(q, gate, kv_s, kv_w, k_cmp, v_cmp, slopes, qb=128, kc=256):
    bsz, t, _ = q.shape
    n_cmp_pad = k_cmp.shape[2]
    n_sel = t // SEL_BLOCK
    ks, vs = _head_major(kv_s, BF16)
    kw, vw = _head_major(kv_w, BF16)
    rows = A_GROUP * qb
    kv_spec = pl.BlockSpec((None, None, t, A_HEAD_DIM), lambda b, g, i: (b, g, 0, 0))
    cmp_spec = pl.BlockSpec((None, None, n_cmp_pad, A_HEAD_DIM), lambda b, g, i: (b, g, 0, 0))
    return pl.pallas_call(
        functools.partial(_nsa_prompt_kernel, qb=qb, kc=kc, t_len=t),
        grid=(bsz, A_KV_HEADS, t // qb),
        in_specs=[pl.BlockSpec((None, qb, 256), lambda b, g, i: (b, i, g)),
                  pl.BlockSpec((None, None, qb, 12), lambda b, g, i: (b, g, i, 0)),
                  pl.BlockSpec((None, rows, 1), lambda b, g, i: (g, 0, 0)),
                  cmp_spec, cmp_spec, kv_spec, kv_spec, kv_spec, kv_spec,
                  pl.BlockSpec((n_sel, n_cmp_pad), lambda b, g, i: (0, 0))],
        out_specs=pl.BlockSpec((None, qb, 256), lambda b, g, i: (b, i, g)),
        out_shape=jax.ShapeDtypeStruct((bsz, t, D_MODEL), F32),
        scratch_shapes=[pltpu.VMEM((qb, t), F32), pltpu.VMEM((rows, 1), F32),
                        pltpu.VMEM((rows, 1), F32), pltpu.VMEM((rows, A_HEAD_DIM), F32)],
        compiler_params=_cparams("parallel", "parallel", "arbitrary"),
        name="nsa_prompt_attn",
    )(q, _group_gates(gate), _slope_rows(slopes, qb), k_cmp, v_cmp, ks, vs, kw, vw,
      _sel_map(n_sel, n_cmp_pad, n_cmp_pad - 1))


def _alibi_slopes():
    return 2.0 ** (-8.0 * jnp.arange(1, A_HEADS + 1, dtype=F32) / A_HEADS)


def _nsa_weights(w_in, pe, w1, w2, w_out):
    cuts = [int(c) for c in np.cumsum([D_MODEL, A_KV_WIDTH, A_KV_WIDTH, A_KV_WIDTH, 3 * A_HEADS])]
    pieces = jnp.split(w_in.astype(BF16), cuts, axis=-1)
    return dict(w_pieces=pieces, pe=pe.reshape(2, 1, CMP_BLOCK * A_HEAD_DIM), w1=w1, w2=w2,
                w_out=w_out.astype(BF16))


def _nsa_layer_prompt(x, wt, ln_g, ln_b):
    bsz, t, d = x.shape
    x2 = x.reshape(bsz * t, d)
    q, kv_c, kv_s, kv_w, gate, z = _multi_proj(x2, wt["w_pieces"])
    kv_c3, kv_s3, kv_w3 = (a.reshape(bsz, t, A_KV_WIDTH) for a in (kv_c, kv_s, kv_w))
    k_cmp, v_cmp = _compress_prompt(kv_c3, wt["pe"], wt["w1"], wt["w2"])
    o = _nsa_prompt_attn(q.reshape(bsz, t, d), gate.reshape(bsz, t, 3 * A_HEADS), kv_s3, kv_w3,
                         k_cmp, v_cmp, _alibi_slopes())
    x_new = _out_ln(o.reshape(bsz * t, d), z, x2, wt["w_out"], ln_g, ln_b)
    return x_new.reshape(bsz, t, d), kv_c3, kv_s3, kv_w3


def _page_copy(cache_ref, layer, page, buf_ref, slot, sem):
    rows = PAGE_SIZE * 4
    return pltpu.make_async_copy(cache_ref.at[layer, page], buf_ref.at[pl.ds(slot * rows, rows)], sem)


def _sample_compress_kernel(pt_ref, cache_ref, pe_ref, w1_ref, w2_ref, k_ref, v_ref, buf_ref, sem,
                            *, layer, n_pages):
    b = pl.program_id(0)
    for p in range(n_pages):
        _page_copy(cache_ref, layer, pt_ref[b * n_pages + p], buf_ref, p, sem).start()
    for p in range(n_pages):
        _page_copy(cache_ref, layer, pt_ref[b * n_pages + p], buf_ref, p, sem).wait()
    n_chunk = n_pages * PAGE_SIZE // CMP_STRIDE
    acc = _chunk_proj(buf_ref, n_chunk, w1_ref)
    _compress_finish(acc, pe_ref, w1_ref, w2_ref, k_ref, v_ref, n_chunk)


def _sample_compress(cache, layer, page_table, pe, w1, w2):
    n_layers, n_phys = cache.shape[:2]
    bsz, n_pages = page_table.shape
    n_chunk = n_pages * PAGE_SIZE // CMP_STRIDE
    cache4 = cache.reshape(n_layers, n_phys, PAGE_SIZE * 4, 128)
    out = jax.ShapeDtypeStruct((bsz, A_KV_HEADS, n_chunk, A_HEAD_DIM), F32)
    out_spec = pl.BlockSpec((None, A_KV_HEADS, n_chunk, A_HEAD_DIM), lambda b, pt: (b, 0, 0, 0))
    full = lambda a: pl.BlockSpec(a.shape, lambda b, pt: (0,) * a.ndim)
    grid_spec = pltpu.PrefetchScalarGridSpec(
        num_scalar_prefetch=1, grid=(bsz,),
        in_specs=[pl.BlockSpec(memory_space=pl.ANY), full(pe), full(w1), full(w2)],
        out_specs=[out_spec, out_spec],
        scratch_shapes=[pltpu.VMEM((n_pages * PAGE_SIZE * 4, 128), F32), pltpu.SemaphoreType.DMA(())])
    return pl.pallas_call(
        functools.partial(_sample_compress_kernel, layer=layer, n_pages=n_pages),
        grid_spec=grid_spec, out_shape=[out, out],
        compiler_params=_cparams("arbitrary"),
        name="sample_compress",
    )(page_table.reshape(-1), cache4, pe, w1, w2)


def _sample_cmp_kernel(q_ref, slope_ref, kc_ref, vc_ref, map_ref, oc_ref, idx_ref, *, past, n_cand):
    n_pad = kc_ref.shape[1]
    n_cmp = n_pad - 1
    n_idx = lax.broadcasted_iota(jnp.int32, (A_GROUP, n_pad), 1)
    dist = (past - (CMP_BLOCK - 1)) - n_idx * CMP_STRIDE
    valid = (dist >= 0) & (n_idx < n_cmp)
    q = q_ref[...] * (A_HEAD_DIM ** -0.5)
    n_keep = idx_ref.shape[1]
    width = map_ref.shape[1]
    lane = lax.broadcasted_iota(jnp.int32, (1, width), 1)
    row_i = lax.broadcasted_iota(jnp.int32, (width, width), 0)
    col_j = lax.broadcasted_iota(jnp.int32, (width, width), 1)
    for g in range(A_KV_HEADS):
        hs = slice(g * A_GROUP, (g + 1) * A_GROUP)
        s = _dot_nt(q[hs].astype(BF16), kc_ref[g].astype(BF16)) - slope_ref[hs] * dist.astype(F32)
        p = _softmax_rows(s, valid)
        oc_ref[hs, :] = _dot(p.astype(BF16), vc_ref[g].astype(BF16))
        p_grp = jnp.sum(p, axis=0, keepdims=True)
        p_slc = _dot3(p_grp, map_ref[...])
        forced = (lane == 0) | (lane == n_cand - 1)
        score = jnp.where(forced, 1e30, jnp.where(lane < n_cand, p_slc, -1.0))
        s_j = jnp.broadcast_to(score, (width, width))
        s_i = s_j.T
        tie_ahead = jnp.where(row_i < col_j, 1.0, 0.0)
        ahead = jnp.where(s_i > s_j, 1.0, jnp.where(s_i == s_j, tie_ahead, 0.0))
        rank = jnp.sum(ahead, axis=0, keepdims=True)
        want = lax.broadcasted_iota(jnp.int32, (n_keep, width), 0).astype(F32)
        picked = jnp.where(rank == want, lane.astype(F32), 0.0)
        idx_ref[g] = jnp.sum(picked, axis=-1, keepdims=True).astype(jnp.int32)


def _sample_cmp_topk(q, k_cmp, v_cmp, slopes, past):
    bsz = q.shape[0]
    n_pad = k_cmp.shape[2]
    n_cand = past // SEL_BLOCK
    width = -(-n_cand // 128) * 128
    j = np.arange(width)[None, :]
    n = np.arange(n_pad)[:, None]
    sel_map = (n >= SEL_RATIO * j - (CMP_RATIO - 1)) & (n <= SEL_RATIO * j + SEL_RATIO - 1) & (n < n_pad - 1)
    cmp_spec = pl.BlockSpec((None, A_KV_HEADS, n_pad, A_HEAD_DIM), lambda b: (b, 0, 0, 0))
    return pl.pallas_call(
        functools.partial(_sample_cmp_kernel, past=past, n_cand=n_cand),
        grid=(bsz,),
        in_specs=[pl.BlockSpec((None, A_HEADS, A_HEAD_DIM), lambda b: (b, 0, 0)),
                  _full_spec((A_HEADS, 1)), cmp_spec, cmp_spec, _full_spec((n_pad, width))],
        out_specs=[pl.BlockSpec((None, A_HEADS, A_HEAD_DIM), lambda b: (b, 0, 0)),
                   pl.BlockSpec((None, A_KV_HEADS, SEL_TOPK, 1), lambda b: (b, 0, 0, 0))],
        out_shape=[jax.ShapeDtypeStruct((bsz, A_HEADS, A_HEAD_DIM), F32),
                   jax.ShapeDtypeStruct((bsz, A_KV_HEADS, SEL_TOPK, 1), jnp.int32)],
        compiler_params=_cparams("parallel"),
        name="sample_cmp_topk",
    )(q, slopes.reshape(A_HEADS, 1), k_cmp, v_cmp, jnp.asarray(sel_map, BF16))


def _sel_block_copy(cache_ref, layer, pt_ref, idx_ref, b, g, r, n_pages, buf_ref, sem):
    blk = idx_ref[(b * A_KV_HEADS + g) * SEL_TOPK + r]
    page = pt_ref[b * n_pages + blk // (PAGE_SIZE // SEL_BLOCK)]
    row0 = pl.multiple_of((blk % (PAGE_SIZE // SEL_BLOCK)) * SEL_BLOCK, SEL_BLOCK)
    return pltpu.make_async_copy(cache_ref.at[layer, page, pl.ds(row0, SEL_BLOCK)],
                                 buf_ref.at[g * (SEL_TOPK - 1) + r], sem)


def _sample_attn_kernel(pt_ref, idx_ref, cache_ref, q_ref, gate_ref, slope_ref, oc_ref, ks_new_ref,
                        kw_new_ref, win_ref, o_ref, buf_ref, sem, *, layer, n_pages, past):
    b = pl.program_id(0)
    n_blk = SEL_TOPK - 1
    for g in range(A_KV_HEADS):
        for r in range(n_blk):
            _sel_block_copy(cache_ref, layer, pt_ref, idx_ref, b, g, r, n_pages, buf_ref, sem).start()

    q = q_ref[...] * (A_HEAD_DIM ** -0.5)
    gate = jax.nn.sigmoid(gate_ref[...])
    n_buf = win_ref.shape[0]
    pos_in_blk = lax.broadcasted_iota(jnp.int32, (1, SEL_BLOCK), 1)
    dist_w = (n_buf - lax.broadcasted_iota(jnp.int32, (1, n_buf), 1)).astype(F32)

    def attend(qg, slope, keys, vals, dist, k_new, v_new):
        s = _dot_nt(qg.astype(BF16), keys.astype(BF16)) - slope * dist
        s_new = jnp.sum(qg * k_new, axis=-1, keepdims=True)
        m = jnp.maximum(jnp.max(s, axis=-1, keepdims=True), s_new)
        e = jnp.exp(s - m)
        e_new = jnp.exp(s_new - m)
        l = jnp.sum(e, axis=-1, keepdims=True) + e_new
        return (_dot(e.astype(BF16), vals.astype(BF16)) + e_new * v_new) / l

    out_w = []
    for g in range(A_KV_HEADS):
        hs = slice(g * A_GROUP, (g + 1) * A_GROUP)
        ksl = slice(g * A_HEAD_DIM, (g + 1) * A_HEAD_DIM)
        vsl = slice(A_KV_WIDTH // 2 + g * A_HEAD_DIM, A_KV_WIDTH // 2 + (g + 1) * A_HEAD_DIM)
        out_w.append(attend(q[hs], slope_ref[hs], win_ref[:, ksl], win_ref[:, vsl], dist_w,
                            kw_new_ref[:, ksl], kw_new_ref[:, vsl]))

    for g in range(A_KV_HEADS):
        for r in range(n_blk):
            _sel_block_copy(cache_ref, layer, pt_ref, idx_ref, b, g, r, n_pages, buf_ref, sem).wait()

    for g in range(A_KV_HEADS):
        hs = slice(g * A_GROUP, (g + 1) * A_GROUP)
        ksl = slice(g * A_HEAD_DIM, (g + 1) * A_HEAD_DIM)
        vsl = slice(A_KV_WIDTH // 2 + g * A_HEAD_DIM, A_KV_WIDTH // 2 + (g + 1) * A_HEAD_DIM)
        keys = jnp.concatenate([buf_ref[g * n_blk + r, :, ksl] for r in range(n_blk)], axis=0)
        vals = jnp.concatenate([buf_ref[g * n_blk + r, :, vsl] for r in range(n_blk)], axis=0)
        dist = jnp.concatenate(
            [past - (idx_ref[(b * A_KV_HEADS + g) * SEL_TOPK + r] * SEL_BLOCK + pos_in_blk)
             for r in range(n_blk)], axis=1).astype(F32)
        o_s = attend(q[hs], slope_ref[hs], keys, vals, dist, ks_new_ref[:, ksl], ks_new_ref[:, vsl])
        gt = gate[hs]
        o_ref[hs, :] = gt[:, 0:1] * oc_ref[hs, :] + gt[:, 1:2] * o_s + gt[:, 2:3] * out_w[g]


def _sample_attn(page_table, idx, cache_sel, layer, q, gate, slopes, o_c, kv_s_new, kv_w_new, cache_win, past):
    bsz, n_pages = page_table.shape
    n_layers, n_phys = cache_sel.shape[:2]
    n_buf = cache_win.shape[2]
    cache3 = cache_sel.reshape(n_layers, n_phys, PAGE_SIZE, A_KV_WIDTH)
    head_spec = pl.BlockSpec((None, A_HEADS, A_HEAD_DIM), lambda b, pt, ix: (b, 0, 0))
    new_spec = pl.BlockSpec((None, 1, A_KV_WIDTH), lambda b, pt, ix: (b, 0, 0))
    grid_spec = pltpu.PrefetchScalarGridSpec(
        num_scalar_prefetch=2, grid=(bsz,),
        in_specs=[pl.BlockSpec(memory_space=pl.ANY), head_spec,
                  pl.BlockSpec((None, A_HEADS, 3), lambda b, pt, ix: (b, 0, 0)),
                  pl.BlockSpec((A_HEADS, 1), lambda b, pt, ix: (0, 0)),
                  head_spec, new_spec, new_spec,
                  pl.BlockSpec((None, None, n_buf, A_KV_WIDTH), lambda b, pt, ix: (layer, b, 0, 0))],
        out_specs=head_spec,
        scratch_shapes=[pltpu.VMEM((A_KV_HEADS * (SEL_TOPK - 1), SEL_BLOCK, A_KV_WIDTH), F32),
                        pltpu.SemaphoreType.DMA(())])
    return pl.pallas_call(
        functools.partial(_sample_attn_kernel, layer=layer, n_pages=n_pages, past=past),
        grid_spec=grid_spec,
        out_shape=jax.ShapeDtypeStruct((bsz, A_HEADS, A_HEAD_DIM), F32),
        compiler_params=_cparams("arbitrary"),
        name="sample_attn",
    )(page_table.reshape(-1), idx.reshape(-1), cache3, q, gate, slopes.reshape(A_HEADS, 1), o_c,
      kv_s_new, kv_w_new, cache_win)


def _nsa_layer_sample(x, wt, ln_g, ln_b, layer, cache_cmp, cache_sel, cache_win, page_table):
    bsz, _, d = x.shape
    n_pages = page_table.shape[1]
    past = n_pages * PAGE_SIZE
    x2 = x.reshape(bsz, d)
    q, kv_c, kv_s, kv_w, gate, z = _multi_proj(x2, wt["w_pieces"])
    slopes = _alibi_slopes()
    k_cmp, v_cmp = _sample_compress(cache_cmp, layer, page_table, wt["pe"], wt["w1"], wt["w2"])
    q3 = q.reshape(bsz, A_HEADS, A_HEAD_DIM)
    o_c, idx = _sample_cmp_topk(q3, k_cmp, v_cmp, slopes, past)
    gate3 = gate.reshape(bsz, 3, A_HEADS).transpose(0, 2, 1)
    kv_s3, kv_w3 = kv_s.reshape(bsz, 1, A_KV_WIDTH), kv_w.reshape(bsz, 1, A_KV_WIDTH)
    n_layers, _, n_buf = cache_win.shape[:3]
    win4 = cache_win.reshape(n_layers, bsz, n_buf, A_KV_WIDTH)
    o = _sample_attn(page_table, idx, cache_sel, layer, q3, gate3, slopes, o_c, kv_s3, kv_w3, win4, past)
    x_new = _out_ln(o.reshape(bsz, d), z, x2, wt["w_out"], ln_g, ln_b)
    n_keep = min(WINDOW, n_buf + 1)
    win_new = jnp.concatenate([win4[layer][:, n_buf + 1 - n_keep:], kv_w3], axis=1)
    return x_new.reshape(bsz, 1, d), kv_c.reshape(bsz, 1, A_KV_WIDTH), kv_s3, win_new


def _s5_param_kernel(log_dt_ref, ar_ref, ai_ref, br_ref, bi_ref, tile_ref,
                     abr_ref, abi_ref, bbr_ref, bbi_ref):
    dt = jnp.exp(log_dt_ref[...])
    ar, ai = ar_ref[...], ai_ref[...]
    mag = jnp.exp(dt * ar)
    abr, abi = mag * jnp.cos(dt * ai), mag * jnp.sin(dt * ai)
    den = ar * ar + ai * ai
    num_re, num_im = abr - 1.0, abi
    zoh_re = (num_re * ar + num_im * ai) / den
    zoh_im = (num_im * ar - num_re * ai) / den
    abr_ref[...] = abr
    abi_ref[...] = abi
    zr = _dot3(zoh_re, tile_ref[...])
    zi = _dot3(zoh_im, tile_ref[...])
    br, bi = br_ref[...], bi_ref[...]
    bbr_ref[...] = zr * br - zi * bi
    bbi_ref[...] = zr * bi + zi * br


def _s5_weights(w_in, log_dt, a_re, a_im, b_re, b_im, c_re, c_im, d_skip, w_glu, b_glu, w_out):
    g, p, c = S5_GROUPS, S5_STATE, S5_GROUP_CH
    tile = jnp.asarray(np.tile(np.eye(p, dtype=np.float32), (1, c)), BF16)
    to_gcp = lambda b: b.transpose(0, 2, 1).reshape(g, c * p)
    shapes = [jax.ShapeDtypeStruct((g, p), F32)] * 2 + [jax.ShapeDtypeStruct((g, c * p), F32)] * 2
    abr, abi, bbr, bbi = pl.pallas_call(_s5_param_kernel, out_shape=shapes, name="s5_params")(
        log_dt.reshape(g, 1), a_re, a_im, to_gcp(b_re), to_gcp(b_im), tile)
    eye = jnp.eye(S5_GROUPS // S5_SETS, dtype=F32)
    gl = S5_GROUPS // S5_SETS

    def in_blockdiag(bb):
        bb = bb.reshape(S5_SETS, gl, c, p)
        return jnp.einsum('sgcp,gh->sgchp', bb, eye).reshape(S5_SETS, gl * c, gl * p).astype(BF16)

    def out_blockdiag(cc):
        cc = cc.reshape(S5_SETS, gl, c, p)
        return jnp.einsum('sgcp,gh->sgphc', cc, eye).reshape(S5_SETS, gl * p, gl * c).astype(BF16)

    w_u, w_z = jnp.split(w_in.astype(BF16), 2, axis=-1)
    return dict(w_pieces=[w_u, w_z], abr=abr.reshape(1, g * p), abi=abi.reshape(1, g * p),
                wb_re=in_blockdiag(bbr), wb_im=in_blockdiag(bbi),
                wc_re=out_blockdiag(c_re), wc_im=out_blockdiag(c_im),
                d_skip=d_skip.reshape(1, -1), w_glu=w_glu.astype(BF16), b_glu=b_glu.reshape(1, -1),
                w_out=w_out.astype(BF16))


def _s5_scan_kernel(u_ref, wbr_ref, wbi_ref, ar_ref, ai_ref, wcr_ref, wci_ref, h0r_ref, h0i_ref,
                    y_ref, hr_out, hi_out, bur, bui, hr_c, hi_c, *, tc):
    @pl.when(pl.program_id(1) == 0)
    def _():
        hr_c[...] = h0r_ref[...]
        hi_c[...] = h0i_ref[...]

    n_set = S5_GROUPS * S5_STATE // S5_SETS
    n_ch = S5_GROUPS * S5_GROUP_CH // S5_SETS
    ub = u_ref[...].astype(BF16)
    for s in range(S5_SETS):
        us = ub[:, s * n_ch:(s + 1) * n_ch]
        bur[:, s * n_set:(s + 1) * n_set] = _dot(us, wbr_ref[s])
        bui[:, s * n_set:(s + 1) * n_set] = _dot(us, wbi_ref[s])

    for s in range(S5_SETS):
        sl = pl.ds(s * n_set, n_set)
        ar, ai = ar_ref[:, sl], ai_ref[:, sl]

        def step(t, carry):
            hr, hi = carry
            nr = ar * hr - ai * hi + bur[pl.ds(t, 1), sl]
            ni = ar * hi + ai * hr + bui[pl.ds(t, 1), sl]
            bur[pl.ds(t, 1), sl] = nr
            bui[pl.ds(t, 1), sl] = ni
            return nr, ni

        hr, hi = lax.fori_loop(0, tc, step, (hr_c[:, sl], hi_c[:, sl]))
        hr_c[:, sl] = hr
        hi_c[:, sl] = hi

    for s in range(S5_SETS):
        h_re = bur[:, s * n_set:(s + 1) * n_set]
        h_im = bui[:, s * n_set:(s + 1) * n_set]
        re_hi = h_re.astype(BF16)
        re_lo = (h_re - re_hi.astype(F32)).astype(BF16)
        im_hi = h_im.astype(BF16)
        im_lo = (h_im - im_hi.astype(F32)).astype(BF16)
        y = (_dot(re_hi, wcr_ref[s]) + _dot(re_lo, wcr_ref[s])
             - _dot(im_hi, wci_ref[s]) - _dot(im_lo, wci_ref[s]))
        y_ref[:, s * n_ch:(s + 1) * n_ch] = y
    hr_out[...] = hr_c[...]
    hi_out[...] = hi_c[...]


def _s5_scan(u, h0_re, h0_im, wt, tc=256):
    bsz, t, d = u.shape
    tc = min(tc, t)
    n_state = S5_GROUPS * S5_STATE
    st_spec = pl.BlockSpec((None, 1, n_state), lambda b, j: (b, 0, 0))
    full = lambda a: pl.BlockSpec(a.shape, lambda b, j: (0,) * a.ndim)
    st_shape = jax.ShapeDtypeStruct((bsz, 1, n_state), F32)
    return pl.pallas_call(
        functools.partial(_s5_scan_kernel, tc=tc),
        grid=(bsz, t // tc),
        in_specs=[pl.BlockSpec((None, tc, d), lambda b, j: (b, j, 0)),
                  full(wt["wb_re"]), full(wt["wb_im"]), full(wt["abr"]), full(wt["abi"]),
                  full(wt["wc_re"]), full(wt["wc_im"]), st_spec, st_spec],
        out_specs=[pl.BlockSpec((None, tc, d), lambda b, j: (b, j, 0)), st_spec, st_spec],
        out_shape=[jax.ShapeDtypeStruct((bsz, t, d), F32), st_shape, st_shape],
        scratch_shapes=[pltpu.VMEM((tc, n_state), F32), pltpu.VMEM((tc, n_state), F32),
                        pltpu.VMEM((1, n_state), F32), pltpu.VMEM((1, n_state), F32)],
        compiler_params=_cparams("parallel", "arbitrary"),
        name="s5_scan",
    )(u, wt["wb_re"], wt["wb_im"], wt["abr"], wt["abi"], wt["wc_re"], wt["wc_im"], h0_re, h0_im)


def _s5_out_kernel(y_ref, u_ref, z_ref, x_ref, d_ref, wg_ref, bg_ref, w_ref, g_ref, b_ref, out_ref):
    y = jax.nn.gelu(y_ref[...] + d_ref[...] * u_ref[...])
    y = y * jax.nn.sigmoid(_dot(y.astype(BF16), wg_ref[...]) + bg_ref[...])
    out = _dot((y * _silu(z_ref[...])).astype(BF16), w_ref[...])
    out_ref[...] = _deepnorm(x_ref[...], out, g_ref[...], b_ref[...])


def _s5_out(y, u, z, x, wt, g, b, tm=256):
    m, d = x.shape
    tm = min(tm, m)
    vec = _full_spec((1, d))
    return pl.pallas_call(
        _s5_out_kernel,
        grid=(m // tm,),
        in_specs=[_row_spec(tm, d)] * 4 + [vec, _full_spec((d, d)), vec, _full_spec((d, d)), vec, vec],
        out_specs=_row_spec(tm, d),
        out_shape=jax.ShapeDtypeStruct((m, d), F32),
        compiler_params=_cparams("parallel"),
        name="s5_out",
    )(y, u, z, x, wt["d_skip"], wt["w_glu"], wt["b_glu"], wt["w_out"], g.reshape(1, d), b.reshape(1, d))


def _s5_layer(x, h0_re, h0_im, wt, ln_g, ln_b):
    bsz, t, d = x.shape
    x2 = x.reshape(bsz * t, d)
    u, z = _multi_proj(x2, wt["w_pieces"])
    n_state = S5_GROUPS * S5_STATE
    y, hr, hi = _s5_scan(u.reshape(bsz, t, d), h0_re.reshape(bsz, 1, n_state),
                         h0_im.reshape(bsz, 1, n_state), wt)
    x_new = _s5_out(y.reshape(bsz * t, d), u, z, x2, wt, ln_g, ln_b)
    st = lambda h: h.reshape(bsz, S5_GROUPS, S5_STATE)
    return x_new.reshape(bsz, t, d), st(hr), st(hi)


def _head_sum(x, seg, seg_t):
    return _dot3(_dot3(x, seg), seg_t)


def _softplus(x):
    return jnp.maximum(x, 0.0) + jnp.log(1.0 + jnp.exp(-jnp.abs(x)))


def _rwkv_proj_kernel(x_ref, xp_ref, mu_ref, wr_ref, wwl_ref, wk_ref, wv_ref, wal_ref, wz_ref,
                      w0_ref, w2_ref, a0_ref, a2_ref, kk_ref, ka_ref, seg_ref, segt_ref,
                      r_ref, lw_ref, k_ref, v_ref, kkn_ref, a_ref, z_ref):
    x = x_ref[...]
    dx = xp_ref[...] - x
    mix = lambda m: (x + dx * mu_ref[m:m + 1, :]).astype(BF16)
    r_ref[...] = _dot(mix(0), wr_ref[...])
    w_lo = _dot(mix(1), wwl_ref[...])
    k = _dot(mix(2), wk_ref[...])
    v_ref[...] = _dot(mix(3), wv_ref[...])
    a_lo = _dot(mix(4), wal_ref[...])
    z_ref[...] = _dot(mix(5), wz_ref[...])
    w_log = -_softplus(-(w0_ref[...] + _dot(jnp.tanh(w_lo).astype(BF16), w2_ref[...]))) - 0.5
    lw_ref[...] = -jnp.exp(w_log)
    a = jax.nn.sigmoid(a0_ref[...] + _dot(a_lo.astype(BF16), a2_ref[...]))
    a_ref[...] = a
    kk = k * kk_ref[...]
    norm = jnp.sqrt(_head_sum(kk * kk, seg_ref[...], segt_ref[...]))
    kkn_ref[...] = kk / jnp.maximum(norm, 1e-12)
    k_ref[...] = k * (1.0 + (a - 1.0) * ka_ref[...])


def _head_seg():
    seg = np.zeros((D_MODEL, 128), np.float32)
    seg[np.arange(D_MODEL), np.arange(D_MODEL) // C_HEAD_DIM] = 1.0
    return jnp.asarray(seg, BF16), jnp.asarray(seg.T.copy(), BF16)


def _rwkv_weights(mu, w_in, w0, w2, a0, a2, k_k, k_a, r_k, lnx_g, lnx_b, w_out):
    d = D_MODEL
    lora_w, lora_a = w2.shape[0], a2.shape[0]
    cuts = [int(c) for c in np.cumsum([d, lora_w, d, d, lora_a])]
    pieces = jnp.split(w_in.astype(BF16), cuts, axis=-1)
    seg, seg_t = _head_seg()
    row = lambda a: a.reshape(1, d)
    return dict(mu=mu, pieces=pieces, w0=row(w0), w2=w2.astype(BF16), a0=row(a0), a2=a2.astype(BF16),
                k_k=row(k_k), k_a=row(k_a), r_k=row(r_k), lnx_g=row(lnx_g), lnx_b=row(lnx_b),
                w_out=w_out.astype(BF16), seg=seg, seg_t=seg_t)


def _rwkv_proj(x, x_prev, wt, tm=256):
    m, d = x.shape
    tm = min(tm, m)
    ins = [x, x_prev, wt["mu"], *wt["pieces"], wt["w0"], wt["w2"], wt["a0"], wt["a2"],
           wt["k_k"], wt["k_a"], wt["seg"], wt["seg_t"]]
    in_specs = [_row_spec(tm, d), _row_spec(tm, d)] + [_full_spec(a.shape) for a in ins[2:]]
    return pl.pallas_call(
        _rwkv_proj_kernel,
        grid=(m // tm,),
        in_specs=in_specs,
        out_specs=[_row_spec(tm, d)] * 7,
        out_shape=[jax.ShapeDtypeStruct((m, d), F32)] * 7,
        compiler_params=_cparams("parallel"),
        name="rwkv_proj",
    )(*ins)


def _dot_tn_hi(a, b):
    return lax.dot_general(a, b, (((0,), (0,)), ((), ())), preferred_element_type=F32, precision=HIGHEST)


def _wkv_chunk_kernel(r_ref, lw_ref, k_ref, v_ref, kkn_ref, a_ref, y_ref, s_out, s_ref, *, chunk):
    @pl.when(pl.program_id(1) == 0)
    def _():
        s_ref[...] = jnp.zeros(s_ref.shape, F32)

    n = chunk
    lw = lw_ref[...]
    row = lax.broadcasted_iota(jnp.int32, (n, n), 0)
    col = lax.broadcasted_iota(jnp.int32, (n, n), 1)
    strict = row > col
    incl = row >= col
    cum = _dot_hi(jnp.where(incl, 1.0, 0.0), lw)
    tot = cum[n - 1:n, :]
    kkn = kkn_ref[...]
    b_vec = kkn * a_ref[...]
    e_neg = jnp.exp(-cum)
    e_tail = jnp.exp(tot - cum)
    a_t = -kkn * jnp.exp(cum - lw)
    r_t = r_ref[...] * jnp.exp(cum)
    k = k_ref[...]
    b_t, k_t = b_vec * e_neg, k * e_neg
    b_h, k_h = b_vec * e_tail, k * e_tail
    p_tot = jnp.exp(tot)
    v = v_ref[...]
    eye = jnp.where(row == col, 1.0, 0.0)
    n_sq = int(math.log2(n)) - 1

    for h in range(C_HEADS):
        sl = slice(h * C_HEAD_DIM, (h + 1) * C_HEAD_DIM)
        at, rt, bt, kt, vh = a_t[:, sl], r_t[:, sl], b_t[:, sl], k_t[:, sl], v[:, sl]
        s0 = s_ref[h]
        a_ab = jnp.where(strict, _dot_nt_hi(at, bt), 0.0)
        a_ak = jnp.where(strict, _dot_nt_hi(at, kt), 0.0)
        a_rb = jnp.where(incl, _dot_nt_hi(rt, bt), 0.0)
        a_rk = jnp.where(incl, _dot_nt_hi(rt, kt), 0.0)
        inv = eye + a_ab
        pw = a_ab
        for _ in range(n_sq):
            pw = _dot_hi(pw, pw)
            inv = inv + _dot_hi(inv, pw)
        u = _dot_hi(inv, _dot_nt_hi(at, s0) + _dot_hi(a_ak, vh))
        y_ref[:, sl] = _dot_nt_hi(rt, s0) + _dot_hi(a_rb, u) + _dot_hi(a_rk, vh)
        s_ref[h] = s0 * p_tot[:, sl] + _dot_tn_hi(u, b_h[:, sl]) + _dot_tn_hi(vh, k_h[:, sl])
    s_out[...] = s_ref[...]


def _wkv_chunked(r, lw, k, v, kkn, a, chunk=WKV_CHUNK):
    bsz, t, d = r.shape
    spec = pl.BlockSpec((None, chunk, d), lambda b, j: (b, j, 0))
    s_spec = pl.BlockSpec((None, C_HEADS, C_HEAD_DIM, C_HEAD_DIM), lambda b, j: (b, 0, 0, 0))
    return pl.pallas_call(
        functools.partial(_wkv_chunk_kernel, chunk=chunk),
        grid=(bsz, t // chunk),
        in_specs=[spec] * 6,
        out_specs=[spec, s_spec],
        out_shape=[jax.ShapeDtypeStruct((bsz, t, d), F32),
                   jax.ShapeDtypeStruct((bsz, C_HEADS, C_HEAD_DIM, C_HEAD_DIM), F32)],
        scratch_shapes=[pltpu.VMEM((C_HEADS, C_HEAD_DIM, C_HEAD_DIM), F32)],
        compiler_params=_cparams("parallel", "arbitrary"),
        name="wkv_chunked",
    )(r, lw, k, v, kkn, a)


def _wkv_step_kernel(s_ref, r_ref, lw_ref, k_ref, kkn_ref, a_ref, v_ref, y_ref, s_out):
    s0 = s_ref[...]
    kkn = kkn_ref[...]
    sa = jnp.sum(s0 * (-kkn), axis=-1, keepdims=True)
    s1 = s0 * jnp.exp(lw_ref[...]) + sa * (kkn * a_ref[...]) + v_ref[...] * k_ref[...]
    s_out[...] = s1
    y_ref[...] = jnp.sum(s1 * r_ref[...], axis=-1, keepdims=True)


def _wkv_step(s0, r, lw, k, v, kkn, a):
    bsz = s0.shape[0]
    rowv = lambda x: x.reshape(bsz, C_HEADS, 1, C_HEAD_DIM)
    colv = lambda x: x.reshape(bsz, C_HEADS, C_HEAD_DIM, 1)
    s_spec = pl.BlockSpec((None, C_HEADS, C_HEAD_DIM, C_HEAD_DIM), lambda b: (b, 0, 0, 0))
    r_spec = pl.BlockSpec((None, C_HEADS, 1, C_HEAD_DIM), lambda b: (b, 0, 0, 0))
    c_spec = pl.BlockSpec((None, C_HEADS, C_HEAD_DIM, 1), lambda b: (b, 0, 0, 0))
    y, s1 = pl.pallas_call(
        _wkv_step_kernel,
        grid=(bsz,),
        in_specs=[s_spec] + [r_spec] * 5 + [c_spec],
        out_specs=[c_spec, s_spec],
        out_shape=[jax.ShapeDtypeStruct((bsz, C_HEADS, C_HEAD_DIM, 1), F32),
                   jax.ShapeDtypeStruct(s0.shape, F32)],
        compiler_params=_cparams("parallel"),
        name="wkv_step",
    )(s0, rowv(r), rowv(lw), rowv(k), rowv(kkn), rowv(a), colv(v))
    return y.reshape(bsz, D_MODEL), s1


def _rwkv_out_kernel(y_ref, r_ref, k_ref, v_ref, z_ref, x_ref, rk_ref, lg_ref, lb_ref, seg_ref, segt_ref,
                     w_ref, g_ref, b_ref, out_ref):
    seg, seg_t = seg_ref[...], segt_ref[...]
    y = y_ref[...]
    inv_n = 1.0 / C_HEAD_DIM
    yc = y - _head_sum(y, seg, seg_t) * inv_n
    var = _head_sum(yc * yc, seg, seg_t) * inv_n
    yn = yc * lax.rsqrt(var + C_LN_EPS) * lg_ref[...] + lb_ref[...]
    v = v_ref[...]
    yy = yn + _head_sum(r_ref[...] * k_ref[...] * rk_ref[...], seg, seg_t) * v
    out = _dot((yy * _silu(z_ref[...])).astype(BF16), w_ref[...])
    out_ref[...] = _deepnorm(x_ref[...], out, g_ref[...], b_ref[...])


def _rwkv_out(y, r, k, v, z, x, wt, g, b, tm=256):
    m, d = x.shape
    tm = min(tm, m)
    vec = _full_spec((1, d))
    return pl.pallas_call(
        _rwkv_out_kernel,
        grid=(m // tm,),
        in_specs=[_row_spec(tm, d)] * 6 + [vec, vec, vec, _full_spec(wt["seg"].shape),
                                           _full_spec(wt["seg_t"].shape), _full_spec((d, d)), vec, vec],
        out_specs=_row_spec(tm, d),
        out_shape=jax.ShapeDtypeStruct((m, d), F32),
        compiler_params=_cparams("parallel"),
        name="rwkv_out",
    )(y, r, k, v, z, x, wt["r_k"], wt["lnx_g"], wt["lnx_b"], wt["seg"], wt["seg_t"], wt["w_out"],
      g.reshape(1, d), b.reshape(1, d))


def _rwkv_layer(x, s0, x_last, wt, ln_g, ln_b):
    bsz, t, d = x.shape
    x_prev = jnp.concatenate([x_last[:, None, :], x[:, :-1]], axis=1)
    x2 = x.reshape(bsz * t, d)
    r, lw, k, v, kkn, a, z = _rwkv_proj(x2, x_prev.reshape(bsz * t, d), wt)
    if s0 is None:
        b3 = lambda u: u.reshape(bsz, t, d)
        y, s1 = _wkv_chunked(b3(r), b3(lw), b3(k), b3(v), b3(kkn), b3(a))
        y = y.reshape(bsz * t, d)
    else:
        y, s1 = _wkv_step(s0, r, lw, k, v, kkn, a)
    x_new = _rwkv_out(y, r, k, v, z, x2, wt, ln_g, ln_b)
    return x_new.reshape(bsz, t, d), s1, x[:, -1]


def _run_trunk(x, ln_g, ln_b, a_wts, b_wts, c_wts, past=None):
    bsz = x.shape[0]
    new = {k: [] for k in ("cmp", "sel", "win", "re", "im", "wkv", "shift")}
    for i in range(DEPTH):
        j, kind = divmod(i, N_MIXERS)
        if kind == 0:
            if past is None:
                x, kc, ks, kw = _nsa_layer_prompt(x, a_wts[j], ln_g[i], ln_b[i])
                kw = kw[:, kw.shape[1] - min(WINDOW, kw.shape[1]):]
            else:
                x, kc, ks, kw = _nsa_layer_sample(x, a_wts[j], ln_g[i], ln_b[i], j, past["cmp"],
                                                  past["sel"], past["win"], past["page_table"])
            new["cmp"].append(kc)
            new["sel"].append(ks)
            new["win"].append(kw)
        elif kind == 1:
            if past is None:
                h0r = h0i = jnp.zeros((bsz, S5_GROUPS, S5_STATE), F32)
            else:
                h0r, h0i = past["s5_re"][j], past["s5_im"][j]
            x, hr, hi = _s5_layer(x, h0r, h0i, b_wts[j], ln_g[i], ln_b[i])
            new["re"].append(hr)
            new["im"].append(hi)
        else:
            if past is None:
                s0, x_last = None, jnp.zeros((bsz, D_MODEL), F32)
            else:
                s0, x_last = past["wkv"][j], past["shift"][j]
            x, s1, xl = _rwkv_layer(x, s0, x_last, c_wts[j], ln_g[i], ln_b[i])
            new["wkv"].append(s1)
            new["shift"].append(xl)
    kv_tail = (2, A_KV_HEADS, A_HEAD_DIM)
    kv = lambda rows: jnp.stack(rows).reshape((len(rows),) + rows[0].shape[:2] + kv_tail)
    return (x, kv(new["cmp"]), kv(new["sel"]), kv(new["win"]), jnp.stack(new["re"]), jnp.stack(new["im"]),
            jnp.stack(new["wkv"]), jnp.stack(new["shift"]))


def kernel(x_prompt, x_sample, cache_cmp_kv, cache_sel_kv, cache_win_kv, state_s5_re, state_s5_im,
           state_wkv, state_shift, page_table, ln_g, ln_b,
           a_w_in, a_cmp_pe, a_cmp_w1, a_cmp_w2, a_w_out,
           b_w_in, b_log_dt, b_a_re, b_a_im, b_b_re, b_b_im, b_c_re, b_c_im, b_d, b_w_glu, b_b_glu, b_w_out,
           c_mu, c_w_in, c_w0, c_w2, c_a0, c_a2, c_k_k, c_k_a, c_r_k, c_lnx_g, c_lnx_b, c_w_out):
    a_par = (a_w_in, a_cmp_pe, a_cmp_w1, a_cmp_w2, a_w_out)
    b_par = (b_w_in, b_log_dt, b_a_re, b_a_im, b_b_re, b_b_im, b_c_re, b_c_im, b_d, b_w_glu, b_b_glu, b_w_out)
    c_par = (c_mu, c_w_in, c_w0, c_w2, c_a0, c_a2, c_k_k, c_k_a, c_r_k, c_lnx_g, c_lnx_b, c_w_out)
    a_wts = [_nsa_weights(*[p[j] for p in a_par]) for j in range(a_w_in.shape[0])]
    b_wts = [_s5_weights(*[p[j] for p in b_par]) for j in range(b_w_in.shape[0])]
    c_wts = [_rwkv_weights(*[p[j] for p in c_par]) for j in range(c_w_in.shape[0])]

    (y_p, p_cmp, p_sel, p_win, p_re, p_im, p_wkv, p_shift) = _run_trunk(
        x_prompt, ln_g, ln_b, a_wts, b_wts, c_wts)
    past = dict(cmp=cache_cmp_kv, sel=cache_sel_kv, win=cache_win_kv, page_table=page_table,
                s5_re=state_s5_re, s5_im=state_s5_im, wkv=state_wkv, shift=state_shift)
    (y_s, s_cmp, s_sel, s_win, s_re, s_im, s_wkv, s_shift) = _run_trunk(
        x_sample, ln_g, ln_b, a_wts, b_wts, c_wts, past)
    return (y_p, y_s, p_cmp, s_cmp, p_sel, s_sel, p_win, s_win,
            p_re, s_re, p_im, s_im, p_wkv, s_wkv, p_shift, s_shift)
```

```python
import functools
import math

import numpy as np
import jax
import jax.numpy as jnp
from jax import lax
from jax.experimental import pallas as pl
from jax.experimental.pallas import tpu as pltpu

F32 = jnp.float32
BF16 = jnp.bfloat16
HIGHEST = lax.Precision.HIGHEST

D_MODEL = 1024
DEPTH = 4
N_MIXERS = 3
PAGE_SIZE = 128

A_HEADS = 16
A_KV_HEADS = 4
A_HEAD_DIM = 64
A_GROUP = 4
CMP_BLOCK = 32
CMP_STRIDE = 16
CMP_RATIO = 2
CMP_HIDDEN = 128
SEL_BLOCK = 64
SEL_RATIO = 4
SEL_TOPK = 16
WINDOW = 512
A_KV_WIDTH = 2 * A_KV_HEADS * A_HEAD_DIM

S5_GROUP_CH = 16
S5_GROUPS = 64
S5_STATE = 64
S5_SETS = 4

C_HEAD_DIM = 64
C_HEADS = 16
C_LN_EPS = 64e-5
WKV_CHUNK = 64

LN_EPS = 1e-5
DEEPNORM_ALPHA = (2 * DEPTH) ** 0.25

NEG_BIG = -1e30
LOG2E = 1.4426950408889634
POS_FEATURES = 6
MAX_POS = 64 * 256
VMEM_LIMIT = 56 * 1024 * 1024


def _cparams(*sem):
    return pltpu.CompilerParams(dimension_semantics=sem, vmem_limit_bytes=VMEM_LIMIT)


def _dot(a, b):
    return jnp.dot(a, b, preferred_element_type=F32)


def _dot_nt(a, b):
    return lax.dot_general(a, b, (((1,), (1,)), ((), ())), preferred_element_type=F32)


def _dot_hi(a, b):
    return jnp.dot(a, b, preferred_element_type=F32, precision=HIGHEST)


def _dot_nt_hi(a, b):
    return lax.dot_general(a, b, (((1,), (1,)), ((), ())), preferred_element_type=F32, precision=HIGHEST)


def _split3(x):
    hi = x.astype(BF16)
    r1 = x - hi.astype(F32)
    mid = r1.astype(BF16)
    lo = (r1 - mid.astype(F32)).astype(BF16)
    return hi, mid, lo


def _dot3(x, m01):
    hi, mid, lo = _split3(x)
    return _dot(hi, m01) + _dot(mid, m01) + _dot(lo, m01)


def _silu(z):
    return z * jax.nn.sigmoid(z)


def _proj_kernel(x_ref, *refs, n_out):
    xb = x_ref[...].astype(BF16)
    for w_ref, o_ref in zip(refs[:n_out], refs[n_out:]):
        o_ref[...] = _dot(xb, w_ref[...])


def _multi_proj(x, ws, tm=256):
    m, k = x.shape
    tm = min(tm, m)
    n_out = len(ws)
    return pl.pallas_call(
        functools.partial(_proj_kernel, n_out=n_out),
        grid=(m // tm,),
        in_specs=[pl.BlockSpec((tm, k), lambda i: (i, 0))]
        + [pl.BlockSpec(w.shape, lambda i: (0, 0)) for w in ws],
        out_specs=[pl.BlockSpec((tm, w.shape[1]), lambda i: (i, 0)) for w in ws],
        out_shape=[jax.ShapeDtypeStruct((m, w.shape[1]), F32) for w in ws],
        compiler_params=_cparams("parallel"),
        name="multi_proj",
    )(x, *ws)


def _deepnorm(x, y, g, b):
    v = DEEPNORM_ALPHA * x + y
    mu = jnp.mean(v, axis=-1, keepdims=True)
    var = jnp.mean(jnp.square(v - mu), axis=-1, keepdims=True)
    return (v - mu) * lax.rsqrt(var + LN_EPS) * g + b


def _out_ln_kernel(o_ref, z_ref, x_ref, w_ref, g_ref, b_ref, out_ref):
    gated = (o_ref[...] * _silu(z_ref[...])).astype(BF16)
    y = _dot(gated, w_ref[...])
    out_ref[...] = _deepnorm(x_ref[...], y, g_ref[...], b_ref[...])


def _row_spec(tm, n):
    return pl.BlockSpec((tm, n), lambda i: (i, 0))


def _full_spec(shape):
    return pl.BlockSpec(shape, lambda i: (0,) * len(shape))


def _out_ln(o, z, x, w_out, g, b, tm=256):
    m, d = x.shape
    tm = min(tm, m)
    return pl.pallas_call(
        _out_ln_kernel,
        grid=(m // tm,),
        in_specs=[_row_spec(tm, d), _row_spec(tm, d), _row_spec(tm, d),
                  _full_spec(w_out.shape), _full_spec((1, d)), _full_spec((1, d))],
        out_specs=_row_spec(tm, d),
        out_shape=jax.ShapeDtypeStruct((m, d), F32),
        compiler_params=_cparams("parallel"),
        name="out_ln",
    )(o, z, x, w_out, g.reshape(1, d), b.reshape(1, d))


def _chunk_proj(x_ref, n_chunk, w1_ref):
    acc = [[None] * A_KV_HEADS for _ in range(2)]
    row_stride = CMP_STRIDE * 4
    for c in range(2):
        for s in range(CMP_STRIDE):
            w_r0 = w1_ref[c, pl.ds(s * 64, 64), :].astype(BF16)
            w_r1 = w1_ref[c, pl.ds((CMP_STRIDE + s) * 64, 64), :].astype(BF16)
            for pair in range(2):
                xs = x_ref[pl.ds(s * 4 + c * 2 + pair, n_chunk, stride=row_stride), :].astype(BF16)
                for half in range(2):
                    g = pair * 2 + half
                    xg = xs[:, half * 64:(half + 1) * 64]
                    t0 = _dot(xg, w_r0)
                    t1 = _dot(xg, w_r1)
                    if acc[c][g] is None:
                        acc[c][g] = (t0, t1)
                    else:
                        acc[c][g] = (acc[c][g][0] + t0, acc[c][g][1] + t1)
    return acc


def _compress_finish(acc, pe_ref, w1_ref, w2_ref, k_ref, v_ref, n_chunk):
    for c, o_ref in ((0, k_ref), (1, v_ref)):
        pe_term = _dot_hi(pe_ref[c], w1_ref[c])
        w2 = w2_ref[c].astype(BF16)
        for g in range(A_KV_HEADS):
            r0, r1 = acc[c][g]
            h = r0 + pltpu.roll(r1, n_chunk - 1, 0) + pe_term
            o_ref[g] = _dot(_silu(h).astype(BF16), w2)


def _compress_kernel(x_ref, pe_ref, w1_ref, w2_ref, k_ref, v_ref, *, n_chunk):
    acc = _chunk_proj(x_ref, n_chunk, w1_ref)
    _compress_finish(acc, pe_ref, w1_ref, w2_ref, k_ref, v_ref, n_chunk)


def _compress_prompt(kv_c, pe, w1, w2):
    bsz, t, _ = kv_c.shape
    n_chunk = t // CMP_STRIDE
    out = jax.ShapeDtypeStruct((bsz, A_KV_HEADS, n_chunk, A_HEAD_DIM), F32)
    out_spec = pl.BlockSpec((None, A_KV_HEADS, n_chunk, A_HEAD_DIM), lambda b: (b, 0, 0, 0))
    return pl.pallas_call(
        functools.partial(_compress_kernel, n_chunk=n_chunk),
        grid=(bsz,),
        in_specs=[pl.BlockSpec((None, t * 4, 128), lambda b: (b, 0, 0)),
                  _full_spec(pe.shape), _full_spec(w1.shape), _full_spec(w2.shape)],
        out_specs=[out_spec, out_spec],
        out_shape=[out, out],
        compiler_params=_cparams("parallel"),
        name="compress_prompt",
    )(kv_c.reshape(bsz, t * 4, 128), pe, w1, w2)


def _softmax_rows(s, valid):
    s = jnp.where(valid, s, NEG_BIG)
    m = jnp.max(s, axis=-1, keepdims=True)
    e = jnp.where(valid, jnp.exp(s - m), 0.0)
    return e / jnp.maximum(jnp.sum(e, axis=-1, keepdims=True), 1e-30)


def _topk_mask(score_t, n_keep):
    n_blk = score_t.shape[0]
    blk = lax.broadcasted_iota(jnp.int32, score_t.shape, 0)
    rank = jnp.zeros(score_t.shape, F32)
    for i in range(n_blk):
        row = score_t[i:i + 1, :]
        tie_ahead = jnp.where(blk > i, 1.0, 0.0)
        rank = rank + jnp.where(row > score_t, 1.0, jnp.where(row == score_t, tie_ahead, 0.0))
    return jnp.where(rank < n_keep, 1.0, 0.0)


def _tile_heads(x):
    return jnp.concatenate([x] * A_GROUP, axis=0)


def _nsa_prompt_kernel(q_ref, gate_ref, sf_ref, kc_ref, vc_ref, ks_ref, vs_ref, kw_ref, vw_ref,
                       sel_map_ref, expand_ref, o_ref, *, qb, i_base):
    t0 = (i_base + pl.program_id(2)) * qb
    n_keys = ks_ref.shape[0]
    n_cmp_pad = kc_ref.shape[0]
    n_sel = n_keys // SEL_BLOCK

    q = q_ref[...] * (A_HEAD_DIM ** -0.5 * LOG2E)
    qg = jnp.concatenate([q[:, r * 64:(r + 1) * 64] for r in range(A_GROUP)], axis=0)
    qa = jnp.concatenate([qg, sf_ref[...]], axis=1).astype(BF16)
    t_q = t0 + lax.broadcasted_iota(jnp.int32, (qb, 1), 0)

    n_idx = lax.broadcasted_iota(jnp.int32, (qb, n_cmp_pad), 1)
    ok_c = n_idx * CMP_STRIDE + (CMP_BLOCK - 1) <= t_q
    s_c = _dot_nt(qa, kc_ref[...]) + _tile_heads(jnp.where(ok_c, 0.0, NEG_BIG))
    m_c = jnp.max(s_c, axis=-1, keepdims=True)
    e_c = jnp.where(s_c > 0.5 * NEG_BIG, jnp.exp2(s_c - m_c), 0.0)
    p_c = e_c / jnp.maximum(jnp.sum(e_c, axis=-1, keepdims=True), 1e-30)
    o_c = _dot(p_c.astype(BF16), vc_ref[...])

    p_grp = p_c[0:qb] + p_c[qb:2 * qb] + p_c[2 * qb:3 * qb] + p_c[3 * qb:4 * qb]
    sel_map = sel_map_ref[...]
    hi, mid, lo = _split3(p_grp)
    p_slc_t = _dot_nt(sel_map, hi) + _dot_nt(sel_map, mid) + _dot_nt(sel_map, lo)
    blk = lax.broadcasted_iota(jnp.int32, (n_sel, qb), 0)
    cur = (t0 + lax.broadcasted_iota(jnp.int32, (n_sel, qb), 1)) // SEL_BLOCK
    forced = (blk == 0) | (blk == cur) | (blk == cur - 1)
    score_t = jnp.where(forced, 1e30, jnp.where(blk <= cur, p_slc_t, -1.0))
    keep_t = jnp.where(blk <= cur, _topk_mask(score_t, SEL_TOPK), 0.0)
    keep = keep_t.T.astype(BF16)

    k_pos = lax.broadcasted_iota(jnp.int32, (qb, n_keys), 1)
    bias = jnp.where(k_pos > t_q, NEG_BIG, (_dot(keep, expand_ref[...]) - 1.0) * (-NEG_BIG))
    s_s = _dot_nt(qa, ks_ref[...]) + _tile_heads(bias)
    p_s = jnp.exp2(s_s - jnp.max(s_s, axis=-1, keepdims=True))
    o_s = _dot(p_s.astype(BF16), vs_ref[...]) / jnp.sum(p_s, axis=-1, keepdims=True)

    n_win = min(WINDOW + qb, n_keys)
    w0 = pl.multiple_of(jnp.maximum(t0 + qb - n_win, 0), 8)
    w_pos = w0 + lax.broadcasted_iota(jnp.int32, (qb, n_win), 1)
    ok_w = (w_pos <= t_q) & (t_q - w_pos <= WINDOW)
    s_w = _dot_nt(qa, kw_ref[pl.ds(w0, n_win), :]) + _tile_heads(jnp.where(ok_w, 0.0, NEG_BIG))
    p_w = jnp.exp2(s_w - jnp.max(s_w, axis=-1, keepdims=True))
    o_w = _dot(p_w.astype(BF16), vw_ref[pl.ds(w0, n_win), :]) / jnp.sum(p_w, axis=-1, keepdims=True)

    gate = jax.nn.sigmoid(gate_ref[...])
    for r in range(A_GROUP):
        rs = slice(r * qb, (r + 1) * qb)
        o_ref[:, r * 64:(r + 1) * 64] = (gate[:, r:r + 1] * o_c[rs]
                                         + gate[:, A_GROUP + r:A_GROUP + r + 1] * o_s[rs]
                                         + gate[:, 2 * A_GROUP + r:2 * A_GROUP + r + 1] * o_w[rs])


def _head_major(kv, dtype):
    bsz, t, _ = kv.shape
    kv = kv.reshape(bsz, t, 2, A_KV_HEADS, A_HEAD_DIM).transpose(2, 0, 3, 1, 4).astype(dtype)
    return kv[0], kv[1]


def _group_gates(gate):
    bsz, t, _ = gate.shape
    return gate.reshape(bsz, t, 3, A_KV_HEADS, A_GROUP).transpose(0, 3, 1, 2, 4).reshape(bsz, A_KV_HEADS, t, 12)


def _sel_map(n_sel, n_cmp_pad, n_cmp):
    j = np.arange(n_sel)[:, None]
    n = np.arange(n_cmp_pad)[None, :]
    m = (n >= SEL_RATIO * j - (CMP_RATIO - 1)) & (n <= SEL_RATIO * j + SEL_RATIO - 1) & (n < n_cmp)
    return jnp.asarray(m, BF16)


def _slope_features(slopes, qb):
    s = slopes * LOG2E
    s1 = s.astype(BF16).astype(F32)
    s2 = (s - s1).astype(BF16).astype(F32)
    s3 = (s - s1 - s2).astype(BF16).astype(F32)
    feat = jnp.stack([64.0 * s1, s1, 64.0 * s2, s2, 64.0 * s3, s3], axis=-1)
    feat = jnp.pad(feat, ((0, 0), (0, A_HEAD_DIM - POS_FEATURES)))
    return jnp.repeat(feat.reshape(A_KV_HEADS, A_GROUP, A_HEAD_DIM), qb, axis=1)


def _with_pos(keys, pos):
    assert int(pos.max()) < MAX_POS
    a, b = pos // 64, pos % 64
    feat = np.zeros((pos.shape[0], A_HEAD_DIM), np.float32)
    feat[:, 0:POS_FEATURES:2] = a[:, None]
    feat[:, 1:POS_FEATURES:2] = b[:, None]
    feat = jnp.broadcast_to(jnp.asarray(feat, BF16), keys.shape)
    return jnp.concatenate([keys.astype(BF16), feat], axis=-1)


def _nsa_prompt_attn(q, gate, kv_s, kv_w, k_cmp, v_cmp, slopes, qb=128, key_step=512):
    bsz, t, _ = q.shape
    key_step = min(key_step, t)
    pos = np.arange(t)
    ks, vs = _head_major(kv_s, BF16)
    kw, vw = _head_major(kv_w, BF16)
    ks, kw = _with_pos(ks, pos), _with_pos(kw, pos)
    k_cmp = _with_pos(k_cmp, np.arange(k_cmp.shape[2]) * CMP_STRIDE + (CMP_BLOCK - 1))
    v_cmp = v_cmp.astype(BF16)
    gates = _group_gates(gate)
    slope_feat = _slope_features(slopes, qb)
    rows = A_GROUP * qb
    steps = key_step // qb
    outs = []
    for seg in range(t // key_step):
        n_keys = (seg + 1) * key_step
        n_cmp_pad = n_keys // CMP_STRIDE
        n_sel = n_keys // SEL_BLOCK
        expand = jnp.asarray(np.arange(n_keys)[None, :] // SEL_BLOCK == np.arange(n_sel)[:, None], BF16)
        i_base = seg * steps
        head = lambda rows_, width: pl.BlockSpec((None, None, rows_, width), lambda b, g, i: (b, g, 0, 0))
        outs.append(pl.pallas_call(
            functools.partial(_nsa_prompt_kernel, qb=qb, i_base=i_base),
            grid=(bsz, A_KV_HEADS, steps),
            in_specs=[pl.BlockSpec((None, qb, 256), lambda b, g, i, i_base=i_base: (b, i_base + i, g)),
                      pl.BlockSpec((None, None, qb, 12), lambda b, g, i, i_base=i_base: (b, g, i_base + i, 0)),
                      pl.BlockSpec((None, rows, A_HEAD_DIM), lambda b, g, i: (g, 0, 0)),
                      head(n_cmp_pad, 2 * A_HEAD_DIM), head(n_cmp_pad, A_HEAD_DIM),
                      head(n_keys, 2 * A_HEAD_DIM), head(n_keys, A_HEAD_DIM),
                      head(n_keys, 2 * A_HEAD_DIM), head(n_keys, A_HEAD_DIM),
                      pl.BlockSpec((n_sel, n_cmp_pad), lambda b, g, i: (0, 0)),
                      pl.BlockSpec((n_sel, n_keys), lambda b, g, i: (0, 0))],
            out_specs=pl.BlockSpec((None, qb, 256), lambda b, g, i: (b, i, g)),
            out_shape=jax.ShapeDtypeStruct((bsz, key_step, D_MODEL), F32),
            compiler_params=_cparams("parallel", "parallel", "arbitrary"),
            name=f"nsa_prompt_attn_{n_keys}",
        )(q, gates, slope_feat, k_cmp, v_cmp, ks, vs, kw, vw,
          _sel_map(n_sel, n_cmp_pad, t // CMP_STRIDE - 1), expand))
    return jnp.concatenate(outs, axis=1)


def _alibi_slopes():
    return 2.0 ** (-8.0 * jnp.arange(1, A_HEADS + 1, dtype=F32) / A_HEADS)


def _nsa_weights(w_in, pe, w1, w2, w_out):
    cuts = [int(c) for c in np.cumsum([D_MODEL, A_KV_WIDTH, A_KV_WIDTH, A_KV_WIDTH, 3 * A_HEADS])]
    pieces = jnp.split(w_in.astype(BF16), cuts, axis=-1)
    return dict(w_pieces=pieces, pe=pe.reshape(2, 1, CMP_BLOCK * A_HEAD_DIM), w1=w1, w2=w2,
                w_out=w_out.astype(BF16))


def _nsa_layer_prompt(x, wt, ln_g, ln_b):
    bsz, t, d = x.shape
    x2 = x.reshape(bsz * t, d)
    q, kv_c, kv_s, kv_w, gate, z = _multi_proj(x2, wt["w_pieces"])
    kv_c3, kv_s3, kv_w3 = (a.reshape(bsz, t, A_KV_WIDTH) for a in (kv_c, kv_s, kv_w))
    k_cmp, v_cmp = _compress_prompt(kv_c3, wt["pe"], wt["w1"], wt["w2"])
    o = _nsa_prompt_attn(q.reshape(bsz, t, d), gate.reshape(bsz, t, 3 * A_HEADS), kv_s3, kv_w3,
                         k_cmp, v_cmp, _alibi_slopes())
    x_new = _out_ln(o.reshape(bsz * t, d), z, x2, wt["w_out"], ln_g, ln_b)
    return x_new.reshape(bsz, t, d), kv_c3, kv_s3, kv_w3


def _page_copy(cache_ref, layer, page, buf_ref, slot, sem):
    rows = PAGE_SIZE * 4
    return pltpu.make_async_copy(cache_ref.at[layer, page], buf_ref.at[pl.ds(slot * rows, rows)], sem)


def _sample_compress_kernel(pt_ref, cache_ref, pe_ref, w1_ref, w2_ref, k_ref, v_ref, buf_ref, sem,
                            *, layer, n_pages):
    b = pl.program_id(0)
    for p in range(n_pages):
        _page_copy(cache_ref, layer, pt_ref[b * n_pages + p], buf_ref, p, sem).start()
    for p in range(n_pages):
        _page_copy(cache_ref, layer, pt_ref[b * n_pages + p], buf_ref, p, sem).wait()
    n_chunk = n_pages * PAGE_SIZE // CMP_STRIDE
    acc = _chunk_proj(buf_ref, n_chunk, w1_ref)
    _compress_finish(acc, pe_ref, w1_ref, w2_ref, k_ref, v_ref, n_chunk)


def _sample_compress(cache, layer, page_table, pe, w1, w2):
    n_layers, n_phys = cache.shape[:2]
    bsz, n_pages = page_table.shape
    n_chunk = n_pages * PAGE_SIZE // CMP_STRIDE
    cache4 = cache.reshape(n_layers, n_phys, PAGE_SIZE * 4, 128)
    out = jax.ShapeDtypeStruct((bsz, A_KV_HEADS, n_chunk, A_HEAD_DIM), F32)
    out_spec = pl.BlockSpec((None, A_KV_HEADS, n_chunk, A_HEAD_DIM), lambda b, pt: (b, 0, 0, 0))
    full = lambda a: pl.BlockSpec(a.shape, lambda b, pt: (0,) * a.ndim)
    grid_spec = pltpu.PrefetchScalarGridSpec(
        num_scalar_prefetch=1, grid=(bsz,),
        in_specs=[pl.BlockSpec(memory_space=pl.ANY), full(pe), full(w1), full(w2)],
        out_specs=[out_spec, out_spec],
        scratch_shapes=[pltpu.VMEM((n_pages * PAGE_SIZE * 4, 128), F32), pltpu.SemaphoreType.DMA(())])
    return pl.pallas_call(
        functools.partial(_sample_compress_kernel, layer=layer, n_pages=n_pages),
        grid_spec=grid_spec, out_shape=[out, out],
        compiler_params=_cparams("arbitrary"),
        name="sample_compress",
    )(page_table.reshape(-1), cache4, pe, w1, w2)


def _sample_cmp_kernel(q_ref, slope_ref, kc_ref, vc_ref, map_ref, oc_ref, idx_ref, *, past, n_cand):
    n_pad = kc_ref.shape[1]
    n_cmp = n_pad - 1
    n_idx = lax.broadcasted_iota(jnp.int32, (A_GROUP, n_pad), 1)
    dist = (past - (CMP_BLOCK - 1)) - n_idx * CMP_STRIDE
    valid = (dist >= 0) & (n_idx < n_cmp)
    q = q_ref[...] * (A_HEAD_DIM ** -0.5)
    n_keep = idx_ref.shape[1]
    width = map_ref.shape[1]
    lane = lax.broadcasted_iota(jnp.int32, (1, width), 1)
    row_i = lax.broadcasted_iota(jnp.int32, (width, width), 0)
    col_j = lax.broadcasted_iota(jnp.int32, (width, width), 1)
    for g in range(A_KV_HEADS):
        hs = slice(g * A_GROUP, (g + 1) * A_GROUP)
        s = _dot_nt(q[hs].astype(BF16), kc_ref[g].astype(BF16)) - slope_ref[hs] * dist.astype(F32)
        p = _softmax_rows(s, valid)
        oc_ref[hs, :] = _dot(p.astype(BF16), vc_ref[g].astype(BF16))
        p_grp = jnp.sum(p, axis=0, keepdims=True)
        p_slc = _dot3(p_grp, map_ref[...])
        forced = (lane == 0) | (lane == n_cand - 1)
        score = jnp.where(forced, 1e30, jnp.where(lane < n_cand, p_slc, -1.0))
        s_j = jnp.broadcast_to(score, (width, width))
        s_i = s_j.T
        tie_ahead = jnp.where(row_i < col_j, 1.0, 0.0)
        ahead = jnp.where(s_i > s_j, 1.0, jnp.where(s_i == s_j, tie_ahead, 0.0))
        rank = jnp.sum(ahead, axis=0, keepdims=True)
        want = lax.broadcasted_iota(jnp.int32, (n_keep, width), 0).astype(F32)
        picked = jnp.where(rank == want, lane.astype(F32), 0.0)
        idx_ref[g] = jnp.sum(picked, axis=-1, keepdims=True).astype(jnp.int32)


def _sample_cmp_topk(q, k_cmp, v_cmp, slopes, past):
    bsz = q.shape[0]
    n_pad = k_cmp.shape[2]
    n_cand = past // SEL_BLOCK
    width = -(-n_cand // 128) * 128
    j = np.arange(width)[None, :]
    n = np.arange(n_pad)[:, None]
    sel_map = (n >= SEL_RATIO * j - (CMP_RATIO - 1)) & (n <= SEL_RATIO * j + SEL_RATIO - 1) & (n < n_pad - 1)
    cmp_spec = pl.BlockSpec((None, A_KV_HEADS, n_pad, A_HEAD_DIM), lambda b: (b, 0, 0, 0))
    return pl.pallas_call(
        functools.partial(_sample_cmp_kernel, past=past, n_cand=n_cand),
        grid=(bsz,),
        in_specs=[pl.BlockSpec((None, A_HEADS, A_HEAD_DIM), lambda b: (b, 0, 0)),
                  _full_spec((A_HEADS, 1)), cmp_spec, cmp_spec, _full_spec((n_pad, width))],
        out_specs=[pl.BlockSpec((None, A_HEADS, A_HEAD_DIM), lambda b: (b, 0, 0)),
                   pl.BlockSpec((None, A_KV_HEADS, SEL_TOPK, 1), lambda b: (b, 0, 0, 0))],
        out_shape=[jax.ShapeDtypeStruct((bsz, A_HEADS, A_HEAD_DIM), F32),
                   jax.ShapeDtypeStruct((bsz, A_KV_HEADS, SEL_TOPK, 1), jnp.int32)],
        compiler_params=_cparams("parallel"),
        name="sample_cmp_topk",
    )(q, slopes.reshape(A_HEADS, 1), k_cmp, v_cmp, jnp.asarray(sel_map, BF16))


def _sel_block_copy(cache_ref, layer, pt_ref, idx_ref, b, g, r, n_pages, buf_ref, sem):
    blk = idx_ref[(b * A_KV_HEADS + g) * SEL_TOPK + r]
    page = pt_ref[b * n_pages + blk // (PAGE_SIZE // SEL_BLOCK)]
    row0 = pl.multiple_of((blk % (PAGE_SIZE // SEL_BLOCK)) * SEL_BLOCK, SEL_BLOCK)
    return pltpu.make_async_copy(cache_ref.at[layer, page, pl.ds(row0, SEL_BLOCK)],
                                 buf_ref.at[g * (SEL_TOPK - 1) + r], sem)


def _sample_attn_kernel(pt_ref, idx_ref, cache_ref, q_ref, gate_ref, slope_ref, oc_ref, ks_new_ref,
                        kw_new_ref, win_ref, o_ref, buf_ref, sem, *, layer, n_pages, past):
    b = pl.program_id(0)
    n_blk = SEL_TOPK - 1
    for g in range(A_KV_HEADS):
        for r in range(n_blk):
            _sel_block_copy(cache_ref, layer, pt_ref, idx_ref, b, g, r, n_pages, buf_ref, sem).start()

    q = q_ref[...] * (A_HEAD_DIM ** -0.5)
    gate = jax.nn.sigmoid(gate_ref[...])
    n_buf = win_ref.shape[0]
    pos_in_blk = lax.broadcasted_iota(jnp.int32, (1, SEL_BLOCK), 1)
    dist_w = (n_buf - lax.broadcasted_iota(jnp.int32, (1, n_buf), 1)).astype(F32)

    def attend(qg, slope, keys, vals, dist, k_new, v_new):
        s = _dot_nt(qg.astype(BF16), keys.astype(BF16)) - slope * dist
        s_new = jnp.sum(qg * k_new, axis=-1, keepdims=True)
        m = jnp.maximum(jnp.max(s, axis=-1, keepdims=True), s_new)
        e = jnp.exp(s - m)
        e_new = jnp.exp(s_new - m)
        l = jnp.sum(e, axis=-1, keepdims=True) + e_new
        return (_dot(e.astype(BF16), vals.astype(BF16)) + e_new * v_new) / l

    out_w = []
    for g in range(A_KV_HEADS):
        hs = slice(g * A_GROUP, (g + 1) * A_GROUP)
        ksl = slice(g * A_HEAD_DIM, (g + 1) * A_HEAD_DIM)
        vsl = slice(A_KV_WIDTH // 2 + g * A_HEAD_DIM, A_KV_WIDTH // 2 + (g + 1) * A_HEAD_DIM)
        out_w.append(attend(q[hs], slope_ref[hs], win_ref[:, ksl], win_ref[:, vsl], dist_w,
                            kw_new_ref[:, ksl], kw_new_ref[:, vsl]))

    for g in range(A_KV_HEADS):
        for r in range(n_blk):
            _sel_block_copy(cache_ref, layer, pt_ref, idx_ref, b, g, r, n_pages, buf_ref, sem).wait()

    for g in range(A_KV_HEADS):
        hs = slice(g * A_GROUP, (g + 1) * A_GROUP)
        ksl = slice(g * A_HEAD_DIM, (g + 1) * A_HEAD_DIM)
        vsl = slice(A_KV_WIDTH // 2 + g * A_HEAD_DIM, A_KV_WIDTH // 2 + (g + 1) * A_HEAD_DIM)
        keys = jnp.concatenate([buf_ref[g * n_blk + r, :, ksl] for r in range(n_blk)], axis=0)
        vals = jnp.concatenate([buf_ref[g * n_blk + r, :, vsl] for r in range(n_blk)], axis=0)
        dist = jnp.concatenate(
            [past - (idx_ref[(b * A_KV_HEADS + g) * SEL_TOPK + r] * SEL_BLOCK + pos_in_blk)
             for r in range(n_blk)], axis=1).astype(F32)
        o_s = attend(q[hs], slope_ref[hs], keys, vals, dist, ks_new_ref[:, ksl], ks_new_ref[:, vsl])
        gt = gate[hs]
        o_ref[hs, :] = gt[:, 0:1] * oc_ref[hs, :] + gt[:, 1:2] * o_s + gt[:, 2:3] * out_w[g]


def _sample_attn(page_table, idx, cache_sel, layer, q, gate, slopes, o_c, kv_s_new, kv_w_new, cache_win, past):
    bsz, n_pages = page_table.shape
    n_layers, n_phys = cache_sel.shape[:2]
    n_buf = cache_win.shape[2]
    cache3 = cache_sel.reshape(n_layers, n_phys, PAGE_SIZE, A_KV_WIDTH)
    head_spec = pl.BlockSpec((None, A_HEADS, A_HEAD_DIM), lambda b, pt, ix: (b, 0, 0))
    new_spec = pl.BlockSpec((None, 1, A_KV_WIDTH), lambda b, pt, ix: (b, 0, 0))
    grid_spec = pltpu.PrefetchScalarGridSpec(
        num_scalar_prefetch=2, grid=(bsz,),
        in_specs=[pl.BlockSpec(memory_space=pl.ANY), head_spec,
                  pl.BlockSpec((None, A_HEADS, 3), lambda b, pt, ix: (b, 0, 0)),
                  pl.BlockSpec((A_HEADS, 1), lambda b, pt, ix: (0, 0)),
                  head_spec, new_spec, new_spec,
                  pl.BlockSpec((None, None, n_buf, A_KV_WIDTH), lambda b, pt, ix: (layer, b, 0, 0))],
        out_specs=head_spec,
        scratch_shapes=[pltpu.VMEM((A_KV_HEADS * (SEL_TOPK - 1), SEL_BLOCK, A_KV_WIDTH), F32),
                        pltpu.SemaphoreType.DMA(())])
    return pl.pallas_call(
        functools.partial(_sample_attn_kernel, layer=layer, n_pages=n_pages, past=past),
        grid_spec=grid_spec,
        out_shape=jax.ShapeDtypeStruct((bsz, A_HEADS, A_HEAD_DIM), F32),
        compiler_params=_cparams("arbitrary"),
        name="sample_attn",
    )(page_table.reshape(-1), idx.reshape(-1), cache3, q, gate, slopes.reshape(A_HEADS, 1), o_c,
      kv_s_new, kv_w_new, cache_win)


def _nsa_layer_sample(x, wt, ln_g, ln_b, layer, cache_cmp, cache_sel, cache_win, page_table):
    bsz, _, d = x.shape
    n_pages = page_table.shape[1]
    past = n_pages * PAGE_SIZE
    x2 = x.reshape(bsz, d)
    q, kv_c, kv_s, kv_w, gate, z = _multi_proj(x2, wt["w_pieces"])
    slopes = _alibi_slopes()
    k_cmp, v_cmp = _sample_compress(cache_cmp, layer, page_table, wt["pe"], wt["w1"], wt["w2"])
    q3 = q.reshape(bsz, A_HEADS, A_HEAD_DIM)
    o_c, idx = _sample_cmp_topk(q3, k_cmp, v_cmp, slopes, past)
    gate3 = gate.reshape(bsz, 3, A_HEADS).transpose(0, 2, 1)
    kv_s3, kv_w3 = kv_s.reshape(bsz, 1, A_KV_WIDTH), kv_w.reshape(bsz, 1, A_KV_WIDTH)
    n_layers, _, n_buf = cache_win.shape[:3]
    win4 = cache_win.reshape(n_layers, bsz, n_buf, A_KV_WIDTH)
    o = _sample_attn(page_table, idx, cache_sel, layer, q3, gate3, slopes, o_c, kv_s3, kv_w3, win4, past)
    x_new = _out_ln(o.reshape(bsz, d), z, x2, wt["w_out"], ln_g, ln_b)
    n_keep = min(WINDOW, n_buf + 1)
    win_new = jnp.concatenate([win4[layer][:, n_buf + 1 - n_keep:], kv_w3], axis=1)
    return x_new.reshape(bsz, 1, d), kv_c.reshape(bsz, 1, A_KV_WIDTH), kv_s3, win_new


def _s5_param_kernel(log_dt_ref, ar_ref, ai_ref, br_ref, bi_ref, tile_ref,
                     abr_ref, abi_ref, bbr_ref, bbi_ref):
    dt = jnp.exp(log_dt_ref[...])
    ar, ai = ar_ref[...], ai_ref[...]
    mag = jnp.exp(dt * ar)
    abr, abi = mag * jnp.cos(dt * ai), mag * jnp.sin(dt * ai)
    den = ar * ar + ai * ai
    num_re, num_im = abr - 1.0, abi
    zoh_re = (num_re * ar + num_im * ai) / den
    zoh_im = (num_im * ar - num_re * ai) / den
    abr_ref[...] = abr
    abi_ref[...] = abi
    zr = _dot3(zoh_re, tile_ref[...])
    zi = _dot3(zoh_im, tile_ref[...])
    br, bi = br_ref[...], bi_ref[...]
    bbr_ref[...] = zr * br - zi * bi
    bbi_ref[...] = zr * bi + zi * br


def _s5_weights(w_in, log_dt, a_re, a_im, b_re, b_im, c_re, c_im, d_skip, w_glu, b_glu, w_out):
    g, p, c = S5_GROUPS, S5_STATE, S5_GROUP_CH
    tile = jnp.asarray(np.tile(np.eye(p, dtype=np.float32), (1, c)), BF16)
    to_gcp = lambda b: b.transpose(0, 2, 1).reshape(g, c * p)
    shapes = [jax.ShapeDtypeStruct((g, p), F32)] * 2 + [jax.ShapeDtypeStruct((g, c * p), F32)] * 2
    abr, abi, bbr, bbi = pl.pallas_call(_s5_param_kernel, out_shape=shapes, name="s5_params")(
        log_dt.reshape(g, 1), a_re, a_im, to_gcp(b_re), to_gcp(b_im), tile)
    eye = jnp.eye(S5_GROUPS // S5_SETS, dtype=F32)
    gl = S5_GROUPS // S5_SETS

    def in_blockdiag(bb):
        bb = bb.reshape(S5_SETS, gl, c, p)
        return jnp.einsum('sgcp,gh->sgchp', bb, eye).reshape(S5_SETS, gl * c, gl * p).astype(BF16)

    def out_blockdiag(cc):
        cc = cc.reshape(S5_SETS, gl, c, p)
        return jnp.einsum('sgcp,gh->sgphc', cc, eye).reshape(S5_SETS, gl * p, gl * c).astype(BF16)

    w_u, w_z = jnp.split(w_in.astype(BF16), 2, axis=-1)
    return dict(w_pieces=[w_u, w_z], abr=abr.reshape(1, g * p), abi=abi.reshape(1, g * p),
                wb_re=in_blockdiag(bbr), wb_im=in_blockdiag(bbi),
                wc_re=out_blockdiag(c_re), wc_im=out_blockdiag(c_im),
                d_skip=d_skip.reshape(1, -1), w_glu=w_glu.astype(BF16), b_glu=b_glu.reshape(1, -1),
                w_out=w_out.astype(BF16))


def _s5_scan_kernel(u_ref, wbr_ref, wbi_ref, ar_ref, ai_ref, wcr_ref, wci_ref, h0r_ref, h0i_ref,
                    y_ref, hr_out, hi_out, bur, bui, hr_c, hi_c, *, tc):
    @pl.when(pl.program_id(1) == 0)
    def _():
        hr_c[...] = h0r_ref[...]
        hi_c[...] = h0i_ref[...]

    n_set = S5_GROUPS * S5_STATE // S5_SETS
    n_ch = S5_GROUPS * S5_GROUP_CH // S5_SETS
    ub = u_ref[...].astype(BF16)
    for s in range(S5_SETS):
        us = ub[:, s * n_ch:(s + 1) * n_ch]
        bur[:, s * n_set:(s + 1) * n_set] = _dot(us, wbr_ref[s])
        bui[:, s * n_set:(s + 1) * n_set] = _dot(us, wbi_ref[s])

    for s in range(S5_SETS):
        sl = pl.ds(s * n_set, n_set)
        ar, ai = ar_ref[:, sl], ai_ref[:, sl]

        def step(t, carry):
            hr, hi = carry
            nr = ar * hr - ai * hi + bur[pl.ds(t, 1), sl]
            ni = ar * hi + ai * hr + bui[pl.ds(t, 1), sl]
            bur[pl.ds(t, 1), sl] = nr
            bui[pl.ds(t, 1), sl] = ni
            return nr, ni

        hr, hi = lax.fori_loop(0, tc, step, (hr_c[:, sl], hi_c[:, sl]))
        hr_c[:, sl] = hr
        hi_c[:, sl] = hi

    for s in range(S5_SETS):
        h_re = bur[:, s * n_set:(s + 1) * n_set]
        h_im = bui[:, s * n_set:(s + 1) * n_set]
        re_hi = h_re.astype(BF16)
        re_lo = (h_re - re_hi.astype(F32)).astype(BF16)
        im_hi = h_im.astype(BF16)
        im_lo = (h_im - im_hi.astype(F32)).astype(BF16)
        y = (_dot(re_hi, wcr_ref[s]) + _dot(re_lo, wcr_ref[s])
             - _dot(im_hi, wci_ref[s]) - _dot(im_lo, wci_ref[s]))
        y_ref[:, s * n_ch:(s + 1) * n_ch] = y
    hr_out[...] = hr_c[...]
    hi_out[...] = hi_c[...]


def _s5_scan(u, h0_re, h0_im, wt, tc=256):
    bsz, t, d = u.shape
    tc = min(tc, t)
    n_state = S5_GROUPS * S5_STATE
    st_spec = pl.BlockSpec((None, 1, n_state), lambda b, j: (b, 0, 0))
    full = lambda a: pl.BlockSpec(a.shape, lambda b, j: (0,) * a.ndim)
    st_shape = jax.ShapeDtypeStruct((bsz, 1, n_state), F32)
    return pl.pallas_call(
        functools.partial(_s5_scan_kernel, tc=tc),
        grid=(bsz, t // tc),
        in_specs=[pl.BlockSpec((None, tc, d), lambda b, j: (b, j, 0)),
                  full(wt["wb_re"]), full(wt["wb_im"]), full(wt["abr"]), full(wt["abi"]),
                  full(wt["wc_re"]), full(wt["wc_im"]), st_spec, st_spec],
        out_specs=[pl.BlockSpec((None, tc, d), lambda b, j: (b, j, 0)), st_spec, st_spec],
        out_shape=[jax.ShapeDtypeStruct((bsz, t, d), F32), st_shape, st_shape],
        scratch_shapes=[pltpu.VMEM((tc, n_state), F32), pltpu.VMEM((tc, n_state), F32),
                        pltpu.VMEM((1, n_state), F32), pltpu.VMEM((1, n_state), F32)],
        compiler_params=_cparams("parallel", "arbitrary"),
        name="s5_scan",
    )(u, wt["wb_re"], wt["wb_im"], wt["abr"], wt["abi"], wt["wc_re"], wt["wc_im"], h0_re, h0_im)


def _s5_out_kernel(y_ref, u_ref, z_ref, x_ref, d_ref, wg_ref, bg_ref, w_ref, g_ref, b_ref, out_ref):
    y = jax.nn.gelu(y_ref[...] + d_ref[...] * u_ref[...])
    y = y * jax.nn.sigmoid(_dot(y.astype(BF16), wg_ref[...]) + bg_ref[...])
    out = _dot((y * _silu(z_ref[...])).astype(BF16), w_ref[...])
    out_ref[...] = _deepnorm(x_ref[...], out, g_ref[...], b_ref[...])


def _s5_out(y, u, z, x, wt, g, b, tm=256):
    m, d = x.shape
    tm = min(tm, m)
    vec = _full_spec((1, d))
    return pl.pallas_call(
        _s5_out_kernel,
        grid=(m // tm,),
        in_specs=[_row_spec(tm, d)] * 4 + [vec, _full_spec((d, d)), vec, _full_spec((d, d)), vec, vec],
        out_specs=_row_spec(tm, d),
        out_shape=jax.ShapeDtypeStruct((m, d), F32),
        compiler_params=_cparams("parallel"),
        name="s5_out",
    )(y, u, z, x, wt["d_skip"], wt["w_glu"], wt["b_glu"], wt["w_out"], g.reshape(1, d), b.reshape(1, d))


def _s5_layer(x, h0_re, h0_im, wt, ln_g, ln_b):
    bsz, t, d = x.shape
    x2 = x.reshape(bsz * t, d)
    u, z = _multi_proj(x2, wt["w_pieces"])
    n_state = S5_GROUPS * S5_STATE
    y, hr, hi = _s5_scan(u.reshape(bsz, t, d), h0_re.reshape(bsz, 1, n_state),
                         h0_im.reshape(bsz, 1, n_state), wt)
    x_new = _s5_out(y.reshape(bsz * t, d), u, z, x2, wt, ln_g, ln_b)
    st = lambda h: h.reshape(bsz, S5_GROUPS, S5_STATE)
    return x_new.reshape(bsz, t, d), st(hr), st(hi)


def _head_sum(x, seg, seg_t):
    return _dot3(_dot3(x, seg), seg_t)


def _softplus(x):
    return jnp.maximum(x, 0.0) + jnp.log(1.0 + jnp.exp(-jnp.abs(x)))


def _rwkv_proj_kernel(x_ref, xp_ref, mu_ref, wr_ref, wwl_ref, wk_ref, wv_ref, wal_ref, wz_ref,
                      w0_ref, w2_ref, a0_ref, a2_ref, kk_ref, ka_ref, seg_ref, segt_ref,
                      r_ref, lw_ref, k_ref, v_ref, kkn_ref, a_ref, z_ref):
    x = x_ref[...]
    dx = xp_ref[...] - x
    mix = lambda m: (x + dx * mu_ref[m:m + 1, :]).astype(BF16)
    r_ref[...] = _dot(mix(0), wr_ref[...])
    w_lo = _dot(mix(1), wwl_ref[...])
    k = _dot(mix(2), wk_ref[...])
    v_ref[...] = _dot(mix(3), wv_ref[...])
    a_lo = _dot(mix(4), wal_ref[...])
    z_ref[...] = _dot(mix(5), wz_ref[...])
    w_log = -_softplus(-(w0_ref[...] + _dot(jnp.tanh(w_lo).astype(BF16), w2_ref[...]))) - 0.5
    lw_ref[...] = -jnp.exp(w_log)
    a = jax.nn.sigmoid(a0_ref[...] + _dot(a_lo.astype(BF16), a2_ref[...]))
    a_ref[...] = a
    kk = k * kk_ref[...]
    norm = jnp.sqrt(_head_sum(kk * kk, seg_ref[...], segt_ref[...]))
    kkn_ref[...] = kk / jnp.maximum(norm, 1e-12)
    k_ref[...] = k * (1.0 + (a - 1.0) * ka_ref[...])


def _head_seg():
    seg = np.zeros((D_MODEL, 128), np.float32)
    seg[np.arange(D_MODEL), np.arange(D_MODEL) // C_HEAD_DIM] = 1.0
    return jnp.asarray(seg, BF16), jnp.asarray(seg.T.copy(), BF16)


def _rwkv_weights(mu, w_in, w0, w2, a0, a2, k_k, k_a, r_k, lnx_g, lnx_b, w_out):
    d = D_MODEL
    lora_w, lora_a = w2.shape[0], a2.shape[0]
    cuts = [int(c) for c in np.cumsum([d, lora_w, d, d, lora_a])]
    pieces = jnp.split(w_in.astype(BF16), cuts, axis=-1)
    seg, seg_t = _head_seg()
    row = lambda a: a.reshape(1, d)
    return dict(mu=mu, pieces=pieces, w0=row(w0), w2=w2.astype(BF16), a0=row(a0), a2=a2.astype(BF16),
                k_k=row(k_k), k_a=row(k_a), r_k=row(r_k), lnx_g=row(lnx_g), lnx_b=row(lnx_b),
                w_out=w_out.astype(BF16), seg=seg, seg_t=seg_t)


def _rwkv_proj(x, x_prev, wt, tm=256):
    m, d = x.shape
    tm = min(tm, m)
    ins = [x, x_prev, wt["mu"], *wt["pieces"], wt["w0"], wt["w2"], wt["a0"], wt["a2"],
           wt["k_k"], wt["k_a"], wt["seg"], wt["seg_t"]]
    in_specs = [_row_spec(tm, d), _row_spec(tm, d)] + [_full_spec(a.shape) for a in ins[2:]]
    return pl.pallas_call(
        _rwkv_proj_kernel,
        grid=(m // tm,),
        in_specs=in_specs,
        out_specs=[_row_spec(tm, d)] * 7,
        out_shape=[jax.ShapeDtypeStruct((m, d), F32)] * 7,
        compiler_params=_cparams("parallel"),
        name="rwkv_proj",
    )(*ins)


def _dot_tn(a, b):
    return lax.dot_general(a, b, (((0,), (0,)), ((), ())), preferred_element_type=F32)


def _mm(a, w, dot=_dot):
    return dot(a.astype(BF16), w)


def _wkv_chunk_kernel(r_ref, lw_ref, k_ref, v_ref, kkn_ref, a_ref, y_ref, s_out, s_ref, *, chunk):
    n = chunk
    gw = 4 * C_HEAD_DIM
    n_grp = C_HEADS // 4

    @pl.when(pl.program_id(1) == 0)
    def _():
        s_ref[...] = jnp.zeros(s_ref.shape, F32)

    lw = lw_ref[...]
    row = lax.broadcasted_iota(jnp.int32, (n, n), 0)
    col = lax.broadcasted_iota(jnp.int32, (n, n), 1)
    cum = _dot_hi(jnp.where(row >= col, 1.0, 0.0), lw)
    tot = cum[n - 1:n, :]
    kkn = kkn_ref[...]
    b_vec = kkn * a_ref[...]
    e_neg = jnp.exp(-cum)
    e_tail = jnp.exp(tot - cum)
    a_t = -kkn * jnp.exp(cum - lw)
    r_t = r_ref[...] * jnp.exp(cum)
    k = k_ref[...]
    b_t, k_t = b_vec * e_neg, k * e_neg
    b_h, k_h = b_vec * e_tail, k * e_tail
    p_tot = jnp.exp(tot)
    v = v_ref[...]

    t_idx = lax.broadcasted_iota(jnp.int32, (n, gw), 0)
    s_idx = lax.rem(lax.broadcasted_iota(jnp.int32, (n, gw), 1), C_HEAD_DIM)
    strict = jnp.where(s_idx < t_idx, 1.0, 0.0)
    incl = jnp.where(s_idx <= t_idx, 1.0, 0.0)
    eye = jnp.where(s_idx == t_idx, 1.0, 0.0)
    same_head = (lax.broadcasted_iota(jnp.int32, (gw, gw), 0) // C_HEAD_DIM
                 == lax.broadcasted_iota(jnp.int32, (gw, gw), 1) // C_HEAD_DIM)
    head_mask = jnp.where(same_head, 1.0, 0.0)
    head_mask_bf = head_mask.astype(BF16)

    def blockdiag(x):
        return jnp.concatenate([x.astype(BF16)] * 4, axis=0) * head_mask_bf

    grp = range(n_grp)
    sls = [slice(g * gw, (g + 1) * gw) for g in grp]
    ar = [jnp.concatenate([a_t[:, sl], r_t[:, sl]], axis=0) for sl in sls]
    ab = [_mm(ar[g], blockdiag(b_t[:, sls[g]]), _dot_nt) for g in grp]
    ak = [_mm(ar[g], blockdiag(k_t[:, sls[g]]), _dot_nt) for g in grp]
    s0 = [s_ref[g] for g in grp]
    xs = [_mm(ar[g], s0[g].astype(BF16), _dot_nt) for g in grp]
    v_bd = [blockdiag(v[:, sl]) for sl in sls]
    a_ab = [ab[g][:n] * strict for g in grp]
    inv = [eye + a_ab[g] for g in grp]
    pw = a_ab
    for _ in range(int(math.log2(n)) - 1):
        pw = [_mm(pw[g], blockdiag(pw[g])) for g in grp]
        inv = [inv[g] + _mm(inv[g], blockdiag(pw[g])) for g in grp]
    x = [xs[g][:n] + _mm(ak[g][:n] * strict, v_bd[g]) for g in grp]
    u = [_mm(inv[g], blockdiag(x[g])) for g in grp]
    for g in grp:
        y_ref[:, sls[g]] = (xs[g][n:] + _mm(ab[g][n:] * incl, blockdiag(u[g]))
                            + _mm(ak[g][n:] * incl, v_bd[g]))
    for g in grp:
        uv = jnp.concatenate([u[g], v[:, sls[g]]], axis=0)
        bk = jnp.concatenate([b_h[:, sls[g]], k_h[:, sls[g]]], axis=0)
        s_ref[g] = s0[g] * p_tot[:, sls[g]] + _mm(uv, bk.astype(BF16), _dot_tn) * head_mask

    @pl.when(pl.program_id(1) == pl.num_programs(1) - 1)
    def _():
        for h in range(C_HEADS):
            g, hl = divmod(h, 4)
            blk = slice(hl * C_HEAD_DIM, (hl + 1) * C_HEAD_DIM)
            s_out[h] = s_ref[g, blk, blk]


def _wkv_chunked(r, lw, k, v, kkn, a, chunk=WKV_CHUNK):
    bsz, t, d = r.shape
    spec = pl.BlockSpec((None, chunk, d), lambda b, j: (b, j, 0))
    s_spec = pl.BlockSpec((None, C_HEADS, C_HEAD_DIM, C_HEAD_DIM), lambda b, j: (b, 0, 0, 0))
    return pl.pallas_call(
        functools.partial(_wkv_chunk_kernel, chunk=chunk),
        grid=(bsz, t // chunk),
        in_specs=[spec] * 6,
        out_specs=[spec, s_spec],
        out_shape=[jax.ShapeDtypeStruct((bsz, t, d), F32),
                   jax.ShapeDtypeStruct((bsz, C_HEADS, C_HEAD_DIM, C_HEAD_DIM), F32)],
        scratch_shapes=[pltpu.VMEM((C_HEADS // 4, 4 * C_HEAD_DIM, 4 * C_HEAD_DIM), F32)],
        compiler_params=_cparams("parallel", "arbitrary"),
        name="wkv_chunked",
    )(r, lw, k, v, kkn, a)


def _wkv_step_kernel(s_ref, r_ref, lw_ref, k_ref, kkn_ref, a_ref, v_ref, y_ref, s_out):
    s0 = s_ref[...]
    kkn = kkn_ref[...]
    sa = jnp.sum(s0 * (-kkn), axis=-1, keepdims=True)
    s1 = s0 * jnp.exp(lw_ref[...]) + sa * (kkn * a_ref[...]) + v_ref[...] * k_ref[...]
    s_out[...] = s1
    y_ref[...] = jnp.sum(s1 * r_ref[...], axis=-1, keepdims=True)


def _wkv_step(s0, r, lw, k, v, kkn, a):
    bsz = s0.shape[0]
    rowv = lambda x: x.reshape(bsz, C_HEADS, 1, C_HEAD_DIM)
    colv = lambda x: x.reshape(bsz, C_HEADS, C_HEAD_DIM, 1)
    s_spec = pl.BlockSpec((None, C_HEADS, C_HEAD_DIM, C_HEAD_DIM), lambda b: (b, 0, 0, 0))
    r_spec = pl.BlockSpec((None, C_HEADS, 1, C_HEAD_DIM), lambda b: (b, 0, 0, 0))
    c_spec = pl.BlockSpec((None, C_HEADS, C_HEAD_DIM, 1), lambda b: (b, 0, 0, 0))
    y, s1 = pl.pallas_call(
        _wkv_step_kernel,
        grid=(bsz,),
        in_specs=[s_spec] + [r_spec] * 5 + [c_spec],
        out_specs=[c_spec, s_spec],
        out_shape=[jax.ShapeDtypeStruct((bsz, C_HEADS, C_HEAD_DIM, 1), F32),
                   jax.ShapeDtypeStruct(s0.shape, F32)],
        compiler_params=_cparams("parallel"),
        name="wkv_step",
    )(s0, rowv(r), rowv(lw), rowv(k), rowv(kkn), rowv(a), colv(v))
    return y.reshape(bsz, D_MODEL), s1


def _rwkv_out_kernel(y_ref, r_ref, k_ref, v_ref, z_ref, x_ref, rk_ref, lg_ref, lb_ref, seg_ref, segt_ref,
                     w_ref, g_ref, b_ref, out_ref):
    seg, seg_t = seg_ref[...], segt_ref[...]
    y = y_ref[...]
    inv_n = 1.0 / C_HEAD_DIM
    yc = y - _head_sum(y, seg, seg_t) * inv_n
    var = _head_sum(yc * yc, seg, seg_t) * inv_n
    yn = yc * lax.rsqrt(var + C_LN_EPS) * lg_ref[...] + lb_ref[...]
    v = v_ref[...]
    yy = yn + _head_sum(r_ref[...] * k_ref[...] * rk_ref[...], seg, seg_t) * v
    out = _dot((yy * _silu(z_ref[...])).astype(BF16), w_ref[...])
    out_ref[...] = _deepnorm(x_ref[...], out, g_ref[...], b_ref[...])


def _rwkv_out(y, r, k, v, z, x, wt, g, b, tm=256):
    m, d = x.shape
    tm = min(tm, m)
    vec = _full_spec((1, d))
    return pl.pallas_call(
        _rwkv_out_kernel,
        grid=(m // tm,),
        in_specs=[_row_spec(tm, d)] * 6 + [vec, vec, vec, _full_spec(wt["seg"].shape),
                                           _full_spec(wt["seg_t"].shape), _full_spec((d, d)), vec, vec],
        out_specs=_row_spec(tm, d),
        out_shape=jax.ShapeDtypeStruct((m, d), F32),
        compiler_params=_cparams("parallel"),
        name="rwkv_out",
    )(y, r, k, v, z, x, wt["r_k"], wt["lnx_g"], wt["lnx_b"], wt["seg"], wt["seg_t"], wt["w_out"],
      g.reshape(1, d), b.reshape(1, d))


def _rwkv_layer(x, s0, x_last, wt, ln_g, ln_b):
    bsz, t, d = x.shape
    x_prev = jnp.concatenate([x_last[:, None, :], x[:, :-1]], axis=1)
    x2 = x.reshape(bsz * t, d)
    r, lw, k, v, kkn, a, z = _rwkv_proj(x2, x_prev.reshape(bsz * t, d), wt)
    if s0 is None:
        b3 = lambda u: u.reshape(bsz, t, d)
        y, s1 = _wkv_chunked(b3(r), b3(lw), b3(k), b3(v), b3(kkn), b3(a))
        y = y.reshape(bsz * t, d)
    else:
        y, s1 = _wkv_step(s0, r, lw, k, v, kkn, a)
    x_new = _rwkv_out(y, r, k, v, z, x2, wt, ln_g, ln_b)
    return x_new.reshape(bsz, t, d), s1, x[:, -1]


def _run_trunk(x, ln_g, ln_b, a_wts, b_wts, c_wts, past=None):
    bsz = x.shape[0]
    new = {k: [] for k in ("cmp", "sel", "win", "re", "im", "wkv", "shift")}
    for i in range(DEPTH):
        j, kind = divmod(i, N_MIXERS)
        if kind == 0:
            if past is None:
                x, kc, ks, kw = _nsa_layer_prompt(x, a_wts[j], ln_g[i], ln_b[i])
                kw = kw[:, kw.shape[1] - min(WINDOW, kw.shape[1]):]
            else:
                x, kc, ks, kw = _nsa_layer_sample(x, a_wts[j], ln_g[i], ln_b[i], j, past["cmp"],
                                                  past["sel"], past["win"], past["page_table"])
            new["cmp"].append(kc)
            new["sel"].append(ks)
            new["win"].append(kw)
        elif kind == 1:
            if past is None:
                h0r = h0i = jnp.zeros((bsz, S5_GROUPS, S5_STATE), F32)
            else:
                h0r, h0i = past["s5_re"][j], past["s5_im"][j]
            x, hr, hi = _s5_layer(x, h0r, h0i, b_wts[j], ln_g[i], ln_b[i])
            new["re"].append(hr)
            new["im"].append(hi)
        else:
            if past is None:
                s0, x_last = None, jnp.zeros((bsz, D_MODEL), F32)
            else:
                s0, x_last = past["wkv"][j], past["shift"][j]
            x, s1, xl = _rwkv_layer(x, s0, x_last, c_wts[j], ln_g[i], ln_b[i])
            new["wkv"].append(s1)
            new["shift"].append(xl)
    kv_tail = (2, A_KV_HEADS, A_HEAD_DIM)
    kv = lambda rows: jnp.stack(rows).reshape((len(rows),) + rows[0].shape[:2] + kv_tail)
    return (x, kv(new["cmp"]), kv(new["sel"]), kv(new["win"]), jnp.stack(new["re"]), jnp.stack(new["im"]),
            jnp.stack(new["wkv"]), jnp.stack(new["shift"]))


def kernel(x_prompt, x_sample, cache_cmp_kv, cache_sel_kv, cache_win_kv, state_s5_re, state_s5_im,
           state_wkv, state_shift, page_table, ln_g, ln_b,
           a_w_in, a_cmp_pe, a_cmp_w1, a_cmp_w2, a_w_out,
           b_w_in, b_log_dt, b_a_re, b_a_im, b_b_re, b_b_im, b_c_re, b_c_im, b_d, b_w_glu, b_b_glu, b_w_out,
           c_mu, c_w_in, c_w0, c_w2, c_a0, c_a2, c_k_k, c_k_a, c_r_k, c_lnx_g, c_lnx_b, c_w_out):
    a_par = (a_w_in, a_cmp_pe, a_cmp_w1, a_cmp_w2, a_w_out)
    b_par = (b_w_in, b_log_dt, b_a_re, b_a_im, b_b_re, b_b_im, b_c_re, b_c_im, b_d, b_w_glu, b_b_glu, b_w_out)
    c_par = (c_mu, c_w_in, c_w0, c_w2, c_a0, c_a2, c_k_k, c_k_a, c_r_k, c_lnx_g, c_lnx_b, c_w_out)
    a_wts = [_nsa_weights(*[p[j] for p in a_par]) for j in range(a_w_in.shape[0])]
    b_wts = [_s5_weights(*[p[j] for p in b_par]) for j in range(b_w_in.shape[0])]
    c_wts = [_rwkv_weights(*[p[j] for p in c_par]) for j in range(c_w_in.shape[0])]

    (y_p, p_cmp, p_sel, p_win, p_re, p_im, p_wkv, p_shift) = _run_trunk(
        x_prompt, ln_g, ln_b, a_wts, b_wts, c_wts)
    past = dict(cmp=cache_cmp_kv, sel=cache_sel_kv, win=cache_win_kv, page_table=page_table,
                s5_re=state_s5_re, s5_im=state_s5_im, wkv=state_wkv, shift=state_shift)
    (y_s, s_cmp, s_sel, s_win, s_re, s_im, s_wkv, s_shift) = _run_trunk(
        x_sample, ln_g, ln_b, a_wts, b_wts, c_wts, past)
    return (y_p, y_s, p_cmp, s_cmp, p_sel, s_sel, p_win, s_win,
            p_re, s_re, p_im, s_im, p_wkv, s_wkv, p_shift, s_shift)
```

```python
import functools
import math

import numpy as np
import jax
import jax.numpy as jnp
from jax import lax
from jax.experimental import pallas as pl
from jax.experimental.pallas import tpu as pltpu

F32 = jnp.float32
BF16 = jnp.bfloat16
HIGHEST = lax.Precision.HIGHEST

D_MODEL = 1024
DEPTH = 4
N_MIXERS = 3
PAGE_SIZE = 128
PAGE_GROUP = 16

A_HEADS = 16
A_KV_HEADS = 4
A_HEAD_DIM = 64
A_GROUP = 4
CMP_BLOCK = 32
CMP_STRIDE = 16
CMP_RATIO = 2
CMP_HIDDEN = 128
SEL_BLOCK = 64
SEL_RATIO = 4
SEL_TOPK = 16
WINDOW = 512
A_KV_WIDTH = 2 * A_KV_HEADS * A_HEAD_DIM

S5_GROUP_CH = 16
S5_GROUPS = 64
S5_STATE = 64
S5_SETS = 4

C_HEAD_DIM = 64
C_HEADS = 16
C_LN_EPS = 64e-5
WKV_CHUNK = 64

LN_EPS = 1e-5
DEEPNORM_ALPHA = (2 * DEPTH) ** 0.25

NEG_BIG = -1e30
LOG2E = 1.4426950408889634
POS_FEATURES = 6
MAX_POS = 64 * 256
VMEM_LIMIT = 56 * 1024 * 1024


def _cparams(*sem):
    return pltpu.CompilerParams(dimension_semantics=sem, vmem_limit_bytes=VMEM_LIMIT)


def _dot(a, b):
    return jnp.dot(a, b, preferred_element_type=F32)


def _dot_nt(a, b):
    return lax.dot_general(a, b, (((1,), (1,)), ((), ())), preferred_element_type=F32)


def _dot_hi(a, b):
    return jnp.dot(a, b, preferred_element_type=F32, precision=HIGHEST)


def _dot_nt_hi(a, b):
    return lax.dot_general(a, b, (((1,), (1,)), ((), ())), preferred_element_type=F32, precision=HIGHEST)


def _split3(x):
    hi = x.astype(BF16)
    r1 = x - hi.astype(F32)
    mid = r1.astype(BF16)
    lo = (r1 - mid.astype(F32)).astype(BF16)
    return hi, mid, lo


def _dot3(x, m01):
    hi, mid, lo = _split3(x)
    return _dot(hi, m01) + _dot(mid, m01) + _dot(lo, m01)


def _silu(z):
    return z * jax.nn.sigmoid(z)


def _proj_kernel(x_ref, *refs, n_out):
    xb = x_ref[...].astype(BF16)
    for w_ref, o_ref in zip(refs[:n_out], refs[n_out:]):
        o_ref[...] = _dot(xb, w_ref[...])


def _multi_proj(x, ws, tm=256):
    m, k = x.shape
    tm = min(tm, m)
    n_out = len(ws)
    return pl.pallas_call(
        functools.partial(_proj_kernel, n_out=n_out),
        grid=(m // tm,),
        in_specs=[pl.BlockSpec((tm, k), lambda i: (i, 0))]
        + [pl.BlockSpec(w.shape, lambda i: (0, 0)) for w in ws],
        out_specs=[pl.BlockSpec((tm, w.shape[1]), lambda i: (i, 0)) for w in ws],
        out_shape=[jax.ShapeDtypeStruct((m, w.shape[1]), F32) for w in ws],
        compiler_params=_cparams("parallel"),
        name="multi_proj",
    )(x, *ws)


def _deepnorm(x, y, g, b):
    v = DEEPNORM_ALPHA * x + y
    mu = jnp.mean(v, axis=-1, keepdims=True)
    var = jnp.mean(jnp.square(v - mu), axis=-1, keepdims=True)
    return (v - mu) * lax.rsqrt(var + LN_EPS) * g + b


def _out_ln_kernel(o_ref, z_ref, x_ref, w_ref, g_ref, b_ref, out_ref):
    gated = (o_ref[...] * _silu(z_ref[...])).astype(BF16)
    y = _dot(gated, w_ref[...])
    out_ref[...] = _deepnorm(x_ref[...], y, g_ref[...], b_ref[...])


def _row_spec(tm, n):
    return pl.BlockSpec((tm, n), lambda i: (i, 0))


def _full_spec(shape):
    return pl.BlockSpec(shape, lambda i: (0,) * len(shape))


def _out_ln(o, z, x, w_out, g, b, tm=256):
    m, d = x.shape
    tm = min(tm, m)
    return pl.pallas_call(
        _out_ln_kernel,
        grid=(m // tm,),
        in_specs=[_row_spec(tm, d), _row_spec(tm, d), _row_spec(tm, d),
                  _full_spec(w_out.shape), _full_spec((1, d)), _full_spec((1, d))],
        out_specs=_row_spec(tm, d),
        out_shape=jax.ShapeDtypeStruct((m, d), F32),
        compiler_params=_cparams("parallel"),
        name="out_ln",
    )(o, z, x, w_out, g.reshape(1, d), b.reshape(1, d))


def _pe_term_kernel(pe_ref, w1_ref, o_ref):
    for c in range(2):
        o_ref[c] = _dot_hi(pe_ref[c], w1_ref[c])


def _compress_weights(pe, w1, w2):
    pe_term = pl.pallas_call(
        _pe_term_kernel, out_shape=jax.ShapeDtypeStruct((2, 1, CMP_HIDDEN), F32), name="cmp_pe_term",
    )(pe.reshape(2, 1, CMP_BLOCK * A_HEAD_DIM), w1)
    w1r = w1.astype(BF16).reshape(2, CMP_RATIO, CMP_STRIDE // 2, 2, A_HEAD_DIM, CMP_HIDDEN)
    eye = jnp.eye(2, dtype=BF16)
    w_big = jnp.einsum('crtjdh,pg->ctjpdgrh', w1r, eye)
    w_big = w_big.reshape(2, CMP_STRIDE // 2, 4 * A_HEAD_DIM, 2 * CMP_RATIO * CMP_HIDDEN)
    return dict(pe_term=pe_term, w_big=w_big, w2=w2.astype(BF16))


def _chunk_proj(load_tap, wbig_ref):
    acc = [[None, None] for _ in range(2)]
    for c in range(2):
        for pair in range(2):
            for tp in range(CMP_STRIDE // 2):
                xs = jnp.concatenate([load_tap(c, pair, 2 * tp), load_tap(c, pair, 2 * tp + 1)], axis=1)
                part = _dot(xs.astype(BF16), wbig_ref[c, tp])
                acc[c][pair] = part if acc[c][pair] is None else acc[c][pair] + part
    return acc


def _compress_finish(acc, pe_ref, w2_ref, k_ref, v_ref, n_chunk):
    for c, o_ref in ((0, k_ref), (1, v_ref)):
        for g in range(A_KV_HEADS):
            pair, half = divmod(g, 2)
            base = half * CMP_RATIO * CMP_HIDDEN
            r0 = acc[c][pair][:, base:base + CMP_HIDDEN]
            r1 = acc[c][pair][:, base + CMP_HIDDEN:base + 2 * CMP_HIDDEN]
            h = r0 + pltpu.roll(r1, n_chunk - 1, 0) + pe_ref[c]
            o_ref[g] = _dot(_silu(h).astype(BF16), w2_ref[c])


def _compress_kernel(x_ref, pe_ref, wbig_ref, w2_ref, k_ref, v_ref, *, n_chunk):
    row_stride = CMP_STRIDE * 4
    load_tap = lambda c, pair, s: x_ref[pl.ds(s * 4 + c * 2 + pair, n_chunk, stride=row_stride), :]
    acc = _chunk_proj(load_tap, wbig_ref)
    _compress_finish(acc, pe_ref, w2_ref, k_ref, v_ref, n_chunk)


def _compress_prompt(kv_c, cw):
    bsz, t, _ = kv_c.shape
    n_chunk = t // CMP_STRIDE
    out = jax.ShapeDtypeStruct((bsz, A_KV_HEADS, n_chunk, A_HEAD_DIM), F32)
    out_spec = pl.BlockSpec((None, A_KV_HEADS, n_chunk, A_HEAD_DIM), lambda b: (b, 0, 0, 0))
    return pl.pallas_call(
        functools.partial(_compress_kernel, n_chunk=n_chunk),
        grid=(bsz,),
        in_specs=[pl.BlockSpec((None, t * 4, 128), lambda b: (b, 0, 0)), _full_spec(cw["pe_term"].shape),
                  _full_spec(cw["w_big"].shape), _full_spec(cw["w2"].shape)],
        out_specs=[out_spec, out_spec],
        out_shape=[out, out],
        compiler_params=_cparams("parallel"),
        name="compress_prompt",
    )(kv_c.reshape(bsz, t * 4, 128), cw["pe_term"], cw["w_big"], cw["w2"])


def _softmax_rows(s, valid):
    s = jnp.where(valid, s, NEG_BIG)
    m = jnp.max(s, axis=-1, keepdims=True)
    e = jnp.where(valid, jnp.exp(s - m), 0.0)
    return e / jnp.maximum(jnp.sum(e, axis=-1, keepdims=True), 1e-30)


def _topk_mask(score_t, n_keep):
    n_blk = score_t.shape[0]
    blk = lax.broadcasted_iota(jnp.int32, score_t.shape, 0)
    rank = jnp.zeros(score_t.shape, F32)
    for i in range(n_blk):
        row = score_t[i:i + 1, :]
        tie_ahead = jnp.where(blk > i, 1.0, 0.0)
        rank = rank + jnp.where(row > score_t, 1.0, jnp.where(row == score_t, tie_ahead, 0.0))
    return jnp.where(rank < n_keep, 1.0, 0.0)


def _tile_heads(x):
    return jnp.concatenate([x] * A_GROUP, axis=0)


def _nsa_prompt_kernel(q_ref, gate_ref, sf_ref, kc_ref, vc_ref, ks_ref, vs_ref, kw_ref, vw_ref,
                       sel_map_ref, expand_ref, o_ref, *, qb, i_base):
    t0 = (i_base + pl.program_id(2)) * qb
    n_keys = ks_ref.shape[0]
    n_cmp_pad = kc_ref.shape[0]
    n_sel = n_keys // SEL_BLOCK

    q = q_ref[...] * (A_HEAD_DIM ** -0.5 * LOG2E)
    qg = jnp.concatenate([q[:, r * 64:(r + 1) * 64] for r in range(A_GROUP)], axis=0)
    qa = jnp.concatenate([qg, sf_ref[...]], axis=1).astype(BF16)
    t_q = t0 + lax.broadcasted_iota(jnp.int32, (qb, 1), 0)

    n_idx = lax.broadcasted_iota(jnp.int32, (qb, n_cmp_pad), 1)
    ok_c = n_idx * CMP_STRIDE + (CMP_BLOCK - 1) <= t_q
    s_c = _dot_nt(qa, kc_ref[...]) + _tile_heads(jnp.where(ok_c, 0.0, NEG_BIG))
    m_c = jnp.max(s_c, axis=-1, keepdims=True)
    e_c = jnp.where(s_c > 0.5 * NEG_BIG, jnp.exp2(s_c - m_c), 0.0)
    p_c = e_c / jnp.maximum(jnp.sum(e_c, axis=-1, keepdims=True), 1e-30)
    o_c = _dot(p_c.astype(BF16), vc_ref[...])

    p_grp = p_c[0:qb] + p_c[qb:2 * qb] + p_c[2 * qb:3 * qb] + p_c[3 * qb:4 * qb]
    sel_map = sel_map_ref[...]
    hi, mid, lo = _split3(p_grp)
    p_slc_t = _dot_nt(sel_map, hi) + _dot_nt(sel_map, mid) + _dot_nt(sel_map, lo)
    blk = lax.broadcasted_iota(jnp.int32, (n_sel, qb), 0)
    cur = (t0 + lax.broadcasted_iota(jnp.int32, (n_sel, qb), 1)) // SEL_BLOCK
    forced = (blk == 0) | (blk == cur) | (blk == cur - 1)
    score_t = jnp.where(forced, 1e30, jnp.where(blk <= cur, p_slc_t, -1.0))
    keep_t = jnp.where(blk <= cur, _topk_mask(score_t, SEL_TOPK), 0.0)
    keep = keep_t.T.astype(BF16)

    k_pos = lax.broadcasted_iota(jnp.int32, (qb, n_keys), 1)
    bias = jnp.where(k_pos > t_q, NEG_BIG, (_dot(keep, expand_ref[...]) - 1.0) * (-NEG_BIG))
    s_s = _dot_nt(qa, ks_ref[...]) + _tile_heads(bias)
    p_s = jnp.exp2(s_s - jnp.max(s_s, axis=-1, keepdims=True))
    o_s = _dot(p_s.astype(BF16), vs_ref[...]) / jnp.sum(p_s, axis=-1, keepdims=True)

    n_win = min(WINDOW + qb, n_keys)
    w0 = pl.multiple_of(jnp.maximum(t0 + qb - n_win, 0), 8)
    w_pos = w0 + lax.broadcasted_iota(jnp.int32, (qb, n_win), 1)
    ok_w = (w_pos <= t_q) & (t_q - w_pos <= WINDOW)
    s_w = _dot_nt(qa, kw_ref[pl.ds(w0, n_win), :]) + _tile_heads(jnp.where(ok_w, 0.0, NEG_BIG))
    p_w = jnp.exp2(s_w - jnp.max(s_w, axis=-1, keepdims=True))
    o_w = _dot(p_w.astype(BF16), vw_ref[pl.ds(w0, n_win), :]) / jnp.sum(p_w, axis=-1, keepdims=True)

    gate = jax.nn.sigmoid(gate_ref[...])
    for r in range(A_GROUP):
        rs = slice(r * qb, (r + 1) * qb)
        o_ref[:, r * 64:(r + 1) * 64] = (gate[:, r:r + 1] * o_c[rs]
                                         + gate[:, A_GROUP + r:A_GROUP + r + 1] * o_s[rs]
                                         + gate[:, 2 * A_GROUP + r:2 * A_GROUP + r + 1] * o_w[rs])


def _head_major(kv, dtype):
    bsz, t, _ = kv.shape
    kv = kv.reshape(bsz, t, 2, A_KV_HEADS, A_HEAD_DIM).transpose(2, 0, 3, 1, 4).astype(dtype)
    return kv[0], kv[1]


def _group_gates(gate):
    bsz, t, _ = gate.shape
    return gate.reshape(bsz, t, 3, A_KV_HEADS, A_GROUP).transpose(0, 3, 1, 2, 4).reshape(bsz, A_KV_HEADS, t, 12)


def _sel_map(n_sel, n_cmp_pad, n_cmp):
    j = np.arange(n_sel)[:, None]
    n = np.arange(n_cmp_pad)[None, :]
    m = (n >= SEL_RATIO * j - (CMP_RATIO - 1)) & (n <= SEL_RATIO * j + SEL_RATIO - 1) & (n < n_cmp)
    return jnp.asarray(m, BF16)


def _slope_features(slopes, qb):
    s = slopes * LOG2E
    s1 = s.astype(BF16).astype(F32)
    s2 = (s - s1).astype(BF16).astype(F32)
    s3 = (s - s1 - s2).astype(BF16).astype(F32)
    feat = jnp.stack([64.0 * s1, s1, 64.0 * s2, s2, 64.0 * s3, s3], axis=-1)
    feat = jnp.pad(feat, ((0, 0), (0, A_HEAD_DIM - POS_FEATURES)))
    return jnp.repeat(feat.reshape(A_KV_HEADS, A_GROUP, A_HEAD_DIM), qb, axis=1)


def _with_pos(keys, pos):
    assert int(pos.max()) < MAX_POS
    a, b = pos // 64, pos % 64
    feat = np.zeros((pos.shape[0], A_HEAD_DIM), np.float32)
    feat[:, 0:POS_FEATURES:2] = a[:, None]
    feat[:, 1:POS_FEATURES:2] = b[:, None]
    feat = jnp.broadcast_to(jnp.asarray(feat, BF16), keys.shape)
    return jnp.concatenate([keys.astype(BF16), feat], axis=-1)


def _nsa_prompt_attn(q, gate, kv_s, kv_w, k_cmp, v_cmp, slopes, qb=128, key_step=512):
    bsz, t, _ = q.shape
    key_step = min(key_step, t)
    pos = np.arange(t)
    ks, vs = _head_major(kv_s, BF16)
    kw, vw = _head_major(kv_w, BF16)
    ks, kw = _with_pos(ks, pos), _with_pos(kw, pos)
    k_cmp = _with_pos(k_cmp, np.arange(k_cmp.shape[2]) * CMP_STRIDE + (CMP_BLOCK - 1))
    v_cmp = v_cmp.astype(BF16)
    gates = _group_gates(gate)
    slope_feat = _slope_features(slopes, qb)
    rows = A_GROUP * qb
    steps = key_step // qb
    outs = []
    for seg in range(t // key_step):
        n_keys = (seg + 1) * key_step
        n_cmp_pad = n_keys // CMP_STRIDE
        n_sel = n_keys // SEL_BLOCK
        expand = jnp.asarray(np.arange(n_keys)[None, :] // SEL_BLOCK == np.arange(n_sel)[:, None], BF16)
        i_base = seg * steps
        head = lambda rows_, width: pl.BlockSpec((None, None, rows_, width), lambda b, g, i: (b, g, 0, 0))
        outs.append(pl.pallas_call(
            functools.partial(_nsa_prompt_kernel, qb=qb, i_base=i_base),
            grid=(bsz, A_KV_HEADS, steps),
            in_specs=[pl.BlockSpec((None, qb, 256), lambda b, g, i, i_base=i_base: (b, i_base + i, g)),
                      pl.BlockSpec((None, None, qb, 12), lambda b, g, i, i_base=i_base: (b, g, i_base + i, 0)),
                      pl.BlockSpec((None, rows, A_HEAD_DIM), lambda b, g, i: (g, 0, 0)),
                      head(n_cmp_pad, 2 * A_HEAD_DIM), head(n_cmp_pad, A_HEAD_DIM),
                      head(n_keys, 2 * A_HEAD_DIM), head(n_keys, A_HEAD_DIM),
                      head(n_keys, 2 * A_HEAD_DIM), head(n_keys, A_HEAD_DIM),
                      pl.BlockSpec((n_sel, n_cmp_pad), lambda b, g, i: (0, 0)),
                      pl.BlockSpec((n_sel, n_keys), lambda b, g, i: (0, 0))],
            out_specs=pl.BlockSpec((None, qb, 256), lambda b, g, i: (b, i, g)),
            out_shape=jax.ShapeDtypeStruct((bsz, key_step, D_MODEL), F32),
            compiler_params=_cparams("parallel", "parallel", "arbitrary"),
            name=f"nsa_prompt_attn_{n_keys}",
        )(q, gates, slope_feat, k_cmp, v_cmp, ks, vs, kw, vw,
          _sel_map(n_sel, n_cmp_pad, t // CMP_STRIDE - 1), expand))
    return jnp.concatenate(outs, axis=1)


def _alibi_slopes():
    return 2.0 ** (-8.0 * jnp.arange(1, A_HEADS + 1, dtype=F32) / A_HEADS)


def _nsa_weights(w_in, pe, w1, w2, w_out):
    cuts = [int(c) for c in np.cumsum([D_MODEL, A_KV_WIDTH, A_KV_WIDTH, A_KV_WIDTH, 3 * A_HEADS])]
    pieces = jnp.split(w_in.astype(BF16), cuts, axis=-1)
    return dict(w_pieces=pieces, cmp=_compress_weights(pe, w1, w2), w_out=w_out.astype(BF16))


def _nsa_layer_prompt(x, wt, ln_g, ln_b):
    bsz, t, d = x.shape
    x2 = x.reshape(bsz * t, d)
    q, kv_c, kv_s, kv_w, gate, z = _multi_proj(x2, wt["w_pieces"])
    kv_c3, kv_s3, kv_w3 = (a.reshape(bsz, t, A_KV_WIDTH) for a in (kv_c, kv_s, kv_w))
    k_cmp, v_cmp = _compress_prompt(kv_c3, wt["cmp"])
    o = _nsa_prompt_attn(q.reshape(bsz, t, d), gate.reshape(bsz, t, 3 * A_HEADS), kv_s3, kv_w3,
                         k_cmp, v_cmp, _alibi_slopes())
    x_new = _out_ln(o.reshape(bsz * t, d), z, x2, wt["w_out"], ln_g, ln_b)
    return x_new.reshape(bsz, t, d), kv_c3, kv_s3, kv_w3


def _token_minor(cache):
    nd = cache.ndim
    return cache.transpose(*range(nd - 4), nd - 3, nd - 2, nd - 1, nd - 4)


def _page_group_copies(cache_ref, layer, pt_ref, q, buf_ref, sem_ref):
    slot = lax.rem(q, 2)
    return [pltpu.make_async_copy(cache_ref.at[layer, pt_ref[q * PAGE_GROUP + i]], buf_ref.at[slot, i],
                                  sem_ref.at[slot]) for i in range(PAGE_GROUP)]


def _sample_compress_kernel(pt_ref, cache_ref, pe_ref, wbig_ref, w2_ref, k_ref, v_ref, pg_ref, xt_ref, sem,
                            *, layer, n_pages):
    b = pl.program_id(0)
    n_grp = n_pages // PAGE_GROUP
    n_total = pl.num_programs(0) * n_grp

    @pl.when(b == 0)
    def _():
        for cp in _page_group_copies(cache_ref, layer, pt_ref, 0, pg_ref, sem):
            cp.start()

    for gi in range(n_grp):
        q = b * n_grp + gi

        @pl.when(q + 1 < n_total)
        def _():
            for cp in _page_group_copies(cache_ref, layer, pt_ref, q + 1, pg_ref, sem):
                cp.start()

        for cp in _page_group_copies(cache_ref, layer, pt_ref, q, pg_ref, sem):
            cp.wait()
        slot = lax.rem(q, 2)

        def page_body(i, carry):
            row0 = pl.multiple_of((gi * PAGE_GROUP + i) * PAGE_SIZE, PAGE_SIZE)
            for c in range(2):
                for pair in range(2):
                    rows = jnp.concatenate([pg_ref[slot, i, c, 2 * pair].T, pg_ref[slot, i, c, 2 * pair + 1].T], axis=1)
                    xt_ref[c, pair, pl.ds(row0, PAGE_SIZE), :] = rows
            return carry

        lax.fori_loop(0, PAGE_GROUP, page_body, 0)

    n_chunk = n_pages * PAGE_SIZE // CMP_STRIDE
    load_tap = lambda c, pair, s: xt_ref[c, pair, pl.ds(s, n_chunk, stride=CMP_STRIDE), :]
    acc = _chunk_proj(load_tap, wbig_ref)
    _compress_finish(acc, pe_ref, w2_ref, k_ref, v_ref, n_chunk)


def _sample_compress(cache, layer, page_table, cw):
    bsz, n_pages = page_table.shape
    assert n_pages % PAGE_GROUP == 0
    n_tok = n_pages * PAGE_SIZE
    n_chunk = n_tok // CMP_STRIDE
    out = jax.ShapeDtypeStruct((bsz, A_KV_HEADS, n_chunk, A_HEAD_DIM), F32)
    out_spec = pl.BlockSpec((None, A_KV_HEADS, n_chunk, A_HEAD_DIM), lambda b, pt: (b, 0, 0, 0))
    full = lambda a: pl.BlockSpec(a.shape, lambda b, pt: (0,) * a.ndim)
    grid_spec = pltpu.PrefetchScalarGridSpec(
        num_scalar_prefetch=1, grid=(bsz,),
        in_specs=[pl.BlockSpec(memory_space=pl.ANY), full(cw["pe_term"]), full(cw["w_big"]), full(cw["w2"])],
        out_specs=[out_spec, out_spec],
        scratch_shapes=[pltpu.VMEM((2, PAGE_GROUP, 2, A_KV_HEADS, A_HEAD_DIM, PAGE_SIZE), F32),
                        pltpu.VMEM((2, 2, n_tok, 2 * A_HEAD_DIM), F32), pltpu.SemaphoreType.DMA((2,))])
    return pl.pallas_call(
        functools.partial(_sample_compress_kernel, layer=layer, n_pages=n_pages),
        grid_spec=grid_spec, out_shape=[out, out],
        compiler_params=_cparams("arbitrary"),
        name="sample_compress",
    )(page_table.reshape(-1), _token_minor(cache), cw["pe_term"], cw["w_big"], cw["w2"])


def _sample_cmp_kernel(q_ref, slope_ref, kc_ref, vc_ref, map_ref, oc_ref, idx_ref, *, past, n_cand):
    n_pad = kc_ref.shape[1]
    n_cmp = n_pad - 1
    n_idx = lax.broadcasted_iota(jnp.int32, (A_GROUP, n_pad), 1)
    dist = (past - (CMP_BLOCK - 1)) - n_idx * CMP_STRIDE
    valid = (dist >= 0) & (n_idx < n_cmp)
    q = q_ref[...] * (A_HEAD_DIM ** -0.5)
    n_keep = idx_ref.shape[1]
    width = map_ref.shape[1]
    lane = lax.broadcasted_iota(jnp.int32, (1, width), 1)
    row_i = lax.broadcasted_iota(jnp.int32, (width, width), 0)
    col_j = lax.broadcasted_iota(jnp.int32, (width, width), 1)
    for g in range(A_KV_HEADS):
        hs = slice(g * A_GROUP, (g + 1) * A_GROUP)
        s = _dot_nt(q[hs].astype(BF16), kc_ref[g].astype(BF16)) - slope_ref[hs] * dist.astype(F32)
        p = _softmax_rows(s, valid)
        oc_ref[hs, :] = _dot(p.astype(BF16), vc_ref[g].astype(BF16))
        p_grp = jnp.sum(p, axis=0, keepdims=True)
        p_slc = _dot3(p_grp, map_ref[...])
        forced = (lane == 0) | (lane == n_cand - 1)
        score = jnp.where(forced, 1e30, jnp.where(lane < n_cand, p_slc, -1.0))
        s_j = jnp.broadcast_to(score, (width, width))
        s_i = s_j.T
        tie_ahead = jnp.where(row_i < col_j, 1.0, 0.0)
        ahead = jnp.where(s_i > s_j, 1.0, jnp.where(s_i == s_j, tie_ahead, 0.0))
        rank = jnp.sum(ahead, axis=0, keepdims=True)
        want = lax.broadcasted_iota(jnp.int32, (n_keep, width), 0).astype(F32)
        picked = jnp.where(rank == want, lane.astype(F32), 0.0)
        idx_ref[g] = jnp.sum(picked, axis=-1, keepdims=True).astype(jnp.int32)


def _sample_cmp_topk(q, k_cmp, v_cmp, slopes, past):
    bsz = q.shape[0]
    n_pad = k_cmp.shape[2]
    n_cand = past // SEL_BLOCK
    width = -(-n_cand // 128) * 128
    j = np.arange(width)[None, :]
    n = np.arange(n_pad)[:, None]
    sel_map = (n >= SEL_RATIO * j - (CMP_RATIO - 1)) & (n <= SEL_RATIO * j + SEL_RATIO - 1) & (n < n_pad - 1)
    cmp_spec = pl.BlockSpec((None, A_KV_HEADS, n_pad, A_HEAD_DIM), lambda b: (b, 0, 0, 0))
    return pl.pallas_call(
        functools.partial(_sample_cmp_kernel, past=past, n_cand=n_cand),
        grid=(bsz,),
        in_specs=[pl.BlockSpec((None, A_HEADS, A_HEAD_DIM), lambda b: (b, 0, 0)),
                  _full_spec((A_HEADS, 1)), cmp_spec, cmp_spec, _full_spec((n_pad, width))],
        out_specs=[pl.BlockSpec((None, A_HEADS, A_HEAD_DIM), lambda b: (b, 0, 0)),
                   pl.BlockSpec((None, A_KV_HEADS, SEL_TOPK, 1), lambda b: (b, 0, 0, 0))],
        out_shape=[jax.ShapeDtypeStruct((bsz, A_HEADS, A_HEAD_DIM), F32),
                   jax.ShapeDtypeStruct((bsz, A_KV_HEADS, SEL_TOPK, 1), jnp.int32)],
        compiler_params=_cparams("parallel"),
        name="sample_cmp_topk",
    )(q, slopes.reshape(A_HEADS, 1), k_cmp, v_cmp, jnp.asarray(sel_map, BF16))


def _sel_tile_copies(cache_ref, layer, pt_ref, idx_ref, b, g, r, n_pages, buf_ref, sem):
    blk = idx_ref[(b * A_KV_HEADS + g) * SEL_TOPK + r]
    page = pt_ref[b * n_pages + blk // (PAGE_SIZE // SEL_BLOCK)]
    return [pltpu.make_async_copy(cache_ref.at[layer, page, c, g], buf_ref.at[g, r, c], sem) for c in range(2)]


def _sample_attn_kernel(pt_ref, idx_ref, cache_ref, q_ref, gate_ref, slope_ref, oc_ref, ks_new_ref,
                        kw_new_ref, win_ref, o_ref, buf_ref, sem, *, layer, n_pages, past):
    b = pl.program_id(0)
    n_blk = SEL_TOPK - 1
    for g in range(A_KV_HEADS):
        for r in range(n_blk):
            for cp in _sel_tile_copies(cache_ref, layer, pt_ref, idx_ref, b, g, r, n_pages, buf_ref, sem):
                cp.start()

    q = q_ref[...] * (A_HEAD_DIM ** -0.5)
    gate = jax.nn.sigmoid(gate_ref[...])
    n_buf = win_ref.shape[-1]
    tok = lax.broadcasted_iota(jnp.int32, (1, PAGE_SIZE), 1)
    dist_w = (n_buf - lax.broadcasted_iota(jnp.int32, (1, n_buf), 1)).astype(F32)

    def attend(qg, slope, keys_t, vals_t, dist, valid, k_new, v_new):
        s = _dot(qg.astype(BF16), keys_t.astype(BF16)) - slope * dist
        if valid is not None:
            s = jnp.where(valid, s, NEG_BIG)
        s_new = jnp.sum(qg * k_new, axis=-1, keepdims=True)
        m = jnp.maximum(jnp.max(s, axis=-1, keepdims=True), s_new)
        e = jnp.exp(s - m)
        e_new = jnp.exp(s_new - m)
        l = jnp.sum(e, axis=-1, keepdims=True) + e_new
        return (_dot_nt(e.astype(BF16), vals_t.astype(BF16)) + e_new * v_new) / l

    def new_row(ref, g):
        ksl = slice(g * A_HEAD_DIM, (g + 1) * A_HEAD_DIM)
        vsl = slice(A_KV_WIDTH // 2 + g * A_HEAD_DIM, A_KV_WIDTH // 2 + (g + 1) * A_HEAD_DIM)
        return ref[:, ksl], ref[:, vsl]

    out_w = []
    for g in range(A_KV_HEADS):
        hs = slice(g * A_GROUP, (g + 1) * A_GROUP)
        out_w.append(attend(q[hs], slope_ref[hs], win_ref[0, g], win_ref[1, g], dist_w, None,
                            *new_row(kw_new_ref, g)))

    for g in range(A_KV_HEADS):
        for r in range(n_blk):
            for cp in _sel_tile_copies(cache_ref, layer, pt_ref, idx_ref, b, g, r, n_pages, buf_ref, sem):
                cp.wait()

    blocks_per_page = PAGE_SIZE // SEL_BLOCK
    for g in range(A_KV_HEADS):
        hs = slice(g * A_GROUP, (g + 1) * A_GROUP)
        keys_t = jnp.concatenate([buf_ref[g, r, 0] for r in range(n_blk)], axis=1)
        vals_t = jnp.concatenate([buf_ref[g, r, 1] for r in range(n_blk)], axis=1)
        dist, valid = [], []
        for r in range(n_blk):
            blk = idx_ref[(b * A_KV_HEADS + g) * SEL_TOPK + r]
            dist.append(past - ((blk // blocks_per_page) * PAGE_SIZE + tok))
            valid.append(tok // SEL_BLOCK == blk % blocks_per_page)
        dist = jnp.concatenate(dist, axis=1).astype(F32)
        valid = jnp.concatenate(valid, axis=1)
        o_s = attend(q[hs], slope_ref[hs], keys_t, vals_t, dist, valid, *new_row(ks_new_ref, g))
        gt = gate[hs]
        o_ref[hs, :] = gt[:, 0:1] * oc_ref[hs, :] + gt[:, 1:2] * o_s + gt[:, 2:3] * out_w[g]


def _sample_attn(page_table, idx, cache_sel, layer, q, gate, slopes, o_c, kv_s_new, kv_w_new, cache_win, past):
    bsz, n_pages = page_table.shape
    n_buf = cache_win.shape[2]
    head_spec = pl.BlockSpec((None, A_HEADS, A_HEAD_DIM), lambda b, pt, ix: (b, 0, 0))
    new_spec = pl.BlockSpec((None, 1, A_KV_WIDTH), lambda b, pt, ix: (b, 0, 0))
    grid_spec = pltpu.PrefetchScalarGridSpec(
        num_scalar_prefetch=2, grid=(bsz,),
        in_specs=[pl.BlockSpec(memory_space=pl.ANY), head_spec,
                  pl.BlockSpec((None, A_HEADS, 3), lambda b, pt, ix: (b, 0, 0)),
                  pl.BlockSpec((A_HEADS, 1), lambda b, pt, ix: (0, 0)),
                  head_spec, new_spec, new_spec,
                  pl.BlockSpec((None, None, 2, A_KV_HEADS, A_HEAD_DIM, n_buf),
                               lambda b, pt, ix: (layer, b, 0, 0, 0, 0))],
        out_specs=head_spec,
        scratch_shapes=[pltpu.VMEM((A_KV_HEADS, SEL_TOPK - 1, 2, A_HEAD_DIM, PAGE_SIZE), F32),
                        pltpu.SemaphoreType.DMA(())])
    return pl.pallas_call(
        functools.partial(_sample_attn_kernel, layer=layer, n_pages=n_pages, past=past),
        grid_spec=grid_spec,
        out_shape=jax.ShapeDtypeStruct((bsz, A_HEADS, A_HEAD_DIM), F32),
        compiler_params=_cparams("arbitrary"),
        name="sample_attn",
    )(page_table.reshape(-1), idx.reshape(-1), _token_minor(cache_sel), q, gate, slopes.reshape(A_HEADS, 1),
      o_c, kv_s_new, kv_w_new, _token_minor(cache_win))


def _nsa_layer_sample(x, wt, ln_g, ln_b, layer, cache_cmp, cache_sel, cache_win, page_table):
    bsz, _, d = x.shape
    n_pages = page_table.shape[1]
    past = n_pages * PAGE_SIZE
    x2 = x.reshape(bsz, d)
    q, kv_c, kv_s, kv_w, gate, z = _multi_proj(x2, wt["w_pieces"])
    slopes = _alibi_slopes()
    k_cmp, v_cmp = _sample_compress(cache_cmp, layer, page_table, wt["cmp"])
    q3 = q.reshape(bsz, A_HEADS, A_HEAD_DIM)
    o_c, idx = _sample_cmp_topk(q3, k_cmp, v_cmp, slopes, past)
    gate3 = gate.reshape(bsz, 3, A_HEADS).transpose(0, 2, 1)
    kv_s3, kv_w3 = kv_s.reshape(bsz, 1, A_KV_WIDTH), kv_w.reshape(bsz, 1, A_KV_WIDTH)
    o = _sample_attn(page_table, idx, cache_sel, layer, q3, gate3, slopes, o_c, kv_s3, kv_w3, cache_win, past)
    x_new = _out_ln(o.reshape(bsz, d), z, x2, wt["w_out"], ln_g, ln_b)
    n_buf = cache_win.shape[2]
    n_keep = min(WINDOW, n_buf + 1)
    win_old = cache_win[layer, :, n_buf + 1 - n_keep:].reshape(bsz, n_keep - 1, A_KV_WIDTH)
    win_new = jnp.concatenate([win_old, kv_w3], axis=1)
    return x_new.reshape(bsz, 1, d), kv_c.reshape(bsz, 1, A_KV_WIDTH), kv_s3, win_new


def _s5_param_kernel(log_dt_ref, ar_ref, ai_ref, br_ref, bi_ref, tile_ref,
                     abr_ref, abi_ref, bbr_ref, bbi_ref):
    dt = jnp.exp(log_dt_ref[...])
    ar, ai = ar_ref[...], ai_ref[...]
    mag = jnp.exp(dt * ar)
    abr, abi = mag * jnp.cos(dt * ai), mag * jnp.sin(dt * ai)
    den = ar * ar + ai * ai
    num_re, num_im = abr - 1.0, abi
    zoh_re = (num_re * ar + num_im * ai) / den
    zoh_im = (num_im * ar - num_re * ai) / den
    abr_ref[...] = abr
    abi_ref[...] = abi
    zr = _dot3(zoh_re, tile_ref[...])
    zi = _dot3(zoh_im, tile_ref[...])
    br, bi = br_ref[...], bi_ref[...]
    bbr_ref[...] = zr * br - zi * bi
    bbi_ref[...] = zr * bi + zi * br


def _s5_weights(w_in, log_dt, a_re, a_im, b_re, b_im, c_re, c_im, d_skip, w_glu, b_glu, w_out):
    g, p, c = S5_GROUPS, S5_STATE, S5_GROUP_CH
    tile = jnp.asarray(np.tile(np.eye(p, dtype=np.float32), (1, c)), BF16)
    to_gcp = lambda b: b.transpose(0, 2, 1).reshape(g, c * p)
    shapes = [jax.ShapeDtypeStruct((g, p), F32)] * 2 + [jax.ShapeDtypeStruct((g, c * p), F32)] * 2
    abr, abi, bbr, bbi = pl.pallas_call(_s5_param_kernel, out_shape=shapes, name="s5_params")(
        log_dt.reshape(g, 1), a_re, a_im, to_gcp(b_re), to_gcp(b_im), tile)
    eye = jnp.eye(S5_GROUPS // S5_SETS, dtype=F32)
    gl = S5_GROUPS // S5_SETS

    def in_blockdiag(bb):
        bb = bb.reshape(S5_SETS, gl, c, p)
        return jnp.einsum('sgcp,gh->sgchp', bb, eye).reshape(S5_SETS, gl * c, gl * p).astype(BF16)

    def out_blockdiag(cc):
        cc = cc.reshape(S5_SETS, gl, c, p)
        return jnp.einsum('sgcp,gh->sgphc', cc, eye).reshape(S5_SETS, gl * p, gl * c).astype(BF16)

    w_u, w_z = jnp.split(w_in.astype(BF16), 2, axis=-1)
    return dict(w_pieces=[w_u, w_z], abr=abr.reshape(1, g * p), abi=abi.reshape(1, g * p),
                wb_re=in_blockdiag(bbr), wb_im=in_blockdiag(bbi),
                wc_re=out_blockdiag(c_re), wc_im=out_blockdiag(c_im),
                d_skip=d_skip.reshape(1, -1), w_glu=w_glu.astype(BF16), b_glu=b_glu.reshape(1, -1),
                w_out=w_out.astype(BF16))


def _s5_scan_kernel(u_ref, wbr_ref, wbi_ref, ar_ref, ai_ref, wcr_ref, wci_ref, h0r_ref, h0i_ref,
                    y_ref, hr_out, hi_out, bur, bui, hr_c, hi_c, *, tc):
    @pl.when(pl.program_id(1) == 0)
    def _():
        hr_c[...] = h0r_ref[...]
        hi_c[...] = h0i_ref[...]

    n_set = S5_GROUPS * S5_STATE // S5_SETS
    n_ch = S5_GROUPS * S5_GROUP_CH // S5_SETS
    ub = u_ref[...].astype(BF16)
    for s in range(S5_SETS):
        us = ub[:, s * n_ch:(s + 1) * n_ch]
        bur[:, s * n_set:(s + 1) * n_set] = _dot(us, wbr_ref[s])
        bui[:, s * n_set:(s + 1) * n_set] = _dot(us, wbi_ref[s])

    for s in range(S5_SETS):
        sl = pl.ds(s * n_set, n_set)
        ar, ai = ar_ref[:, sl], ai_ref[:, sl]

        def step(t, carry):
            hr, hi = carry
            nr = ar * hr - ai * hi + bur[pl.ds(t, 1), sl]
            ni = ar * hi + ai * hr + bui[pl.ds(t, 1), sl]
            bur[pl.ds(t, 1), sl] = nr
            bui[pl.ds(t, 1), sl] = ni
            return nr, ni

        hr, hi = lax.fori_loop(0, tc, step, (hr_c[:, sl], hi_c[:, sl]))
        hr_c[:, sl] = hr
        hi_c[:, sl] = hi

    for s in range(S5_SETS):
        h_re = bur[:, s * n_set:(s + 1) * n_set]
        h_im = bui[:, s * n_set:(s + 1) * n_set]
        re_hi = h_re.astype(BF16)
        re_lo = (h_re - re_hi.astype(F32)).astype(BF16)
        im_hi = h_im.astype(BF16)
        im_lo = (h_im - im_hi.astype(F32)).astype(BF16)
        y = (_dot(re_hi, wcr_ref[s]) + _dot(re_lo, wcr_ref[s])
             - _dot(im_hi, wci_ref[s]) - _dot(im_lo, wci_ref[s]))
        y_ref[:, s * n_ch:(s + 1) * n_ch] = y
    hr_out[...] = hr_c[...]
    hi_out[...] = hi_c[...]


def _s5_scan(u, h0_re, h0_im, wt, tc=256):
    bsz, t, d = u.shape
    tc = min(tc, t)
    n_state = S5_GROUPS * S5_STATE
    st_spec = pl.BlockSpec((None, 1, n_state), lambda b, j: (b, 0, 0))
    full = lambda a: pl.BlockSpec(a.shape, lambda b, j: (0,) * a.ndim)
    st_shape = jax.ShapeDtypeStruct((bsz, 1, n_state), F32)
    return pl.pallas_call(
        functools.partial(_s5_scan_kernel, tc=tc),
        grid=(bsz, t // tc),
        in_specs=[pl.BlockSpec((None, tc, d), lambda b, j: (b, j, 0)),
                  full(wt["wb_re"]), full(wt["wb_im"]), full(wt["abr"]), full(wt["abi"]),
                  full(wt["wc_re"]), full(wt["wc_im"]), st_spec, st_spec],
        out_specs=[pl.BlockSpec((None, tc, d), lambda b, j: (b, j, 0)), st_spec, st_spec],
        out_shape=[jax.ShapeDtypeStruct((bsz, t, d), F32), st_shape, st_shape],
        scratch_shapes=[pltpu.VMEM((tc, n_state), F32), pltpu.VMEM((tc, n_state), F32),
                        pltpu.VMEM((1, n_state), F32), pltpu.VMEM((1, n_state), F32)],
        compiler_params=_cparams("parallel", "arbitrary"),
        name="s5_scan",
    )(u, wt["wb_re"], wt["wb_im"], wt["abr"], wt["abi"], wt["wc_re"], wt["wc_im"], h0_re, h0_im)


def _s5_out_kernel(y_ref, u_ref, z_ref, x_ref, d_ref, wg_ref, bg_ref, w_ref, g_ref, b_ref, out_ref):
    y = jax.nn.gelu(y_ref[...] + d_ref[...] * u_ref[...])
    y = y * jax.nn.sigmoid(_dot(y.astype(BF16), wg_ref[...]) + bg_ref[...])
    out = _dot((y * _silu(z_ref[...])).astype(BF16), w_ref[...])
    out_ref[...] = _deepnorm(x_ref[...], out, g_ref[...], b_ref[...])


def _s5_out(y, u, z, x, wt, g, b, tm=256):
    m, d = x.shape
    tm = min(tm, m)
    vec = _full_spec((1, d))
    return pl.pallas_call(
        _s5_out_kernel,
        grid=(m // tm,),
        in_specs=[_row_spec(tm, d)] * 4 + [vec, _full_spec((d, d)), vec, _full_spec((d, d)), vec, vec],
        out_specs=_row_spec(tm, d),
        out_shape=jax.ShapeDtypeStruct((m, d), F32),
        compiler_params=_cparams("parallel"),
        name="s5_out",
    )(y, u, z, x, wt["d_skip"], wt["w_glu"], wt["b_glu"], wt["w_out"], g.reshape(1, d), b.reshape(1, d))


def _s5_layer(x, h0_re, h0_im, wt, ln_g, ln_b):
    bsz, t, d = x.shape
    x2 = x.reshape(bsz * t, d)
    u, z = _multi_proj(x2, wt["w_pieces"])
    n_state = S5_GROUPS * S5_STATE
    y, hr, hi = _s5_scan(u.reshape(bsz, t, d), h0_re.reshape(bsz, 1, n_state),
                         h0_im.reshape(bsz, 1, n_state), wt)
    x_new = _s5_out(y.reshape(bsz * t, d), u, z, x2, wt, ln_g, ln_b)
    st = lambda h: h.reshape(bsz, S5_GROUPS, S5_STATE)
    return x_new.reshape(bsz, t, d), st(hr), st(hi)


def _head_sum(x, seg, seg_t):
    return _dot3(_dot3(x, seg), seg_t)


def _softplus(x):
    return jnp.maximum(x, 0.0) + jnp.log(1.0 + jnp.exp(-jnp.abs(x)))


def _rwkv_proj_kernel(x_ref, xp_ref, mu_ref, wr_ref, wwl_ref, wk_ref, wv_ref, wal_ref, wz_ref,
                      w0_ref, w2_ref, a0_ref, a2_ref, kk_ref, ka_ref, seg_ref, segt_ref,
                      r_ref, lw_ref, k_ref, v_ref, kkn_ref, a_ref, z_ref):
    x = x_ref[...]
    dx = xp_ref[...] - x
    mix = lambda m: (x + dx * mu_ref[m:m + 1, :]).astype(BF16)
    r_ref[...] = _dot(mix(0), wr_ref[...])
    w_lo = _dot(mix(1), wwl_ref[...])
    k = _dot(mix(2), wk_ref[...])
    v_ref[...] = _dot(mix(3), wv_ref[...])
    a_lo = _dot(mix(4), wal_ref[...])
    z_ref[...] = _dot(mix(5), wz_ref[...])
    w_log = -_softplus(-(w0_ref[...] + _dot(jnp.tanh(w_lo).astype(BF16), w2_ref[...]))) - 0.5
    lw_ref[...] = -jnp.exp(w_log)
    a = jax.nn.sigmoid(a0_ref[...] + _dot(a_lo.astype(BF16), a2_ref[...]))
    a_ref[...] = a
    kk = k * kk_ref[...]
    norm = jnp.sqrt(_head_sum(kk * kk, seg_ref[...], segt_ref[...]))
    kkn_ref[...] = kk / jnp.maximum(norm, 1e-12)
    k_ref[...] = k * (1.0 + (a - 1.0) * ka_ref[...])


def _head_seg():
    seg = np.zeros((D_MODEL, 128), np.float32)
    seg[np.arange(D_MODEL), np.arange(D_MODEL) // C_HEAD_DIM] = 1.0
    return jnp.asarray(seg, BF16), jnp.asarray(seg.T.copy(), BF16)


def _rwkv_weights(mu, w_in, w0, w2, a0, a2, k_k, k_a, r_k, lnx_g, lnx_b, w_out):
    d = D_MODEL
    lora_w, lora_a = w2.shape[0], a2.shape[0]
    cuts = [int(c) for c in np.cumsum([d, lora_w, d, d, lora_a])]
    pieces = jnp.split(w_in.astype(BF16), cuts, axis=-1)
    seg, seg_t = _head_seg()
    row = lambda a: a.reshape(1, d)
    return dict(mu=mu, pieces=pieces, w0=row(w0), w2=w2.astype(BF16), a0=row(a0), a2=a2.astype(BF16),
                k_k=row(k_k), k_a=row(k_a), r_k=row(r_k), lnx_g=row(lnx_g), lnx_b=row(lnx_b),
                w_out=w_out.astype(BF16), seg=seg, seg_t=seg_t)


def _rwkv_proj(x, x_prev, wt, tm=256):
    m, d = x.shape
    tm = min(tm, m)
    ins = [x, x_prev, wt["mu"], *wt["pieces"], wt["w0"], wt["w2"], wt["a0"], wt["a2"],
           wt["k_k"], wt["k_a"], wt["seg"], wt["seg_t"]]
    in_specs = [_row_spec(tm, d), _row_spec(tm, d)] + [_full_spec(a.shape) for a in ins[2:]]
    return pl.pallas_call(
        _rwkv_proj_kernel,
        grid=(m // tm,),
        in_specs=in_specs,
        out_specs=[_row_spec(tm, d)] * 7,
        out_shape=[jax.ShapeDtypeStruct((m, d), F32)] * 7,
        compiler_params=_cparams("parallel"),
        name="rwkv_proj",
    )(*ins)


def _dot_tn(a, b):
    return lax.dot_general(a, b, (((0,), (0,)), ((), ())), preferred_element_type=F32)


def _mm(a, w, dot=_dot):
    return dot(a.astype(BF16), w)


def _wkv_chunk_kernel(r_ref, lw_ref, k_ref, v_ref, kkn_ref, a_ref, y_ref, s_out, s_ref, *, chunk):
    n = chunk
    gw = 4 * C_HEAD_DIM
    n_grp = C_HEADS // 4

    @pl.when(pl.program_id(1) == 0)
    def _():
        s_ref[...] = jnp.zeros(s_ref.shape, F32)

    lw = lw_ref[...]
    row = lax.broadcasted_iota(jnp.int32, (n, n), 0)
    col = lax.broadcasted_iota(jnp.int32, (n, n), 1)
    cum = _dot_hi(jnp.where(row >= col, 1.0, 0.0), lw)
    tot = cum[n - 1:n, :]
    kkn = kkn_ref[...]
    b_vec = kkn * a_ref[...]
    e_neg = jnp.exp(-cum)
    e_tail = jnp.exp(tot - cum)
    a_t = -kkn * jnp.exp(cum - lw)
    r_t = r_ref[...] * jnp.exp(cum)
    k = k_ref[...]
    b_t, k_t = b_vec * e_neg, k * e_neg
    b_h, k_h = b_vec * e_tail, k * e_tail
    p_tot = jnp.exp(tot)
    v = v_ref[...]

    t_idx = lax.broadcasted_iota(jnp.int32, (n, gw), 0)
    s_idx = lax.rem(lax.broadcasted_iota(jnp.int32, (n, gw), 1), C_HEAD_DIM)
    strict = jnp.where(s_idx < t_idx, 1.0, 0.0)
    incl = jnp.where(s_idx <= t_idx, 1.0, 0.0)
    eye = jnp.where(s_idx == t_idx, 1.0, 0.0)
    same_head = (lax.broadcasted_iota(jnp.int32, (gw, gw), 0) // C_HEAD_DIM
                 == lax.broadcasted_iota(jnp.int32, (gw, gw), 1) // C_HEAD_DIM)
    head_mask = jnp.where(same_head, 1.0, 0.0)
    head_mask_bf = head_mask.astype(BF16)

    def blockdiag(x):
        return jnp.concatenate([x.astype(BF16)] * 4, axis=0) * head_mask_bf

    grp = range(n_grp)
    sls = [slice(g * gw, (g + 1) * gw) for g in grp]
    ar = [jnp.concatenate([a_t[:, sl], r_t[:, sl]], axis=0) for sl in sls]
    ab = [_mm(ar[g], blockdiag(b_t[:, sls[g]]), _dot_nt) for g in grp]
    ak = [_mm(ar[g], blockdiag(k_t[:, sls[g]]), _dot_nt) for g in grp]
    s0 = [s_ref[g] for g in grp]
    xs = [_mm(ar[g], s0[g].astype(BF16), _dot_nt) for g in grp]
    v_bd = [blockdiag(v[:, sl]) for sl in sls]
    a_ab = [ab[g][:n] * strict for g in grp]
    inv = [eye + a_ab[g] for g in grp]
    pw = a_ab
    for _ in range(int(math.log2(n)) - 1):
        pw = [_mm(pw[g], blockdiag(pw[g])) for g in grp]
        inv = [inv[g] + _mm(inv[g], blockdiag(pw[g])) for g in grp]
    x = [xs[g][:n] + _mm(ak[g][:n] * strict, v_bd[g]) for g in grp]
    u = [_mm(inv[g], blockdiag(x[g])) for g in grp]
    for g in grp:
        y_ref[:, sls[g]] = (xs[g][n:] + _mm(ab[g][n:] * incl, blockdiag(u[g]))
                            + _mm(ak[g][n:] * incl, v_bd[g]))
    for g in grp:
        uv = jnp.concatenate([u[g], v[:, sls[g]]], axis=0)
        bk = jnp.concatenate([b_h[:, sls[g]], k_h[:, sls[g]]], axis=0)
        s_ref[g] = s0[g] * p_tot[:, sls[g]] + _mm(uv, bk.astype(BF16), _dot_tn) * head_mask

    @pl.when(pl.program_id(1) == pl.num_programs(1) - 1)
    def _():
        for h in range(C_HEADS):
            g, hl = divmod(h, 4)
            blk = slice(hl * C_HEAD_DIM, (hl + 1) * C_HEAD_DIM)
            s_out[h] = s_ref[g, blk, blk]


def _wkv_chunked(r, lw, k, v, kkn, a, chunk=WKV_CHUNK):
    bsz, t, d = r.shape
    spec = pl.BlockSpec((None, chunk, d), lambda b, j: (b, j, 0))
    s_spec = pl.BlockSpec((None, C_HEADS, C_HEAD_DIM, C_HEAD_DIM), lambda b, j: (b, 0, 0, 0))
    return pl.pallas_call(
        functools.partial(_wkv_chunk_kernel, chunk=chunk),
        grid=(bsz, t // chunk),
        in_specs=[spec] * 6,
        out_specs=[spec, s_spec],
        out_shape=[jax.ShapeDtypeStruct((bsz, t, d), F32),
                   jax.ShapeDtypeStruct((bsz, C_HEADS, C_HEAD_DIM, C_HEAD_DIM), F32)],
        scratch_shapes=[pltpu.VMEM((C_HEADS // 4, 4 * C_HEAD_DIM, 4 * C_HEAD_DIM), F32)],
        compiler_params=_cparams("parallel", "arbitrary"),
        name="wkv_chunked",
    )(r, lw, k, v, kkn, a)


def _wkv_step_kernel(s_ref, r_ref, lw_ref, k_ref, kkn_ref, a_ref, v_ref, y_ref, s_out):
    s0 = s_ref[...]
    kkn = kkn_ref[...]
    sa = jnp.sum(s0 * (-kkn), axis=-1, keepdims=True)
    s1 = s0 * jnp.exp(lw_ref[...]) + sa * (kkn * a_ref[...]) + v_ref[...] * k_ref[...]
    s_out[...] = s1
    y_ref[...] = jnp.sum(s1 * r_ref[...], axis=-1, keepdims=True)


def _wkv_step(s0, r, lw, k, v, kkn, a):
    bsz = s0.shape[0]
    rowv = lambda x: x.reshape(bsz, C_HEADS, 1, C_HEAD_DIM)
    colv = lambda x: x.reshape(bsz, C_HEADS, C_HEAD_DIM, 1)
    s_spec = pl.BlockSpec((None, C_HEADS, C_HEAD_DIM, C_HEAD_DIM), lambda b: (b, 0, 0, 0))
    r_spec = pl.BlockSpec((None, C_HEADS, 1, C_HEAD_DIM), lambda b: (b, 0, 0, 0))
    c_spec = pl.BlockSpec((None, C_HEADS, C_HEAD_DIM, 1), lambda b: (b, 0, 0, 0))
    y, s1 = pl.pallas_call(
        _wkv_step_kernel,
        grid=(bsz,),
        in_specs=[s_spec] + [r_spec] * 5 + [c_spec],
        out_specs=[c_spec, s_spec],
        out_shape=[jax.ShapeDtypeStruct((bsz, C_HEADS, C_HEAD_DIM, 1), F32),
                   jax.ShapeDtypeStruct(s0.shape, F32)],
        compiler_params=_cparams("parallel"),
        name="wkv_step",
    )(s0, rowv(r), rowv(lw), rowv(k), rowv(kkn), rowv(a), colv(v))
    return y.reshape(bsz, D_MODEL), s1


def _rwkv_out_kernel(y_ref, r_ref, k_ref, v_ref, z_ref, x_ref, rk_ref, lg_ref, lb_ref, seg_ref, segt_ref,
                     w_ref, g_ref, b_ref, out_ref):
    seg, seg_t = seg_ref[...], segt_ref[...]
    y = y_ref[...]
    inv_n = 1.0 / C_HEAD_DIM
    yc = y - _head_sum(y, seg, seg_t) * inv_n
    var = _head_sum(yc * yc, seg, seg_t) * inv_n
    yn = yc * lax.rsqrt(var + C_LN_EPS) * lg_ref[...] + lb_ref[...]
    v = v_ref[...]
    yy = yn + _head_sum(r_ref[...] * k_ref[...] * rk_ref[...], seg, seg_t) * v
    out = _dot((yy * _silu(z_ref[...])).astype(BF16), w_ref[...])
    out_ref[...] = _deepnorm(x_ref[...], out, g_ref[...], b_ref[...])


def _rwkv_out(y, r, k, v, z, x, wt, g, b, tm=256):
    m, d = x.shape
    tm = min(tm, m)
    vec = _full_spec((1, d))
    return pl.pallas_call(
        _rwkv_out_kernel,
        grid=(m // tm,),
        in_specs=[_row_spec(tm, d)] * 6 + [vec, vec, vec, _full_spec(wt["seg"].shape),
                                           _full_spec(wt["seg_t"].shape), _full_spec((d, d)), vec, vec],
        out_specs=_row_spec(tm, d),
        out_shape=jax.ShapeDtypeStruct((m, d), F32),
        compiler_params=_cparams("parallel"),
        name="rwkv_out",
    )(y, r, k, v, z, x, wt["r_k"], wt["lnx_g"], wt["lnx_b"], wt["seg"], wt["seg_t"], wt["w_out"],
      g.reshape(1, d), b.reshape(1, d))


def _rwkv_layer(x, s0, x_last, wt, ln_g, ln_b):
    bsz, t, d = x.shape
    x_prev = jnp.concatenate([x_last[:, None, :], x[:, :-1]], axis=1)
    x2 = x.reshape(bsz * t, d)
    r, lw, k, v, kkn, a, z = _rwkv_proj(x2, x_prev.reshape(bsz * t, d), wt)
    if s0 is None:
        b3 = lambda u: u.reshape(bsz, t, d)
        y, s1 = _wkv_chunked(b3(r), b3(lw), b3(k), b3(v), b3(kkn), b3(a))
        y = y.reshape(bsz * t, d)
    else:
        y, s1 = _wkv_step(s0, r, lw, k, v, kkn, a)
    x_new = _rwkv_out(y, r, k, v, z, x2, wt, ln_g, ln_b)
    return x_new.reshape(bsz, t, d), s1, x[:, -1]


def _run_trunk(x, ln_g, ln_b, a_wts, b_wts, c_wts, past=None):
    bsz = x.shape[0]
    new = {k: [] for k in ("cmp", "sel", "win", "re", "im", "wkv", "shift")}
    for i in range(DEPTH):
        j, kind = divmod(i, N_MIXERS)
        if kind == 0:
            if past is None:
                x, kc, ks, kw = _nsa_layer_prompt(x, a_wts[j], ln_g[i], ln_b[i])
                kw = kw[:, kw.shape[1] - min(WINDOW, kw.shape[1]):]
            else:
                x, kc, ks, kw = _nsa_layer_sample(x, a_wts[j], ln_g[i], ln_b[i], j, past["cmp"],
                                                  past["sel"], past["win"], past["page_table"])
            new["cmp"].append(kc)
            new["sel"].append(ks)
            new["win"].append(kw)
        elif kind == 1:
            if past is None:
                h0r = h0i = jnp.zeros((bsz, S5_GROUPS, S5_STATE), F32)
            else:
                h0r, h0i = past["s5_re"][j], past["s5_im"][j]
            x, hr, hi = _s5_layer(x, h0r, h0i, b_wts[j], ln_g[i], ln_b[i])
            new["re"].append(hr)
            new["im"].append(hi)
        else:
            if past is None:
                s0, x_last = None, jnp.zeros((bsz, D_MODEL), F32)
            else:
                s0, x_last = past["wkv"][j], past["shift"][j]
            x, s1, xl = _rwkv_layer(x, s0, x_last, c_wts[j], ln_g[i], ln_b[i])
            new["wkv"].append(s1)
            new["shift"].append(xl)
    kv_tail = (2, A_KV_HEADS, A_HEAD_DIM)
    kv = lambda rows: jnp.stack(rows).reshape((len(rows),) + rows[0].shape[:2] + kv_tail)
    return (x, kv(new["cmp"]), kv(new["sel"]), kv(new["win"]), jnp.stack(new["re"]), jnp.stack(new["im"]),
            jnp.stack(new["wkv"]), jnp.stack(new["shift"]))


def kernel(x_prompt, x_sample, cache_cmp_kv, cache_sel_kv, cache_win_kv, state_s5_re, state_s5_im,
           state_wkv, state_shift, page_table, ln_g, ln_b,
           a_w_in, a_cmp_pe, a_cmp_w1, a_cmp_w2, a_w_out,
           b_w_in, b_log_dt, b_a_re, b_a_im, b_b_re, b_b_im, b_c_re, b_c_im, b_d, b_w_glu, b_b_glu, b_w_out,
           c_mu, c_w_in, c_w0, c_w2, c_a0, c_a2, c_k_k, c_k_a, c_r_k, c_lnx_g, c_lnx_b, c_w_out):
    a_par = (a_w_in, a_cmp_pe, a_cmp_w1, a_cmp_w2, a_w_out)
    b_par = (b_w_in, b_log_dt, b_a_re, b_a_im, b_b_re, b_b_im, b_c_re, b_c_im, b_d, b_w_glu, b_b_glu, b_w_out)
    c_par = (c_mu, c_w_in, c_w0, c_w2, c_a0, c_a2, c_k_k, c_k_a, c_r_k, c_lnx_g, c_lnx_b, c_w_out)
    a_wts = [_nsa_weights(*[p[j] for p in a_par]) for j in range(a_w_in.shape[0])]
    b_wts = [_s5_weights(*[p[j] for p in b_par]) for j in range(b_w_in.shape[0])]
    c_wts = [_rwkv_weights(*[p[j] for p in c_par]) for j in range(c_w_in.shape[0])]

    (y_p, p_cmp, p_sel, p_win, p_re, p_im, p_wkv, p_shift) = _run_trunk(
        x_prompt, ln_g, ln_b, a_wts, b_wts, c_wts)
    past = dict(cmp=cache_cmp_kv, sel=cache_sel_kv, win=cache_win_kv, page_table=page_table,
                s5_re=state_s5_re, s5_im=state_s5_im, wkv=state_wkv, shift=state_shift)
    (y_s, s_cmp, s_sel, s_win, s_re, s_im, s_wkv, s_shift) = _run_trunk(
        x_sample, ln_g, ln_b, a_wts, b_wts, c_wts, past)
    return (y_p, y_s, p_cmp, s_cmp, p_sel, s_sel, p_win, s_win,
            p_re, s_re, p_im, s_im, p_wkv, s_wkv, p_shift, s_shift)
```

```python
import functools
import math

import numpy as np
import jax
import jax.numpy as jnp
from jax import lax
from jax.experimental import pallas as pl
from jax.experimental.pallas import tpu as pltpu

F32 = jnp.float32
BF16 = jnp.bfloat16
HIGHEST = lax.Precision.HIGHEST

D_MODEL = 1024
DEPTH = 4
N_MIXERS = 3
PAGE_SIZE = 128
PAGE_GROUP = 16

A_HEADS = 16
A_KV_HEADS = 4
A_HEAD_DIM = 64
A_GROUP = 4
CMP_BLOCK = 32
CMP_STRIDE = 16
CMP_RATIO = 2
CMP_HIDDEN = 128
SEL_BLOCK = 64
SEL_RATIO = 4
SEL_TOPK = 16
WINDOW = 512
A_KV_WIDTH = 2 * A_KV_HEADS * A_HEAD_DIM

S5_GROUP_CH = 16
S5_GROUPS = 64
S5_STATE = 64
S5_SETS = 4
SCAN_TILE = 8
SCAN_LANES = 512

C_HEAD_DIM = 64
C_HEADS = 16
C_LN_EPS = 64e-5
WKV_CHUNK = 64

LN_EPS = 1e-5
DEEPNORM_ALPHA = (2 * DEPTH) ** 0.25

NEG_BIG = -1e30
LOG2E = 1.4426950408889634
POS_FEATURES = 6
MAX_POS = 64 * 256
VMEM_LIMIT = 56 * 1024 * 1024


def _cparams(*sem):
    return pltpu.CompilerParams(dimension_semantics=sem, vmem_limit_bytes=VMEM_LIMIT)


def _dot(a, b):
    return jnp.dot(a, b, preferred_element_type=F32)


def _dot_nt(a, b):
    return lax.dot_general(a, b, (((1,), (1,)), ((), ())), preferred_element_type=F32)


def _dot_hi(a, b):
    return jnp.dot(a, b, preferred_element_type=F32, precision=HIGHEST)


def _dot_nt_hi(a, b):
    return lax.dot_general(a, b, (((1,), (1,)), ((), ())), preferred_element_type=F32, precision=HIGHEST)


def _split3(x):
    hi = x.astype(BF16)
    r1 = x - hi.astype(F32)
    mid = r1.astype(BF16)
    lo = (r1 - mid.astype(F32)).astype(BF16)
    return hi, mid, lo


def _dot3(x, m01):
    hi, mid, lo = _split3(x)
    return _dot(hi, m01) + _dot(mid, m01) + _dot(lo, m01)


def _silu(z):
    return z * jax.nn.sigmoid(z)


def _proj_kernel(x_ref, *refs, n_out):
    xb = x_ref[...].astype(BF16)
    for w_ref, o_ref in zip(refs[:n_out], refs[n_out:]):
        o_ref[...] = _dot(xb, w_ref[...])


def _multi_proj(x, ws, tm=256):
    m, k = x.shape
    tm = min(tm, m)
    n_out = len(ws)
    return pl.pallas_call(
        functools.partial(_proj_kernel, n_out=n_out),
        grid=(m // tm,),
        in_specs=[pl.BlockSpec((tm, k), lambda i: (i, 0))]
        + [pl.BlockSpec(w.shape, lambda i: (0, 0)) for w in ws],
        out_specs=[pl.BlockSpec((tm, w.shape[1]), lambda i: (i, 0)) for w in ws],
        out_shape=[jax.ShapeDtypeStruct((m, w.shape[1]), F32) for w in ws],
        compiler_params=_cparams("parallel"),
        name="multi_proj",
    )(x, *ws)


def _deepnorm(x, y, g, b):
    v = DEEPNORM_ALPHA * x + y
    mu = jnp.mean(v, axis=-1, keepdims=True)
    var = jnp.mean(jnp.square(v - mu), axis=-1, keepdims=True)
    return (v - mu) * lax.rsqrt(var + LN_EPS) * g + b


def _out_ln_kernel(o_ref, z_ref, x_ref, w_ref, g_ref, b_ref, out_ref):
    gated = (o_ref[...] * _silu(z_ref[...])).astype(BF16)
    y = _dot(gated, w_ref[...])
    out_ref[...] = _deepnorm(x_ref[...], y, g_ref[...], b_ref[...])


def _row_spec(tm, n):
    return pl.BlockSpec((tm, n), lambda i: (i, 0))


def _full_spec(shape):
    return pl.BlockSpec(shape, lambda i: (0,) * len(shape))


def _out_ln(o, z, x, w_out, g, b, tm=256):
    m, d = x.shape
    tm = min(tm, m)
    return pl.pallas_call(
        _out_ln_kernel,
        grid=(m // tm,),
        in_specs=[_row_spec(tm, d), _row_spec(tm, d), _row_spec(tm, d),
                  _full_spec(w_out.shape), _full_spec((1, d)), _full_spec((1, d))],
        out_specs=_row_spec(tm, d),
        out_shape=jax.ShapeDtypeStruct((m, d), F32),
        compiler_params=_cparams("parallel"),
        name="out_ln",
    )(o, z, x, w_out, g.reshape(1, d), b.reshape(1, d))


def _pe_term_kernel(pe_ref, w1_ref, o_ref):
    for c in range(2):
        o_ref[c] = _dot_hi(pe_ref[c], w1_ref[c])


def _compress_weights(pe, w1, w2):
    pe_term = pl.pallas_call(
        _pe_term_kernel, out_shape=jax.ShapeDtypeStruct((2, 1, CMP_HIDDEN), F32), name="cmp_pe_term",
    )(pe.reshape(2, 1, CMP_BLOCK * A_HEAD_DIM), w1)
    w1r = w1.astype(BF16).reshape(2, CMP_RATIO, CMP_STRIDE // 2, 2, A_HEAD_DIM, CMP_HIDDEN)
    w_big = w1r.transpose(0, 2, 3, 4, 1, 5).reshape(2, CMP_STRIDE // 2, 2 * A_HEAD_DIM, CMP_RATIO * CMP_HIDDEN)
    return dict(pe_term=pe_term, w_big=w_big, w2=w2.astype(BF16))


def _chunk_proj(load_tap, wbig_ref):
    acc = [[None] * A_KV_HEADS for _ in range(2)]
    for c in range(2):
        for pair in range(2):
            for tp in range(CMP_STRIDE // 2):
                tap0, tap1 = load_tap(c, pair, 2 * tp), load_tap(c, pair, 2 * tp + 1)
                for half in range(2):
                    g = 2 * pair + half
                    hs = slice(half * A_HEAD_DIM, (half + 1) * A_HEAD_DIM)
                    xs = jnp.concatenate([tap0[:, hs], tap1[:, hs]], axis=1)
                    part = _dot(xs.astype(BF16), wbig_ref[c, tp])
                    acc[c][g] = part if acc[c][g] is None else acc[c][g] + part
    return acc


def _compress_finish(acc, pe_ref, w2_ref, k_ref, v_ref, n_chunk):
    for c, o_ref in ((0, k_ref), (1, v_ref)):
        for g in range(A_KV_HEADS):
            r0 = acc[c][g][:, :CMP_HIDDEN]
            r1 = acc[c][g][:, CMP_HIDDEN:]
            h = r0 + pltpu.roll(r1, n_chunk - 1, 0) + pe_ref[c]
            o_ref[g] = _dot(_silu(h).astype(BF16), w2_ref[c])


def _compress_kernel(x_ref, pe_ref, wbig_ref, w2_ref, k_ref, v_ref, *, n_chunk):
    row_stride = CMP_STRIDE * 4
    load_tap = lambda c, pair, s: x_ref[pl.ds(s * 4 + c * 2 + pair, n_chunk, stride=row_stride), :]
    acc = _chunk_proj(load_tap, wbig_ref)
    _compress_finish(acc, pe_ref, w2_ref, k_ref, v_ref, n_chunk)


def _compress_prompt(kv_c, cw):
    bsz, t, _ = kv_c.shape
    n_chunk = t // CMP_STRIDE
    out = jax.ShapeDtypeStruct((bsz, A_KV_HEADS, n_chunk, A_HEAD_DIM), F32)
    out_spec = pl.BlockSpec((None, A_KV_HEADS, n_chunk, A_HEAD_DIM), lambda b: (b, 0, 0, 0))
    return pl.pallas_call(
        functools.partial(_compress_kernel, n_chunk=n_chunk),
        grid=(bsz,),
        in_specs=[pl.BlockSpec((None, t * 4, 128), lambda b: (b, 0, 0)), _full_spec(cw["pe_term"].shape),
                  _full_spec(cw["w_big"].shape), _full_spec(cw["w2"].shape)],
        out_specs=[out_spec, out_spec],
        out_shape=[out, out],
        compiler_params=_cparams("parallel"),
        name="compress_prompt",
    )(kv_c.reshape(bsz, t * 4, 128), cw["pe_term"], cw["w_big"], cw["w2"])


def _softmax_rows(s, valid):
    s = jnp.where(valid, s, NEG_BIG)
    m = jnp.max(s, axis=-1, keepdims=True)
    e = jnp.where(valid, jnp.exp(s - m), 0.0)
    return e / jnp.maximum(jnp.sum(e, axis=-1, keepdims=True), 1e-30)


def _topk_mask(score_t, n_keep):
    n_blk = score_t.shape[0]
    blk = lax.broadcasted_iota(jnp.int32, score_t.shape, 0)
    rank = jnp.zeros(score_t.shape, F32)
    for i in range(n_blk):
        row = score_t[i:i + 1, :]
        tie_ahead = jnp.where(blk > i, 1.0, 0.0)
        rank = rank + jnp.where(row > score_t, 1.0, jnp.where(row == score_t, tie_ahead, 0.0))
    return jnp.where(rank < n_keep, 1.0, 0.0)


def _tile_heads(x):
    return jnp.concatenate([x] * A_GROUP, axis=0)


def _nsa_prompt_kernel(q_ref, gate_ref, sf_ref, kc_ref, vc_ref, ks_ref, vs_ref, kw_ref, vw_ref,
                       sel_map_ref, expand_ref, o_ref, *, qb, i_base):
    t0 = (i_base + pl.program_id(2)) * qb
    n_keys = ks_ref.shape[0]
    n_cmp_pad = kc_ref.shape[0]
    n_sel = n_keys // SEL_BLOCK

    q = q_ref[...] * (A_HEAD_DIM ** -0.5 * LOG2E)
    qg = jnp.concatenate([q[:, r * 64:(r + 1) * 64] for r in range(A_GROUP)], axis=0)
    qa = jnp.concatenate([qg, sf_ref[...]], axis=1).astype(BF16)
    t_q = t0 + lax.broadcasted_iota(jnp.int32, (qb, 1), 0)

    n_idx = lax.broadcasted_iota(jnp.int32, (qb, n_cmp_pad), 1)
    ok_c = n_idx * CMP_STRIDE + (CMP_BLOCK - 1) <= t_q
    s_c = _dot_nt(qa, kc_ref[...]) + _tile_heads(jnp.where(ok_c, 0.0, NEG_BIG))
    m_c = jnp.max(s_c, axis=-1, keepdims=True)
    e_c = jnp.where(s_c > 0.5 * NEG_BIG, jnp.exp2(s_c - m_c), 0.0)
    p_c = e_c / jnp.maximum(jnp.sum(e_c, axis=-1, keepdims=True), 1e-30)
    o_c = _dot(p_c.astype(BF16), vc_ref[...])

    p_grp = p_c[0:qb] + p_c[qb:2 * qb] + p_c[2 * qb:3 * qb] + p_c[3 * qb:4 * qb]
    sel_map = sel_map_ref[...]
    hi, mid, lo = _split3(p_grp)
    p_slc_t = _dot_nt(sel_map, hi) + _dot_nt(sel_map, mid) + _dot_nt(sel_map, lo)
    blk = lax.broadcasted_iota(jnp.int32, (n_sel, qb), 0)
    cur = (t0 + lax.broadcasted_iota(jnp.int32, (n_sel, qb), 1)) // SEL_BLOCK
    forced = (blk == 0) | (blk == cur) | (blk == cur - 1)
    score_t = jnp.where(forced, 1e30, jnp.where(blk <= cur, p_slc_t, -1.0))
    keep_t = jnp.where(blk <= cur, _topk_mask(score_t, SEL_TOPK), 0.0)
    keep = keep_t.T.astype(BF16)

    k_pos = lax.broadcasted_iota(jnp.int32, (qb, n_keys), 1)
    bias = jnp.where(k_pos > t_q, NEG_BIG, (_dot(keep, expand_ref[...]) - 1.0) * (-NEG_BIG))
    s_s = _dot_nt(qa, ks_ref[...]) + _tile_heads(bias)
    p_s = jnp.exp2(s_s - jnp.max(s_s, axis=-1, keepdims=True))
    o_s = _dot(p_s.astype(BF16), vs_ref[...]) / jnp.sum(p_s, axis=-1, keepdims=True)

    n_win = min(WINDOW + qb, n_keys)
    w0 = pl.multiple_of(jnp.maximum(t0 + qb - n_win, 0), 8)
    w_pos = w0 + lax.broadcasted_iota(jnp.int32, (qb, n_win), 1)
    ok_w = (w_pos <= t_q) & (t_q - w_pos <= WINDOW)
    s_w = _dot_nt(qa, kw_ref[pl.ds(w0, n_win), :]) + _tile_heads(jnp.where(ok_w, 0.0, NEG_BIG))
    p_w = jnp.exp2(s_w - jnp.max(s_w, axis=-1, keepdims=True))
    o_w = _dot(p_w.astype(BF16), vw_ref[pl.ds(w0, n_win), :]) / jnp.sum(p_w, axis=-1, keepdims=True)

    gate = jax.nn.sigmoid(gate_ref[...])
    for r in range(A_GROUP):
        rs = slice(r * qb, (r + 1) * qb)
        o_ref[:, r * 64:(r + 1) * 64] = (gate[:, r:r + 1] * o_c[rs]
                                         + gate[:, A_GROUP + r:A_GROUP + r + 1] * o_s[rs]
                                         + gate[:, 2 * A_GROUP + r:2 * A_GROUP + r + 1] * o_w[rs])


def _head_major(kv, dtype):
    bsz, t, _ = kv.shape
    kv = kv.reshape(bsz, t, 2, A_KV_HEADS, A_HEAD_DIM).transpose(2, 0, 3, 1, 4).astype(dtype)
    return kv[0], kv[1]


def _group_gates(gate):
    bsz, t, _ = gate.shape
    return gate.reshape(bsz, t, 3, A_KV_HEADS, A_GROUP).transpose(0, 3, 1, 2, 4).reshape(bsz, A_KV_HEADS, t, 12)


def _sel_map(n_sel, n_cmp_pad, n_cmp):
    j = np.arange(n_sel)[:, None]
    n = np.arange(n_cmp_pad)[None, :]
    m = (n >= SEL_RATIO * j - (CMP_RATIO - 1)) & (n <= SEL_RATIO * j + SEL_RATIO - 1) & (n < n_cmp)
    return jnp.asarray(m, BF16)


def _slope_features(slopes, qb):
    s = slopes * LOG2E
    s1 = s.astype(BF16).astype(F32)
    s2 = (s - s1).astype(BF16).astype(F32)
    s3 = (s - s1 - s2).astype(BF16).astype(F32)
    feat = jnp.stack([64.0 * s1, s1, 64.0 * s2, s2, 64.0 * s3, s3], axis=-1)
    feat = jnp.pad(feat, ((0, 0), (0, A_HEAD_DIM - POS_FEATURES)))
    return jnp.repeat(feat.reshape(A_KV_HEADS, A_GROUP, A_HEAD_DIM), qb, axis=1)


def _with_pos(keys, pos):
    assert int(pos.max()) < MAX_POS
    a, b = pos // 64, pos % 64
    feat = np.zeros((pos.shape[0], A_HEAD_DIM), np.float32)
    feat[:, 0:POS_FEATURES:2] = a[:, None]
    feat[:, 1:POS_FEATURES:2] = b[:, None]
    feat = jnp.broadcast_to(jnp.asarray(feat, BF16), keys.shape)
    return jnp.concatenate([keys.astype(BF16), feat], axis=-1)


def _nsa_prompt_attn(q, gate, kv_s, kv_w, k_cmp, v_cmp, slopes, qb=256, key_step=512):
    bsz, t, _ = q.shape
    key_step = min(key_step, t)
    pos = np.arange(t)
    ks, vs = _head_major(kv_s, BF16)
    kw, vw = _head_major(kv_w, BF16)
    ks, kw = _with_pos(ks, pos), _with_pos(kw, pos)
    k_cmp = _with_pos(k_cmp, np.arange(k_cmp.shape[2]) * CMP_STRIDE + (CMP_BLOCK - 1))
    v_cmp = v_cmp.astype(BF16)
    gates = _group_gates(gate)
    slope_feat = _slope_features(slopes, qb)
    rows = A_GROUP * qb
    steps = key_step // qb
    outs = []
    for seg in range(t // key_step):
        n_keys = (seg + 1) * key_step
        n_cmp_pad = n_keys // CMP_STRIDE
        n_sel = n_keys // SEL_BLOCK
        expand = jnp.asarray(np.arange(n_keys)[None, :] // SEL_BLOCK == np.arange(n_sel)[:, None], BF16)
        i_base = seg * steps
        head = lambda rows_, width: pl.BlockSpec((None, None, rows_, width), lambda b, g, i: (b, g, 0, 0))
        outs.append(pl.pallas_call(
            functools.partial(_nsa_prompt_kernel, qb=qb, i_base=i_base),
            grid=(bsz, A_KV_HEADS, steps),
            in_specs=[pl.BlockSpec((None, qb, 256), lambda b, g, i, i_base=i_base: (b, i_base + i, g)),
                      pl.BlockSpec((None, None, qb, 12), lambda b, g, i, i_base=i_base: (b, g, i_base + i, 0)),
                      pl.BlockSpec((None, rows, A_HEAD_DIM), lambda b, g, i: (g, 0, 0)),
                      head(n_cmp_pad, 2 * A_HEAD_DIM), head(n_cmp_pad, A_HEAD_DIM),
                      head(n_keys, 2 * A_HEAD_DIM), head(n_keys, A_HEAD_DIM),
                      head(n_keys, 2 * A_HEAD_DIM), head(n_keys, A_HEAD_DIM),
                      pl.BlockSpec((n_sel, n_cmp_pad), lambda b, g, i: (0, 0)),
                      pl.BlockSpec((n_sel, n_keys), lambda b, g, i: (0, 0))],
            out_specs=pl.BlockSpec((None, qb, 256), lambda b, g, i: (b, i, g)),
            out_shape=jax.ShapeDtypeStruct((bsz, key_step, D_MODEL), F32),
            compiler_params=_cparams("parallel", "parallel", "arbitrary"),
            name=f"nsa_prompt_attn_{n_keys}",
        )(q, gates, slope_feat, k_cmp, v_cmp, ks, vs, kw, vw,
          _sel_map(n_sel, n_cmp_pad, t // CMP_STRIDE - 1), expand))
    return jnp.concatenate(outs, axis=1)


def _alibi_slopes():
    return 2.0 ** (-8.0 * jnp.arange(1, A_HEADS + 1, dtype=F32) / A_HEADS)


def _nsa_weights(w_in, pe, w1, w2, w_out):
    cuts = [int(c) for c in np.cumsum([D_MODEL, A_KV_WIDTH, A_KV_WIDTH, A_KV_WIDTH, 3 * A_HEADS])]
    pieces = jnp.split(w_in.astype(BF16), cuts, axis=-1)
    return dict(w_pieces=pieces, cmp=_compress_weights(pe, w1, w2), w_out=w_out.astype(BF16))


def _nsa_layer_prompt(x, wt, ln_g, ln_b):
    bsz, t, d = x.shape
    x2 = x.reshape(bsz * t, d)
    q, kv_c, kv_s, kv_w, gate, z = _multi_proj(x2, wt["w_pieces"])
    kv_c3, kv_s3, kv_w3 = (a.reshape(bsz, t, A_KV_WIDTH) for a in (kv_c, kv_s, kv_w))
    k_cmp, v_cmp = _compress_prompt(kv_c3, wt["cmp"])
    o = _nsa_prompt_attn(q.reshape(bsz, t, d), gate.reshape(bsz, t, 3 * A_HEADS), kv_s3, kv_w3,
                         k_cmp, v_cmp, _alibi_slopes())
    x_new = _out_ln(o.reshape(bsz * t, d), z, x2, wt["w_out"], ln_g, ln_b)
    return x_new.reshape(bsz, t, d), kv_c3, kv_s3, kv_w3


def _token_minor(cache):
    nd = cache.ndim
    return cache.transpose(*range(nd - 4), nd - 3, nd - 2, nd - 1, nd - 4)


def _page_group_copies(cache_ref, layer, pt_ref, q, buf_ref, sem_ref):
    slot = lax.rem(q, 2)
    return [pltpu.make_async_copy(cache_ref.at[layer, pt_ref[q * PAGE_GROUP + i]], buf_ref.at[slot, i],
                                  sem_ref.at[slot]) for i in range(PAGE_GROUP)]


def _sample_compress_kernel(pt_ref, cache_ref, pe_ref, wbig_ref, w2_ref, k_ref, v_ref, pg_ref, xt_ref, sem,
                            *, layer, n_pages):
    b = pl.program_id(0)
    n_grp = n_pages // PAGE_GROUP
    n_total = pl.num_programs(0) * n_grp

    @pl.when(b == 0)
    def _():
        for cp in _page_group_copies(cache_ref, layer, pt_ref, 0, pg_ref, sem):
            cp.start()

    for gi in range(n_grp):
        q = b * n_grp + gi

        @pl.when(q + 1 < n_total)
        def _():
            for cp in _page_group_copies(cache_ref, layer, pt_ref, q + 1, pg_ref, sem):
                cp.start()

        for cp in _page_group_copies(cache_ref, layer, pt_ref, q, pg_ref, sem):
            cp.wait()
        slot = lax.rem(q, 2)

        def page_body(i, carry):
            row0 = pl.multiple_of((gi * PAGE_GROUP + i) * PAGE_SIZE, PAGE_SIZE)
            for c in range(2):
                for pair in range(2):
                    tiles = pg_ref[slot, i, c, pl.ds(2 * pair, 2)]
                    xt_ref[c, pair, pl.ds(row0, PAGE_SIZE), :] = tiles.reshape(2 * A_HEAD_DIM, PAGE_SIZE).T
            return carry

        lax.fori_loop(0, PAGE_GROUP, page_body, 0)

    n_chunk = n_pages * PAGE_SIZE // CMP_STRIDE
    load_tap = lambda c, pair, s: xt_ref[c, pair, pl.ds(s, n_chunk, stride=CMP_STRIDE), :]
    acc = _chunk_proj(load_tap, wbig_ref)
    _compress_finish(acc, pe_ref, w2_ref, k_ref, v_ref, n_chunk)


def _sample_compress(cache, layer, page_table, cw):
    bsz, n_pages = page_table.shape
    assert n_pages % PAGE_GROUP == 0
    n_tok = n_pages * PAGE_SIZE
    n_chunk = n_tok // CMP_STRIDE
    out = jax.ShapeDtypeStruct((bsz, A_KV_HEADS, n_chunk, A_HEAD_DIM), F32)
    out_spec = pl.BlockSpec((None, A_KV_HEADS, n_chunk, A_HEAD_DIM), lambda b, pt: (b, 0, 0, 0))
    full = lambda a: pl.BlockSpec(a.shape, lambda b, pt: (0,) * a.ndim)
    grid_spec = pltpu.PrefetchScalarGridSpec(
        num_scalar_prefetch=1, grid=(bsz,),
        in_specs=[pl.BlockSpec(memory_space=pl.ANY), full(cw["pe_term"]), full(cw["w_big"]), full(cw["w2"])],
        out_specs=[out_spec, out_spec],
        scratch_shapes=[pltpu.VMEM((2, PAGE_GROUP, 2, A_KV_HEADS, A_HEAD_DIM, PAGE_SIZE), F32),
                        pltpu.VMEM((2, 2, n_tok, 2 * A_HEAD_DIM), F32), pltpu.SemaphoreType.DMA((2,))])
    return pl.pallas_call(
        functools.partial(_sample_compress_kernel, layer=layer, n_pages=n_pages),
        grid_spec=grid_spec, out_shape=[out, out],
        compiler_params=_cparams("arbitrary"),
        name="sample_compress",
    )(page_table.reshape(-1), _token_minor(cache), cw["pe_term"], cw["w_big"], cw["w2"])


def _sample_cmp_kernel(q_ref, slope_ref, kc_ref, vc_ref, map_ref, oc_ref, idx_ref, *, past, n_cand):
    n_pad = kc_ref.shape[1]
    n_cmp = n_pad - 1
    n_idx = lax.broadcasted_iota(jnp.int32, (A_GROUP, n_pad), 1)
    dist = (past - (CMP_BLOCK - 1)) - n_idx * CMP_STRIDE
    valid = (dist >= 0) & (n_idx < n_cmp)
    q = q_ref[...] * (A_HEAD_DIM ** -0.5)
    n_keep = idx_ref.shape[1]
    width = map_ref.shape[1]
    lane = lax.broadcasted_iota(jnp.int32, (1, width), 1)
    row_i = lax.broadcasted_iota(jnp.int32, (width, width), 0)
    col_j = lax.broadcasted_iota(jnp.int32, (width, width), 1)
    for g in range(A_KV_HEADS):
        hs = slice(g * A_GROUP, (g + 1) * A_GROUP)
        s = _dot_nt(q[hs].astype(BF16), kc_ref[g].astype(BF16)) - slope_ref[hs] * dist.astype(F32)
        p = _softmax_rows(s, valid)
        oc_ref[hs, :] = _dot(p.astype(BF16), vc_ref[g].astype(BF16))
        p_grp = jnp.sum(p, axis=0, keepdims=True)
        p_slc = _dot3(p_grp, map_ref[...])
        forced = (lane == 0) | (lane == n_cand - 1)
        score = jnp.where(forced, 1e30, jnp.where(lane < n_cand, p_slc, -1.0))
        s_j = jnp.broadcast_to(score, (width, width))
        s_i = s_j.T
        tie_ahead = jnp.where(row_i < col_j, 1.0, 0.0)
        ahead = jnp.where(s_i > s_j, 1.0, jnp.where(s_i == s_j, tie_ahead, 0.0))
        rank = jnp.sum(ahead, axis=0, keepdims=True)
        want = lax.broadcasted_iota(jnp.int32, (n_keep, width), 0).astype(F32)
        picked = jnp.where(rank == want, lane.astype(F32), 0.0)
        idx_ref[g] = jnp.sum(picked, axis=-1, keepdims=True).astype(jnp.int32)


def _sample_cmp_topk(q, k_cmp, v_cmp, slopes, past):
    bsz = q.shape[0]
    n_pad = k_cmp.shape[2]
    n_cand = past // SEL_BLOCK
    width = -(-n_cand // 128) * 128
    j = np.arange(width)[None, :]
    n = np.arange(n_pad)[:, None]
    sel_map = (n >= SEL_RATIO * j - (CMP_RATIO - 1)) & (n <= SEL_RATIO * j + SEL_RATIO - 1) & (n < n_pad - 1)
    cmp_spec = pl.BlockSpec((None, A_KV_HEADS, n_pad, A_HEAD_DIM), lambda b: (b, 0, 0, 0))
    return pl.pallas_call(
        functools.partial(_sample_cmp_kernel, past=past, n_cand=n_cand),
        grid=(bsz,),
        in_specs=[pl.BlockSpec((None, A_HEADS, A_HEAD_DIM), lambda b: (b, 0, 0)),
                  _full_spec((A_HEADS, 1)), cmp_spec, cmp_spec, _full_spec((n_pad, width))],
        out_specs=[pl.BlockSpec((None, A_HEADS, A_HEAD_DIM), lambda b: (b, 0, 0)),
                   pl.BlockSpec((None, A_KV_HEADS, SEL_TOPK, 1), lambda b: (b, 0, 0, 0))],
        out_shape=[jax.ShapeDtypeStruct((bsz, A_HEADS, A_HEAD_DIM), F32),
                   jax.ShapeDtypeStruct((bsz, A_KV_HEADS, SEL_TOPK, 1), jnp.int32)],
        compiler_params=_cparams("parallel"),
        name="sample_cmp_topk",
    )(q, slopes.reshape(A_HEADS, 1), k_cmp, v_cmp, jnp.asarray(sel_map, BF16))


def _sel_tile_copies(cache_ref, layer, pt_ref, idx_ref, b, g, r, n_pages, buf_ref, sem):
    blk = idx_ref[(b * A_KV_HEADS + g) * SEL_TOPK + r]
    page = pt_ref[b * n_pages + blk // (PAGE_SIZE // SEL_BLOCK)]
    return [pltpu.make_async_copy(cache_ref.at[layer, page, c, g], buf_ref.at[g, r, c], sem) for c in range(2)]


def _sample_attn_kernel(pt_ref, idx_ref, cache_ref, q_ref, gate_ref, slope_ref, oc_ref, ks_new_ref,
                        kw_new_ref, win_ref, o_ref, buf_ref, sem, *, layer, n_pages, past):
    b = pl.program_id(0)
    n_blk = SEL_TOPK - 1
    for g in range(A_KV_HEADS):
        for r in range(n_blk):
            for cp in _sel_tile_copies(cache_ref, layer, pt_ref, idx_ref, b, g, r, n_pages, buf_ref, sem):
                cp.start()

    q = q_ref[...] * (A_HEAD_DIM ** -0.5)
    gate = jax.nn.sigmoid(gate_ref[...])
    n_buf = win_ref.shape[-1]
    tok = lax.broadcasted_iota(jnp.int32, (1, PAGE_SIZE), 1)
    dist_w = (n_buf - lax.broadcasted_iota(jnp.int32, (1, n_buf), 1)).astype(F32)

    def attend(qg, slope, keys_t, vals_t, dist, valid, k_new, v_new):
        s = _dot(qg.astype(BF16), keys_t.astype(BF16)) - slope * dist
        if valid is not None:
            s = jnp.where(valid, s, NEG_BIG)
        s_new = jnp.sum(qg * k_new, axis=-1, keepdims=True)
        m = jnp.maximum(jnp.max(s, axis=-1, keepdims=True), s_new)
        e = jnp.exp(s - m)
        e_new = jnp.exp(s_new - m)
        l = jnp.sum(e, axis=-1, keepdims=True) + e_new
        return (_dot_nt(e.astype(BF16), vals_t.astype(BF16)) + e_new * v_new) / l

    def new_row(ref, g):
        ksl = slice(g * A_HEAD_DIM, (g + 1) * A_HEAD_DIM)
        vsl = slice(A_KV_WIDTH // 2 + g * A_HEAD_DIM, A_KV_WIDTH // 2 + (g + 1) * A_HEAD_DIM)
        return ref[:, ksl], ref[:, vsl]

    out_w = []
    for g in range(A_KV_HEADS):
        hs = slice(g * A_GROUP, (g + 1) * A_GROUP)
        out_w.append(attend(q[hs], slope_ref[hs], win_ref[0, g], win_ref[1, g], dist_w, None,
                            *new_row(kw_new_ref, g)))

    for g in range(A_KV_HEADS):
        for r in range(n_blk):
            for cp in _sel_tile_copies(cache_ref, layer, pt_ref, idx_ref, b, g, r, n_pages, buf_ref, sem):
                cp.wait()

    blocks_per_page = PAGE_SIZE // SEL_BLOCK
    for g in range(A_KV_HEADS):
        hs = slice(g * A_GROUP, (g + 1) * A_GROUP)
        keys_t = jnp.concatenate([buf_ref[g, r, 0] for r in range(n_blk)], axis=1)
        vals_t = jnp.concatenate([buf_ref[g, r, 1] for r in range(n_blk)], axis=1)
        dist, valid = [], []
        for r in range(n_blk):
            blk = idx_ref[(b * A_KV_HEADS + g) * SEL_TOPK + r]
            dist.append(past - ((blk // blocks_per_page) * PAGE_SIZE + tok))
            valid.append(tok // SEL_BLOCK == blk % blocks_per_page)
        dist = jnp.concatenate(dist, axis=1).astype(F32)
        valid = jnp.concatenate(valid, axis=1)
        o_s = attend(q[hs], slope_ref[hs], keys_t, vals_t, dist, valid, *new_row(ks_new_ref, g))
        gt = gate[hs]
        o_ref[hs, :] = gt[:, 0:1] * oc_ref[hs, :] + gt[:, 1:2] * o_s + gt[:, 2:3] * out_w[g]


def _sample_attn(page_table, idx, cache_sel, layer, q, gate, slopes, o_c, kv_s_new, kv_w_new, cache_win, past):
    bsz, n_pages = page_table.shape
    n_buf = cache_win.shape[2]
    head_spec = pl.BlockSpec((None, A_HEADS, A_HEAD_DIM), lambda b, pt, ix: (b, 0, 0))
    new_spec = pl.BlockSpec((None, 1, A_KV_WIDTH), lambda b, pt, ix: (b, 0, 0))
    grid_spec = pltpu.PrefetchScalarGridSpec(
        num_scalar_prefetch=2, grid=(bsz,),
        in_specs=[pl.BlockSpec(memory_space=pl.ANY), head_spec,
                  pl.BlockSpec((None, A_HEADS, 3), lambda b, pt, ix: (b, 0, 0)),
                  pl.BlockSpec((A_HEADS, 1), lambda b, pt, ix: (0, 0)),
                  head_spec, new_spec, new_spec,
                  pl.BlockSpec((None, None, 2, A_KV_HEADS, A_HEAD_DIM, n_buf),
                               lambda b, pt, ix: (layer, b, 0, 0, 0, 0))],
        out_specs=head_spec,
        scratch_shapes=[pltpu.VMEM((A_KV_HEADS, SEL_TOPK - 1, 2, A_HEAD_DIM, PAGE_SIZE), F32),
                        pltpu.SemaphoreType.DMA(())])
    return pl.pallas_call(
        functools.partial(_sample_attn_kernel, layer=layer, n_pages=n_pages, past=past),
        grid_spec=grid_spec,
        out_shape=jax.ShapeDtypeStruct((bsz, A_HEADS, A_HEAD_DIM), F32),
        compiler_params=_cparams("arbitrary"),
        name="sample_attn",
    )(page_table.reshape(-1), idx.reshape(-1), _token_minor(cache_sel), q, gate, slopes.reshape(A_HEADS, 1),
      o_c, kv_s_new, kv_w_new, _token_minor(cache_win))


def _nsa_layer_sample(x, wt, ln_g, ln_b, layer, cache_cmp, cache_sel, cache_win, page_table):
    bsz, _, d = x.shape
    n_pages = page_table.shape[1]
    past = n_pages * PAGE_SIZE
    x2 = x.reshape(bsz, d)
    q, kv_c, kv_s, kv_w, gate, z = _multi_proj(x2, wt["w_pieces"])
    slopes = _alibi_slopes()
    k_cmp, v_cmp = _sample_compress(cache_cmp, layer, page_table, wt["cmp"])
    q3 = q.reshape(bsz, A_HEADS, A_HEAD_DIM)
    o_c, idx = _sample_cmp_topk(q3, k_cmp, v_cmp, slopes, past)
    gate3 = gate.reshape(bsz, 3, A_HEADS).transpose(0, 2, 1)
    kv_s3, kv_w3 = kv_s.reshape(bsz, 1, A_KV_WIDTH), kv_w.reshape(bsz, 1, A_KV_WIDTH)
    o = _sample_attn(page_table, idx, cache_sel, layer, q3, gate3, slopes, o_c, kv_s3, kv_w3, cache_win, past)
    x_new = _out_ln(o.reshape(bsz, d), z, x2, wt["w_out"], ln_g, ln_b)
    n_buf = cache_win.shape[2]
    n_keep = min(WINDOW, n_buf + 1)
    win_old = cache_win[layer, :, n_buf + 1 - n_keep:].reshape(bsz, n_keep - 1, A_KV_WIDTH)
    win_new = jnp.concatenate([win_old, kv_w3], axis=1)
    return x_new.reshape(bsz, 1, d), kv_c.reshape(bsz, 1, A_KV_WIDTH), kv_s3, win_new


def _s5_param_kernel(log_dt_ref, ar_ref, ai_ref, br_ref, bi_ref, tile_ref,
                     abr_ref, abi_ref, bbr_ref, bbi_ref):
    dt = jnp.exp(log_dt_ref[...])
    ar, ai = ar_ref[...], ai_ref[...]
    mag = jnp.exp(dt * ar)
    abr, abi = mag * jnp.cos(dt * ai), mag * jnp.sin(dt * ai)
    den = ar * ar + ai * ai
    num_re, num_im = abr - 1.0, abi
    zoh_re = (num_re * ar + num_im * ai) / den
    zoh_im = (num_im * ar - num_re * ai) / den
    abr_ref[...] = abr
    abi_ref[...] = abi
    zr = _dot3(zoh_re, tile_ref[...])
    zi = _dot3(zoh_im, tile_ref[...])
    br, bi = br_ref[...], bi_ref[...]
    bbr_ref[...] = zr * br - zi * bi
    bbi_ref[...] = zr * bi + zi * br


def _s5_weights(w_in, log_dt, a_re, a_im, b_re, b_im, c_re, c_im, d_skip, w_glu, b_glu, w_out):
    g, p, c = S5_GROUPS, S5_STATE, S5_GROUP_CH
    tile = jnp.asarray(np.tile(np.eye(p, dtype=np.float32), (1, c)), BF16)
    to_gcp = lambda b: b.transpose(0, 2, 1).reshape(g, c * p)
    shapes = [jax.ShapeDtypeStruct((g, p), F32)] * 2 + [jax.ShapeDtypeStruct((g, c * p), F32)] * 2
    abr, abi, bbr, bbi = pl.pallas_call(_s5_param_kernel, out_shape=shapes, name="s5_params")(
        log_dt.reshape(g, 1), a_re, a_im, to_gcp(b_re), to_gcp(b_im), tile)
    eye = jnp.eye(S5_GROUPS // S5_SETS, dtype=F32)
    gl = S5_GROUPS // S5_SETS

    def in_blockdiag(bb):
        bb = bb.reshape(S5_SETS, gl, c, p)
        return jnp.einsum('sgcp,gh->sgchp', bb, eye).reshape(S5_SETS, gl * c, gl * p).astype(BF16)

    def out_blockdiag(cc):
        cc = cc.reshape(S5_SETS, gl, c, p)
        return jnp.einsum('sgcp,gh->sgphc', cc, eye).reshape(S5_SETS, gl * p, gl * c).astype(BF16)

    w_u, w_z = jnp.split(w_in.astype(BF16), 2, axis=-1)
    return dict(w_pieces=[w_u, w_z], abr=abr.reshape(1, g * p), abi=abi.reshape(1, g * p),
                wb_re=in_blockdiag(bbr), wb_im=in_blockdiag(bbi),
                wc_re=out_blockdiag(c_re), wc_im=out_blockdiag(c_im),
                d_skip=d_skip.reshape(1, -1), w_glu=w_glu.astype(BF16), b_glu=b_glu.reshape(1, -1),
                w_out=w_out.astype(BF16))


def _cmul(ar, ai, br, bi):
    return ar * br - ai * bi, ar * bi + ai * br


def _s5_row_scan(bur, bui, ar_ref, ai_ref, hr_c, hi_c, tc):
    n_set = S5_GROUPS * S5_STATE // S5_SETS
    for s in range(S5_SETS):
        sl = pl.ds(s * n_set, n_set)
        ar, ai = ar_ref[:, sl], ai_ref[:, sl]

        def step(t, carry):
            hr, hi = carry
            pr, pi = _cmul(ar, ai, hr, hi)
            nr, ni = pr + bur[pl.ds(t, 1), sl], pi + bui[pl.ds(t, 1), sl]
            bur[pl.ds(t, 1), sl] = nr
            bui[pl.ds(t, 1), sl] = ni
            return nr, ni

        hr, hi = lax.fori_loop(0, tc, step, (hr_c[:, sl], hi_c[:, sl]))
        hr_c[:, sl] = hr
        hi_c[:, sl] = hi


def _s5_tile_scan(bur, bui, ar_ref, ai_ref, hr_c, hi_c, tc):
    width = SCAN_LANES
    row = lax.broadcasted_iota(jnp.int32, (SCAN_TILE, width), 0)
    for s in range(S5_GROUPS * S5_STATE // width):
        sl = pl.ds(s * width, width)
        ar, ai = ar_ref[:, sl], ai_ref[:, sl]
        pows = [(ar, ai)]
        for _ in range(SCAN_TILE - 1):
            pows.append(_cmul(*pows[-1], ar, ai))
        pw_r = jnp.concatenate([p[0] for p in pows], axis=0)
        pw_i = jnp.concatenate([p[1] for p in pows], axis=0)
        steps = []
        for k in (1, 2, 4):
            keep = row >= k
            steps.append((k, jnp.where(keep, pows[k - 1][0], 0.0), jnp.where(keep, pows[k - 1][1], 0.0)))

        def tile(i, carry):
            cr, ci = carry
            rows = pl.ds(pl.multiple_of(i * SCAN_TILE, SCAN_TILE), SCAN_TILE)
            yr, yi = bur[rows, sl], bui[rows, sl]
            for k, kr, ki in steps:
                pr, pi = _cmul(kr, ki, pltpu.roll(yr, k, 0), pltpu.roll(yi, k, 0))
                yr, yi = yr + pr, yi + pi
            pr, pi = _cmul(pw_r, pw_i, cr, ci)
            yr, yi = yr + pr, yi + pi
            bur[rows, sl] = yr
            bui[rows, sl] = yi
            return yr[SCAN_TILE - 1:, :], yi[SCAN_TILE - 1:, :]

        hr, hi = lax.fori_loop(0, tc // SCAN_TILE, tile, (hr_c[:, sl], hi_c[:, sl]), unroll=4)
        hr_c[:, sl] = hr
        hi_c[:, sl] = hi


def _s5_scan_kernel(u_ref, wbr_ref, wbi_ref, ar_ref, ai_ref, wcr_ref, wci_ref, h0r_ref, h0i_ref,
                    y_ref, hr_out, hi_out, bur, bui, hr_c, hi_c, *, tc):
    @pl.when(pl.program_id(1) == 0)
    def _():
        hr_c[...] = h0r_ref[...]
        hi_c[...] = h0i_ref[...]

    n_set = S5_GROUPS * S5_STATE // S5_SETS
    n_ch = S5_GROUPS * S5_GROUP_CH // S5_SETS
    ub = u_ref[...].astype(BF16)
    for s in range(S5_SETS):
        us = ub[:, s * n_ch:(s + 1) * n_ch]
        bur[:, s * n_set:(s + 1) * n_set] = _dot(us, wbr_ref[s])
        bui[:, s * n_set:(s + 1) * n_set] = _dot(us, wbi_ref[s])

    if tc % SCAN_TILE == 0:
        _s5_tile_scan(bur, bui, ar_ref, ai_ref, hr_c, hi_c, tc)
    else:
        _s5_row_scan(bur, bui, ar_ref, ai_ref, hr_c, hi_c, tc)

    for s in range(S5_SETS):
        h_re = bur[:, s * n_set:(s + 1) * n_set].astype(BF16)
        h_im = bui[:, s * n_set:(s + 1) * n_set].astype(BF16)
        y_ref[:, s * n_ch:(s + 1) * n_ch] = _dot(h_re, wcr_ref[s]) - _dot(h_im, wci_ref[s])
    hr_out[...] = hr_c[...]
    hi_out[...] = hi_c[...]


def _s5_scan(u, h0_re, h0_im, wt, tc=256):
    bsz, t, d = u.shape
    tc = min(tc, t)
    n_state = S5_GROUPS * S5_STATE
    st_spec = pl.BlockSpec((None, 1, n_state), lambda b, j: (b, 0, 0))
    full = lambda a: pl.BlockSpec(a.shape, lambda b, j: (0,) * a.ndim)
    st_shape = jax.ShapeDtypeStruct((bsz, 1, n_state), F32)
    return pl.pallas_call(
        functools.partial(_s5_scan_kernel, tc=tc),
        grid=(bsz, t // tc),
        in_specs=[pl.BlockSpec((None, tc, d), lambda b, j: (b, j, 0)),
                  full(wt["wb_re"]), full(wt["wb_im"]), full(wt["abr"]), full(wt["abi"]),
                  full(wt["wc_re"]), full(wt["wc_im"]), st_spec, st_spec],
        out_specs=[pl.BlockSpec((None, tc, d), lambda b, j: (b, j, 0)), st_spec, st_spec],
        out_shape=[jax.ShapeDtypeStruct((bsz, t, d), F32), st_shape, st_shape],
        scratch_shapes=[pltpu.VMEM((tc, n_state), F32), pltpu.VMEM((tc, n_state), F32),
                        pltpu.VMEM((1, n_state), F32), pltpu.VMEM((1, n_state), F32)],
        compiler_params=_cparams("parallel", "arbitrary"),
        name="s5_scan",
    )(u, wt["wb_re"], wt["wb_im"], wt["abr"], wt["abi"], wt["wc_re"], wt["wc_im"], h0_re, h0_im)


def _s5_out_kernel(y_ref, u_ref, z_ref, x_ref, d_ref, wg_ref, bg_ref, w_ref, g_ref, b_ref, out_ref):
    y = jax.nn.gelu(y_ref[...] + d_ref[...] * u_ref[...])
    y = y * jax.nn.sigmoid(_dot(y.astype(BF16), wg_ref[...]) + bg_ref[...])
    out = _dot((y * _silu(z_ref[...])).astype(BF16), w_ref[...])
    out_ref[...] = _deepnorm(x_ref[...], out, g_ref[...], b_ref[...])


def _s5_out(y, u, z, x, wt, g, b, tm=256):
    m, d = x.shape
    tm = min(tm, m)
    vec = _full_spec((1, d))
    return pl.pallas_call(
        _s5_out_kernel,
        grid=(m // tm,),
        in_specs=[_row_spec(tm, d)] * 4 + [vec, _full_spec((d, d)), vec, _full_spec((d, d)), vec, vec],
        out_specs=_row_spec(tm, d),
        out_shape=jax.ShapeDtypeStruct((m, d), F32),
        compiler_params=_cparams("parallel"),
        name="s5_out",
    )(y, u, z, x, wt["d_skip"], wt["w_glu"], wt["b_glu"], wt["w_out"], g.reshape(1, d), b.reshape(1, d))


def _s5_layer(x, h0_re, h0_im, wt, ln_g, ln_b):
    bsz, t, d = x.shape
    x2 = x.reshape(bsz * t, d)
    u, z = _multi_proj(x2, wt["w_pieces"])
    n_state = S5_GROUPS * S5_STATE
    y, hr, hi = _s5_scan(u.reshape(bsz, t, d), h0_re.reshape(bsz, 1, n_state),
                         h0_im.reshape(bsz, 1, n_state), wt)
    x_new = _s5_out(y.reshape(bsz * t, d), u, z, x2, wt, ln_g, ln_b)
    st = lambda h: h.reshape(bsz, S5_GROUPS, S5_STATE)
    return x_new.reshape(bsz, t, d), st(hr), st(hi)


def _head_sum(x, seg):
    w = seg.shape[0]
    return jnp.concatenate([_dot3(x[:, j:j + w], seg) for j in range(0, x.shape[1], w)], axis=1)


def _softplus(x):
    return jnp.maximum(x, 0.0) + jnp.log(1.0 + jnp.exp(-jnp.abs(x)))


def _rwkv_proj_kernel(x_ref, xp_ref, mu_ref, wr_ref, wwl_ref, wk_ref, wv_ref, wal_ref, wz_ref,
                      w0_ref, w2_ref, a0_ref, a2_ref, kk_ref, ka_ref, seg_ref,
                      r_ref, lw_ref, k_ref, v_ref, kkn_ref, a_ref, z_ref):
    x = x_ref[...]
    dx = xp_ref[...] - x
    mix = lambda m: (x + dx * mu_ref[m:m + 1, :]).astype(BF16)
    r_ref[...] = _dot(mix(0), wr_ref[...])
    w_lo = _dot(mix(1), wwl_ref[...])
    k = _dot(mix(2), wk_ref[...])
    v_ref[...] = _dot(mix(3), wv_ref[...])
    a_lo = _dot(mix(4), wal_ref[...])
    z_ref[...] = _dot(mix(5), wz_ref[...])
    w_log = -_softplus(-(w0_ref[...] + _dot(jnp.tanh(w_lo).astype(BF16), w2_ref[...]))) - 0.5
    lw_ref[...] = -jnp.exp(w_log)
    a = jax.nn.sigmoid(a0_ref[...] + _dot(a_lo.astype(BF16), a2_ref[...]))
    a_ref[...] = a
    kk = k * kk_ref[...]
    norm = jnp.sqrt(_head_sum(kk * kk, seg_ref[...]))
    kkn_ref[...] = kk / jnp.maximum(norm, 1e-12)
    k_ref[...] = k * (1.0 + (a - 1.0) * ka_ref[...])


def _head_seg(width=4 * C_HEAD_DIM):
    lane = np.arange(width) // C_HEAD_DIM
    return jnp.asarray(lane[:, None] == lane[None, :], BF16)


def _rwkv_weights(mu, w_in, w0, w2, a0, a2, k_k, k_a, r_k, lnx_g, lnx_b, w_out):
    d = D_MODEL
    lora_w, lora_a = w2.shape[0], a2.shape[0]
    cuts = [int(c) for c in np.cumsum([d, lora_w, d, d, lora_a])]
    pieces = jnp.split(w_in.astype(BF16), cuts, axis=-1)
    row = lambda a: a.reshape(1, d)
    return dict(mu=mu, pieces=pieces, w0=row(w0), w2=w2.astype(BF16), a0=row(a0), a2=a2.astype(BF16),
                k_k=row(k_k), k_a=row(k_a), r_k=row(r_k), lnx_g=row(lnx_g), lnx_b=row(lnx_b),
                w_out=w_out.astype(BF16), seg=_head_seg())


def _rwkv_proj(x, x_prev, wt, tm=256):
    m, d = x.shape
    tm = min(tm, m)
    ins = [x, x_prev, wt["mu"], *wt["pieces"], wt["w0"], wt["w2"], wt["a0"], wt["a2"],
           wt["k_k"], wt["k_a"], wt["seg"]]
    in_specs = [_row_spec(tm, d), _row_spec(tm, d)] + [_full_spec(a.shape) for a in ins[2:]]
    return pl.pallas_call(
        _rwkv_proj_kernel,
        grid=(m // tm,),
        in_specs=in_specs,
        out_specs=[_row_spec(tm, d)] * 7,
        out_shape=[jax.ShapeDtypeStruct((m, d), F32)] * 7,
        compiler_params=_cparams("parallel"),
        name="rwkv_proj",
    )(*ins)


def _dot_tn(a, b):
    return lax.dot_general(a, b, (((0,), (0,)), ((), ())), preferred_element_type=F32)


def _mm(a, w, dot=_dot):
    return dot(a.astype(BF16), w)


def _wkv_chunk_kernel(r_ref, lw_ref, k_ref, v_ref, kkn_ref, a_ref, y_ref, s_out, s_ref, *, chunk):
    n = chunk
    gw = 4 * C_HEAD_DIM
    n_grp = C_HEADS // 4

    @pl.when(pl.program_id(1) == 0)
    def _():
        s_ref[...] = jnp.zeros(s_ref.shape, F32)

    lw = lw_ref[...]
    row = lax.broadcasted_iota(jnp.int32, (n, n), 0)
    col = lax.broadcasted_iota(jnp.int32, (n, n), 1)
    cum = _dot_hi(jnp.where(row >= col, 1.0, 0.0), lw)
    tot = cum[n - 1:n, :]
    kkn = kkn_ref[...]
    b_vec = kkn * a_ref[...]
    e_neg = jnp.exp(-cum)
    e_tail = jnp.exp(tot - cum)
    a_t = -kkn * jnp.exp(cum - lw)
    r_t = r_ref[...] * jnp.exp(cum)
    k = k_ref[...]
    b_t, k_t = b_vec * e_neg, k * e_neg
    b_h, k_h = b_vec * e_tail, k * e_tail
    p_tot = jnp.exp(tot)
    v = v_ref[...]

    t_idx = lax.broadcasted_iota(jnp.int32, (n, gw), 0)
    s_idx = lax.rem(lax.broadcasted_iota(jnp.int32, (n, gw), 1), C_HEAD_DIM)
    strict = jnp.where(s_idx < t_idx, 1.0, 0.0)
    incl = jnp.where(s_idx <= t_idx, 1.0, 0.0)
    eye = jnp.where(s_idx == t_idx, 1.0, 0.0)
    same_head = (lax.broadcasted_iota(jnp.int32, (gw, gw), 0) // C_HEAD_DIM
                 == lax.broadcasted_iota(jnp.int32, (gw, gw), 1) // C_HEAD_DIM)
    head_mask = jnp.where(same_head, 1.0, 0.0)
    head_mask_bf = head_mask.astype(BF16)

    def blockdiag(x):
        return jnp.concatenate([x.astype(BF16)] * 4, axis=0) * head_mask_bf

    grp = range(n_grp)
    sls = [slice(g * gw, (g + 1) * gw) for g in grp]
    ar = [jnp.concatenate([a_t[:, sl], r_t[:, sl]], axis=0) for sl in sls]
    ab = [_mm(ar[g], blockdiag(b_t[:, sls[g]]), _dot_nt) for g in grp]
    ak = [_mm(ar[g], blockdiag(k_t[:, sls[g]]), _dot_nt) for g in grp]
    s0 = [s_ref[g] for g in grp]
    xs = [_mm(ar[g], s0[g].astype(BF16), _dot_nt) for g in grp]
    v_bd = [blockdiag(v[:, sl]) for sl in sls]
    a_ab = [ab[g][:n] * strict for g in grp]
    inv = [eye + a_ab[g] for g in grp]
    pw = a_ab
    for _ in range(int(math.log2(n)) - 1):
        pw = [_mm(pw[g], blockdiag(pw[g])) for g in grp]
        inv = [inv[g] + _mm(inv[g], blockdiag(pw[g])) for g in grp]
    x = [xs[g][:n] + _mm(ak[g][:n] * strict, v_bd[g]) for g in grp]
    u = [_mm(inv[g], blockdiag(x[g])) for g in grp]
    for g in grp:
        y_ref[:, sls[g]] = (xs[g][n:] + _mm(ab[g][n:] * incl, blockdiag(u[g]))
                            + _mm(ak[g][n:] * incl, v_bd[g]))
    for g in grp:
        uv = jnp.concatenate([u[g], v[:, sls[g]]], axis=0)
        bk = jnp.concatenate([b_h[:, sls[g]], k_h[:, sls[g]]], axis=0)
        s_ref[g] = s0[g] * p_tot[:, sls[g]] + _mm(uv, bk.astype(BF16), _dot_tn) * head_mask

    @pl.when(pl.program_id(1) == pl.num_programs(1) - 1)
    def _():
        for h in range(C_HEADS):
            g, hl = divmod(h, 4)
            blk = slice(hl * C_HEAD_DIM, (hl + 1) * C_HEAD_DIM)
            s_out[h] = s_ref[g, blk, blk]


def _wkv_chunked(r, lw, k, v, kkn, a, chunk=WKV_CHUNK):
    bsz, t, d = r.shape
    spec = pl.BlockSpec((None, chunk, d), lambda b, j: (b, j, 0))
    s_spec = pl.BlockSpec((None, C_HEADS, C_HEAD_DIM, C_HEAD_DIM), lambda b, j: (b, 0, 0, 0))
    return pl.pallas_call(
        functools.partial(_wkv_chunk_kernel, chunk=chunk),
        grid=(bsz, t // chunk),
        in_specs=[spec] * 6,
        out_specs=[spec, s_spec],
        out_shape=[jax.ShapeDtypeStruct((bsz, t, d), F32),
                   jax.ShapeDtypeStruct((bsz, C_HEADS, C_HEAD_DIM, C_HEAD_DIM), F32)],
        scratch_shapes=[pltpu.VMEM((C_HEADS // 4, 4 * C_HEAD_DIM, 4 * C_HEAD_DIM), F32)],
        compiler_params=_cparams("parallel", "arbitrary"),
        name="wkv_chunked",
    )(r, lw, k, v, kkn, a)


def _wkv_step_kernel(s_ref, r_ref, lw_ref, k_ref, kkn_ref, a_ref, v_ref, y_ref, s_out):
    s0 = s_ref[...]
    kkn = kkn_ref[...]
    sa = jnp.sum(s0 * (-kkn), axis=-1, keepdims=True)
    s1 = s0 * jnp.exp(lw_ref[...]) + sa * (kkn * a_ref[...]) + v_ref[...] * k_ref[...]
    s_out[...] = s1
    y_ref[...] = jnp.sum(s1 * r_ref[...], axis=-1, keepdims=True)


def _wkv_step(s0, r, lw, k, v, kkn, a):
    bsz = s0.shape[0]
    rowv = lambda x: x.reshape(bsz, C_HEADS, 1, C_HEAD_DIM)
    colv = lambda x: x.reshape(bsz, C_HEADS, C_HEAD_DIM, 1)
    s_spec = pl.BlockSpec((None, C_HEADS, C_HEAD_DIM, C_HEAD_DIM), lambda b: (b, 0, 0, 0))
    r_spec = pl.BlockSpec((None, C_HEADS, 1, C_HEAD_DIM), lambda b: (b, 0, 0, 0))
    c_spec = pl.BlockSpec((None, C_HEADS, C_HEAD_DIM, 1), lambda b: (b, 0, 0, 0))
    y, s1 = pl.pallas_call(
        _wkv_step_kernel,
        grid=(bsz,),
        in_specs=[s_spec] + [r_spec] * 5 + [c_spec],
        out_specs=[c_spec, s_spec],
        out_shape=[jax.ShapeDtypeStruct((bsz, C_HEADS, C_HEAD_DIM, 1), F32),
                   jax.ShapeDtypeStruct(s0.shape, F32)],
        compiler_params=_cparams("parallel"),
        name="wkv_step",
    )(s0, rowv(r), rowv(lw), rowv(k), rowv(kkn), rowv(a), colv(v))
    return y.reshape(bsz, D_MODEL), s1


def _rwkv_out_kernel(y_ref, r_ref, k_ref, v_ref, z_ref, x_ref, rk_ref, lg_ref, lb_ref, seg_ref,
                     w_ref, g_ref, b_ref, out_ref):
    seg = seg_ref[...]
    y = y_ref[...]
    inv_n = 1.0 / C_HEAD_DIM
    yc = y - _head_sum(y, seg) * inv_n
    var = _head_sum(yc * yc, seg) * inv_n
    yn = yc * lax.rsqrt(var + C_LN_EPS) * lg_ref[...] + lb_ref[...]
    v = v_ref[...]
    yy = yn + _head_sum(r_ref[...] * k_ref[...] * rk_ref[...], seg) * v
    out = _dot((yy * _silu(z_ref[...])).astype(BF16), w_ref[...])
    out_ref[...] = _deepnorm(x_ref[...], out, g_ref[...], b_ref[...])


def _rwkv_out(y, r, k, v, z, x, wt, g, b, tm=256):
    m, d = x.shape
    tm = min(tm, m)
    vec = _full_spec((1, d))
    return pl.pallas_call(
        _rwkv_out_kernel,
        grid=(m // tm,),
        in_specs=[_row_spec(tm, d)] * 6 + [vec, vec, vec, _full_spec(wt["seg"].shape),
                                           _full_spec((d, d)), vec, vec],
        out_specs=_row_spec(tm, d),
        out_shape=jax.ShapeDtypeStruct((m, d), F32),
        compiler_params=_cparams("parallel"),
        name="rwkv_out",
    )(y, r, k, v, z, x, wt["r_k"], wt["lnx_g"], wt["lnx_b"], wt["seg"], wt["w_out"],
      g.reshape(1, d), b.reshape(1, d))


def _rwkv_layer(x, s0, x_last, wt, ln_g, ln_b):
    bsz, t, d = x.shape
    x_prev = jnp.concatenate([x_last[:, None, :], x[:, :-1]], axis=1)
    x2 = x.reshape(bsz * t, d)
    r, lw, k, v, kkn, a, z = _rwkv_proj(x2, x_prev.reshape(bsz * t, d), wt)
    if s0 is None:
        b3 = lambda u: u.reshape(bsz, t, d)
        y, s1 = _wkv_chunked(b3(r), b3(lw), b3(k), b3(v), b3(kkn), b3(a))
        y = y.reshape(bsz * t, d)
    else:
        y, s1 = _wkv_step(s0, r, lw, k, v, kkn, a)
    x_new = _rwkv_out(y, r, k, v, z, x2, wt, ln_g, ln_b)
    return x_new.reshape(bsz, t, d), s1, x[:, -1]


def _run_trunk(x, ln_g, ln_b, a_wts, b_wts, c_wts, past=None):
    bsz = x.shape[0]
    new = {k: [] for k in ("cmp", "sel", "win", "re", "im", "wkv", "shift")}
    for i in range(DEPTH):
        j, kind = divmod(i, N_MIXERS)
        if kind == 0:
            if past is None:
                x, kc, ks, kw = _nsa_layer_prompt(x, a_wts[j], ln_g[i], ln_b[i])
                kw = kw[:, kw.shape[1] - min(WINDOW, kw.shape[1]):]
            else:
                x, kc, ks, kw = _nsa_layer_sample(x, a_wts[j], ln_g[i], ln_b[i], j, past["cmp"],
                                                  past["sel"], past["win"], past["page_table"])
            new["cmp"].append(kc)
            new["sel"].append(ks)
            new["win"].append(kw)
        elif kind == 1:
            if past is None:
                h0r = h0i = jnp.zeros((bsz, S5_GROUPS, S5_STATE), F32)
            else:
                h0r, h0i = past["s5_re"][j], past["s5_im"][j]
            x, hr, hi = _s5_layer(x, h0r, h0i, b_wts[j], ln_g[i], ln_b[i])
            new["re"].append(hr)
            new["im"].append(hi)
        else:
            if past is None:
                s0, x_last = None, jnp.zeros((bsz, D_MODEL), F32)
            else:
                s0, x_last = past["wkv"][j], past["shift"][j]
            x, s1, xl = _rwkv_layer(x, s0, x_last, c_wts[j], ln_g[i], ln_b[i])
            new["wkv"].append(s1)
            new["shift"].append(xl)
    kv_tail = (2, A_KV_HEADS, A_HEAD_DIM)
    kv = lambda rows: jnp.stack(rows).reshape((len(rows),) + rows[0].shape[:2] + kv_tail)
    return (x, kv(new["cmp"]), kv(new["sel"]), kv(new["win"]), jnp.stack(new["re"]), jnp.stack(new["im"]),
            jnp.stack(new["wkv"]), jnp.stack(new["shift"]))


def kernel(x_prompt, x_sample, cache_cmp_kv, cache_sel_kv, cache_win_kv, state_s5_re, state_s5_im,
           state_wkv, state_shift, page_table, ln_g, ln_b,
           a_w_in, a_cmp_pe, a_cmp_w1, a_cmp_w2, a_w_out,
           b_w_in, b_log_dt, b_a_re, b_a_im, b_b_re, b_b_im, b_c_re, b_c_im, b_d, b_w_glu, b_b_glu, b_w_out,
           c_mu, c_w_in, c_w0, c_w2, c_a0, c_a2, c_k_k, c_k_a, c_r_k, c_lnx_g, c_lnx_b, c_w_out):
    a_par = (a_w_in, a_cmp_pe, a_cmp_w1, a_cmp_w2, a_w_out)
    b_par = (b_w_in, b_log_dt, b_a_re, b_a_im, b_b_re, b_b_im, b_c_re, b_c_im, b_d, b_w_glu, b_b_glu, b_w_out)
    c_par = (c_mu, c_w_in, c_w0, c_w2, c_a0, c_a2, c_k_k, c_k_a, c_r_k, c_lnx_g, c_lnx_b, c_w_out)
    a_wts = [_nsa_weights(*[p[j] for p in a_par]) for j in range(a_w_in.shape[0])]
    b_wts = [_s5_weights(*[p[j] for p in b_par]) for j in range(b_w_in.shape[0])]
    c_wts = [_rwkv_weights(*[p[j] for p in c_par]) for j in range(c_w_in.shape[0])]

    (y_p, p_cmp, p_sel, p_win, p_re, p_im, p_wkv, p_shift) = _run_trunk(
        x_prompt, ln_g, ln_b, a_wts, b_wts, c_wts)
    past = dict(cmp=cache_cmp_kv, sel=cache_sel_kv, win=cache_win_kv, page_table=page_table,
                s5_re=state_s5_re, s5_im=state_s5_im, wkv=state_wkv, shift=state_shift)
    (y_s, s_cmp, s_sel, s_win, s_re, s_im, s_wkv, s_shift) = _run_trunk(
        x_sample, ln_g, ln_b, a_wts, b_wts, c_wts, past)
    return (y_p, y_s, p_cmp, s_cmp, p_sel, s_sel, p_win, s_win,
            p_re, s_re, p_im, s_im, p_wkv, s_wkv, p_shift, s_shift)
```

```python
import functools
import math

import numpy as np
import jax
import jax.numpy as jnp
from jax import lax
from jax.experimental import pallas as pl
from jax.experimental.pallas import tpu as pltpu

F32 = jnp.float32
BF16 = jnp.bfloat16
HIGHEST = lax.Precision.HIGHEST

D_MODEL = 1024
DEPTH = 4
N_MIXERS = 3
PAGE_SIZE = 128
PAGE_GROUP = 16

A_HEADS = 16
A_KV_HEADS = 4
A_HEAD_DIM = 64
A_GROUP = 4
CMP_BLOCK = 32
CMP_STRIDE = 16
CMP_RATIO = 2
CMP_HIDDEN = 128
SEL_BLOCK = 64
SEL_RATIO = 4
SEL_TOPK = 16
WINDOW = 512
A_KV_WIDTH = 2 * A_KV_HEADS * A_HEAD_DIM

S5_GROUP_CH = 16
S5_GROUPS = 64
S5_STATE = 64
S5_SETS = 4
SCAN_TILE = 8
SCAN_LANES = 512

C_HEAD_DIM = 64
C_HEADS = 16
C_LN_EPS = 64e-5
WKV_CHUNK = 64

LN_EPS = 1e-5
DEEPNORM_ALPHA = (2 * DEPTH) ** 0.25

NEG_BIG = -1e30
LOG2E = 1.4426950408889634
POS_FEATURES = 6
MAX_POS = 64 * 256
VMEM_LIMIT = 56 * 1024 * 1024


def _cparams(*sem):
    return pltpu.CompilerParams(dimension_semantics=sem, vmem_limit_bytes=VMEM_LIMIT)


def _dot(a, b):
    return jnp.dot(a, b, preferred_element_type=F32)


def _dot_nt(a, b):
    return lax.dot_general(a, b, (((1,), (1,)), ((), ())), preferred_element_type=F32)


def _dot_hi(a, b):
    return jnp.dot(a, b, preferred_element_type=F32, precision=HIGHEST)


def _dot_nt_hi(a, b):
    return lax.dot_general(a, b, (((1,), (1,)), ((), ())), preferred_element_type=F32, precision=HIGHEST)


def _split3(x):
    hi = x.astype(BF16)
    r1 = x - hi.astype(F32)
    mid = r1.astype(BF16)
    lo = (r1 - mid.astype(F32)).astype(BF16)
    return hi, mid, lo


def _dot3(x, m01):
    hi, mid, lo = _split3(x)
    return _dot(hi, m01) + _dot(mid, m01) + _dot(lo, m01)


def _silu(z):
    return z * jax.nn.sigmoid(z)


def _proj_kernel(x_ref, *refs, n_out):
    xb = x_ref[...].astype(BF16)
    for w_ref, o_ref in zip(refs[:n_out], refs[n_out:]):
        o_ref[...] = _dot(xb, w_ref[...])


def _multi_proj(x, ws, tm=256):
    m, k = x.shape
    tm = min(tm, m)
    n_out = len(ws)
    return pl.pallas_call(
        functools.partial(_proj_kernel, n_out=n_out),
        grid=(m // tm,),
        in_specs=[pl.BlockSpec((tm, k), lambda i: (i, 0))]
        + [pl.BlockSpec(w.shape, lambda i: (0, 0)) for w in ws],
        out_specs=[pl.BlockSpec((tm, w.shape[1]), lambda i: (i, 0)) for w in ws],
        out_shape=[jax.ShapeDtypeStruct((m, w.shape[1]), F32) for w in ws],
        compiler_params=_cparams("parallel"),
        name="multi_proj",
    )(x, *ws)


def _deepnorm(x, y, g, b):
    v = DEEPNORM_ALPHA * x + y
    mu = jnp.mean(v, axis=-1, keepdims=True)
    var = jnp.mean(jnp.square(v - mu), axis=-1, keepdims=True)
    return (v - mu) * lax.rsqrt(var + LN_EPS) * g + b


def _out_ln_kernel(o_ref, z_ref, x_ref, w_ref, g_ref, b_ref, out_ref):
    gated = (o_ref[...] * _silu(z_ref[...])).astype(BF16)
    y = _dot(gated, w_ref[...])
    out_ref[...] = _deepnorm(x_ref[...], y, g_ref[...], b_ref[...])


def _row_spec(tm, n):
    return pl.BlockSpec((tm, n), lambda i: (i, 0))


def _full_spec(shape):
    return pl.BlockSpec(shape, lambda i: (0,) * len(shape))


def _out_ln(o, z, x, w_out, g, b, tm=256):
    m, d = x.shape
    tm = min(tm, m)
    return pl.pallas_call(
        _out_ln_kernel,
        grid=(m // tm,),
        in_specs=[_row_spec(tm, d), _row_spec(tm, d), _row_spec(tm, d),
                  _full_spec(w_out.shape), _full_spec((1, d)), _full_spec((1, d))],
        out_specs=_row_spec(tm, d),
        out_shape=jax.ShapeDtypeStruct((m, d), F32),
        compiler_params=_cparams("parallel"),
        name="out_ln",
    )(o, z, x, w_out, g.reshape(1, d), b.reshape(1, d))


def _pe_term_kernel(pe_ref, w1_ref, o_ref):
    for c in range(2):
        o_ref[c] = _dot_hi(pe_ref[c], w1_ref[c])


def _compress_weights(pe, w1, w2):
    pe_term = pl.pallas_call(
        _pe_term_kernel, out_shape=jax.ShapeDtypeStruct((2, 1, CMP_HIDDEN), F32), name="cmp_pe_term",
    )(pe.reshape(2, 1, CMP_BLOCK * A_HEAD_DIM), w1)
    w1r = w1.astype(BF16).reshape(2, CMP_RATIO, CMP_STRIDE // 2, 2, A_HEAD_DIM, CMP_HIDDEN)
    w_big = w1r.transpose(0, 2, 3, 4, 1, 5).reshape(2, CMP_STRIDE // 2, 2 * A_HEAD_DIM, CMP_RATIO * CMP_HIDDEN)
    return dict(pe_term=pe_term, w_big=w_big, w2=w2.astype(BF16))


def _chunk_proj(load_tap, wbig_ref):
    acc = [[None] * A_KV_HEADS for _ in range(2)]
    for c in range(2):
        for pair in range(2):
            for tp in range(CMP_STRIDE // 2):
                tap0, tap1 = load_tap(c, pair, 2 * tp), load_tap(c, pair, 2 * tp + 1)
                for half in range(2):
                    g = 2 * pair + half
                    hs = slice(half * A_HEAD_DIM, (half + 1) * A_HEAD_DIM)
                    xs = jnp.concatenate([tap0[:, hs], tap1[:, hs]], axis=1)
                    part = _dot(xs.astype(BF16), wbig_ref[c, tp])
                    acc[c][g] = part if acc[c][g] is None else acc[c][g] + part
    return acc


def _compress_finish(acc, pe_ref, w2_ref, k_ref, v_ref, n_chunk):
    for c, o_ref in ((0, k_ref), (1, v_ref)):
        for g in range(A_KV_HEADS):
            r0 = acc[c][g][:, :CMP_HIDDEN]
            r1 = acc[c][g][:, CMP_HIDDEN:]
            h = r0 + pltpu.roll(r1, n_chunk - 1, 0) + pe_ref[c]
            o_ref[g] = _dot(_silu(h).astype(BF16), w2_ref[c])


def _compress_kernel(x_ref, pe_ref, wbig_ref, w2_ref, k_ref, v_ref, *, n_chunk):
    load_tap = lambda c, pair, s: x_ref[c * 2 + pair, pl.ds(s, n_chunk, stride=CMP_STRIDE), :]
    acc = _chunk_proj(load_tap, wbig_ref)
    _compress_finish(acc, pe_ref, w2_ref, k_ref, v_ref, n_chunk)


def _compress_prompt(kvc4, bsz, cw):
    n_blk, m, _ = kvc4.shape
    t = m // bsz
    n_chunk = t // CMP_STRIDE
    out = jax.ShapeDtypeStruct((bsz, A_KV_HEADS, n_chunk, A_HEAD_DIM), F32)
    out_spec = pl.BlockSpec((None, A_KV_HEADS, n_chunk, A_HEAD_DIM), lambda b: (b, 0, 0, 0))
    return pl.pallas_call(
        functools.partial(_compress_kernel, n_chunk=n_chunk),
        grid=(bsz,),
        in_specs=[pl.BlockSpec((n_blk, t, 128), lambda b: (0, b, 0)), _full_spec(cw["pe_term"].shape),
                  _full_spec(cw["w_big"].shape), _full_spec(cw["w2"].shape)],
        out_specs=[out_spec, out_spec],
        out_shape=[out, out],
        compiler_params=_cparams("parallel"),
        name="compress_prompt",
    )(kvc4, cw["pe_term"], cw["w_big"], cw["w2"])


def _softmax_rows(s, valid):
    s = jnp.where(valid, s, NEG_BIG)
    m = jnp.max(s, axis=-1, keepdims=True)
    e = jnp.where(valid, jnp.exp(s - m), 0.0)
    return e / jnp.maximum(jnp.sum(e, axis=-1, keepdims=True), 1e-30)


def _topk_mask(score_t, n_keep):
    n_blk = score_t.shape[0]
    blk = lax.broadcasted_iota(jnp.int32, score_t.shape, 0)
    rank = jnp.zeros(score_t.shape, F32)
    for i in range(n_blk):
        row = score_t[i:i + 1, :]
        tie_ahead = jnp.where(blk > i, 1.0, 0.0)
        rank = rank + jnp.where(row > score_t, 1.0, jnp.where(row == score_t, tie_ahead, 0.0))
    return jnp.where(rank < n_keep, 1.0, 0.0)


def _tile_heads(x):
    return jnp.concatenate([x] * A_GROUP, axis=0)


def _nsa_prompt_kernel(q_ref, gate_ref, sf_ref, kc_ref, vc_ref, ks_ref, vs_ref, kw_ref, vw_ref,
                       sel_map_ref, expand_ref, o_ref, *, qb, i_base):
    t0 = (i_base + pl.program_id(2)) * qb
    n_keys = ks_ref.shape[0]
    n_cmp_pad = kc_ref.shape[0]
    n_sel = n_keys // SEL_BLOCK

    q = q_ref[...] * (A_HEAD_DIM ** -0.5 * LOG2E)
    qg = jnp.concatenate([q[:, r * 64:(r + 1) * 64] for r in range(A_GROUP)], axis=0)
    qa = jnp.concatenate([qg, sf_ref[...]], axis=1).astype(BF16)
    t_q = t0 + lax.broadcasted_iota(jnp.int32, (qb, 1), 0)

    n_idx = lax.broadcasted_iota(jnp.int32, (qb, n_cmp_pad), 1)
    ok_c = n_idx * CMP_STRIDE + (CMP_BLOCK - 1) <= t_q
    s_c = _dot_nt(qa, kc_ref[...]) + _tile_heads(jnp.where(ok_c, 0.0, NEG_BIG))
    m_c = jnp.max(s_c, axis=-1, keepdims=True)
    e_c = jnp.where(s_c > 0.5 * NEG_BIG, jnp.exp2(s_c - m_c), 0.0)
    p_c = e_c / jnp.maximum(jnp.sum(e_c, axis=-1, keepdims=True), 1e-30)
    o_c = _dot(p_c.astype(BF16), vc_ref[...])

    p_grp = p_c[0:qb] + p_c[qb:2 * qb] + p_c[2 * qb:3 * qb] + p_c[3 * qb:4 * qb]
    sel_map = sel_map_ref[...]
    hi, mid, lo = _split3(p_grp)
    p_slc_t = _dot_nt(sel_map, hi) + _dot_nt(sel_map, mid) + _dot_nt(sel_map, lo)
    blk = lax.broadcasted_iota(jnp.int32, (n_sel, qb), 0)
    cur = (t0 + lax.broadcasted_iota(jnp.int32, (n_sel, qb), 1)) // SEL_BLOCK
    forced = (blk == 0) | (blk == cur) | (blk == cur - 1)
    score_t = jnp.where(forced, 1e30, jnp.where(blk <= cur, p_slc_t, -1.0))
    keep_t = jnp.where(blk <= cur, _topk_mask(score_t, SEL_TOPK), 0.0)
    keep = keep_t.T.astype(BF16)

    k_pos = lax.broadcasted_iota(jnp.int32, (qb, n_keys), 1)
    bias = jnp.where(k_pos > t_q, NEG_BIG, (_dot(keep, expand_ref[...]) - 1.0) * (-NEG_BIG))
    s_s = _dot_nt(qa, ks_ref[...]) + _tile_heads(bias)
    p_s = jnp.exp2(s_s - jnp.max(s_s, axis=-1, keepdims=True))
    o_s = _dot(p_s.astype(BF16), vs_ref[...]) / jnp.sum(p_s, axis=-1, keepdims=True)

    n_win = min(WINDOW + qb, n_keys)
    w0 = pl.multiple_of(jnp.maximum(t0 + qb - n_win, 0), 8)
    w_pos = w0 + lax.broadcasted_iota(jnp.int32, (qb, n_win), 1)
    ok_w = (w_pos <= t_q) & (t_q - w_pos <= WINDOW)
    s_w = _dot_nt(qa, kw_ref[pl.ds(w0, n_win), :]) + _tile_heads(jnp.where(ok_w, 0.0, NEG_BIG))
    p_w = jnp.exp2(s_w - jnp.max(s_w, axis=-1, keepdims=True))
    o_w = _dot(p_w.astype(BF16), vw_ref[pl.ds(w0, n_win), :]) / jnp.sum(p_w, axis=-1, keepdims=True)

    gate = jax.nn.sigmoid(gate_ref[...])
    for r in range(A_GROUP):
        rs = slice(r * qb, (r + 1) * qb)
        o_ref[:, r * 64:(r + 1) * 64] = (gate[:, r:r + 1] * o_c[rs]
                                         + gate[:, A_GROUP + r:A_GROUP + r + 1] * o_s[rs]
                                         + gate[:, 2 * A_GROUP + r:2 * A_GROUP + r + 1] * o_w[rs])


def _nsa_proj_kernel(x_ref, wq_ref, wc_ref, ws_ref, ww_ref, wg_ref, wz_ref,
                     q_ref, kvc_ref, kvs_ref, kvw_ref, z_ref, kvc4_ref, gates_ref,
                     ks_ref, vs_ref, kw_ref, vw_ref, *, tiles_per_seq):
    xb = x_ref[...].astype(BF16)
    tm = xb.shape[0]
    q_ref[...] = _dot(xb, wq_ref[...])
    z_ref[...] = _dot(xb, wz_ref[...])
    kvc = _dot(xb, wc_ref[...])
    kvc_ref[...] = kvc
    for j in range(A_KV_WIDTH // 128):
        kvc4_ref[j] = kvc[:, j * 128:(j + 1) * 128]
    gate = _dot(xb, wg_ref[...])
    t0 = lax.rem(pl.program_id(0), tiles_per_seq) * tm
    pos = t0 + lax.broadcasted_iota(jnp.int32, (tm, A_HEAD_DIM), 0)
    lane = lax.broadcasted_iota(jnp.int32, (tm, A_HEAD_DIM), 1)
    feat = jnp.where(lane < POS_FEATURES, jnp.where(lane % 2 == 0, pos // 64, pos % 64), 0).astype(F32)
    half = A_KV_WIDTH // 2
    for w_ref, full_ref, k_ref, v_ref in ((ws_ref, kvs_ref, ks_ref, vs_ref), (ww_ref, kvw_ref, kw_ref, vw_ref)):
        kv = _dot(xb, w_ref[...])
        full_ref[...] = kv
        for g in range(A_KV_HEADS):
            keys = kv[:, g * A_HEAD_DIM:(g + 1) * A_HEAD_DIM]
            k_ref[g] = jnp.concatenate([keys, feat], axis=1).astype(BF16)
            v_ref[g] = kv[:, half + g * A_HEAD_DIM:half + (g + 1) * A_HEAD_DIM].astype(BF16)
    for g in range(A_KV_HEADS):
        gates_ref[g] = jnp.concatenate(
            [gate[:, br * A_HEADS + g * A_GROUP:br * A_HEADS + (g + 1) * A_GROUP] for br in range(3)], axis=1)


def _nsa_proj(x, ws, tm=256):
    bsz, t, d = x.shape
    assert t < MAX_POS
    m = bsz * t
    tm = min(tm, t)
    tps = t // tm
    rows = lambda n: pl.BlockSpec((tm, n), lambda i: (i, 0))
    heads = lambda n: pl.BlockSpec((None, A_KV_HEADS, tm, n), lambda i: (i // tps, 0, i % tps, 0))
    f32 = lambda *shape: jax.ShapeDtypeStruct(shape, F32)
    bf16 = lambda *shape: jax.ShapeDtypeStruct(shape, BF16)
    hd = A_HEAD_DIM
    return pl.pallas_call(
        functools.partial(_nsa_proj_kernel, tiles_per_seq=tps),
        grid=(m // tm,),
        in_specs=[rows(d)] + [_full_spec(w.shape) for w in ws],
        out_specs=[rows(d), rows(A_KV_WIDTH), rows(A_KV_WIDTH), rows(A_KV_WIDTH), rows(d),
                   pl.BlockSpec((A_KV_WIDTH // 128, tm, 128), lambda i: (0, i, 0)), heads(3 * A_GROUP),
                   heads(2 * hd), heads(hd), heads(2 * hd), heads(hd)],
        out_shape=[f32(m, d), f32(m, A_KV_WIDTH), f32(m, A_KV_WIDTH), f32(m, A_KV_WIDTH), f32(m, d),
                   f32(A_KV_WIDTH // 128, m, 128), f32(bsz, A_KV_HEADS, t, 3 * A_GROUP),
                   bf16(bsz, A_KV_HEADS, t, 2 * hd), bf16(bsz, A_KV_HEADS, t, hd),
                   bf16(bsz, A_KV_HEADS, t, 2 * hd), bf16(bsz, A_KV_HEADS, t, hd)],
        compiler_params=_cparams("parallel"),
        name="nsa_proj",
    )(x.reshape(m, d), *ws)


def _sel_map(n_sel, n_cmp_pad, n_cmp):
    j = np.arange(n_sel)[:, None]
    n = np.arange(n_cmp_pad)[None, :]
    m = (n >= SEL_RATIO * j - (CMP_RATIO - 1)) & (n <= SEL_RATIO * j + SEL_RATIO - 1) & (n < n_cmp)
    return jnp.asarray(m, BF16)


def _slope_features(slopes, qb):
    s = slopes * LOG2E
    s1 = s.astype(BF16).astype(F32)
    s2 = (s - s1).astype(BF16).astype(F32)
    s3 = (s - s1 - s2).astype(BF16).astype(F32)
    feat = jnp.stack([64.0 * s1, s1, 64.0 * s2, s2, 64.0 * s3, s3], axis=-1)
    feat = jnp.pad(feat, ((0, 0), (0, A_HEAD_DIM - POS_FEATURES)))
    return jnp.repeat(feat.reshape(A_KV_HEADS, A_GROUP, A_HEAD_DIM), qb, axis=1)


def _with_pos(keys, pos):
    assert int(pos.max()) < MAX_POS
    a, b = pos // 64, pos % 64
    feat = np.zeros((pos.shape[0], A_HEAD_DIM), np.float32)
    feat[:, 0:POS_FEATURES:2] = a[:, None]
    feat[:, 1:POS_FEATURES:2] = b[:, None]
    feat = jnp.broadcast_to(jnp.asarray(feat, BF16), keys.shape)
    return jnp.concatenate([keys.astype(BF16), feat], axis=-1)


def _nsa_prompt_attn(q, gates, ks, vs, kw, vw, k_cmp, v_cmp, slopes, qb=256, key_step=512):
    bsz, t, _ = q.shape
    key_step = min(key_step, t)
    qb = min(qb, key_step)
    k_cmp = _with_pos(k_cmp, np.arange(k_cmp.shape[2]) * CMP_STRIDE + (CMP_BLOCK - 1))
    v_cmp = v_cmp.astype(BF16)
    slope_feat = _slope_features(slopes, qb)
    rows = A_GROUP * qb
    steps = key_step // qb
    outs = []
    for seg in range(t // key_step):
        n_keys = (seg + 1) * key_step
        n_cmp_pad = n_keys // CMP_STRIDE
        n_sel = n_keys // SEL_BLOCK
        expand = jnp.asarray(np.arange(n_keys)[None, :] // SEL_BLOCK == np.arange(n_sel)[:, None], BF16)
        i_base = seg * steps
        head = lambda rows_, width: pl.BlockSpec((None, None, rows_, width), lambda b, g, i: (b, g, 0, 0))
        outs.append(pl.pallas_call(
            functools.partial(_nsa_prompt_kernel, qb=qb, i_base=i_base),
            grid=(bsz, A_KV_HEADS, steps),
            in_specs=[pl.BlockSpec((None, qb, 256), lambda b, g, i, i_base=i_base: (b, i_base + i, g)),
                      pl.BlockSpec((None, None, qb, 12), lambda b, g, i, i_base=i_base: (b, g, i_base + i, 0)),
                      pl.BlockSpec((None, rows, A_HEAD_DIM), lambda b, g, i: (g, 0, 0)),
                      head(n_cmp_pad, 2 * A_HEAD_DIM), head(n_cmp_pad, A_HEAD_DIM),
                      head(n_keys, 2 * A_HEAD_DIM), head(n_keys, A_HEAD_DIM),
                      head(n_keys, 2 * A_HEAD_DIM), head(n_keys, A_HEAD_DIM),
                      pl.BlockSpec((n_sel, n_cmp_pad), lambda b, g, i: (0, 0)),
                      pl.BlockSpec((n_sel, n_keys), lambda b, g, i: (0, 0))],
            out_specs=pl.BlockSpec((None, qb, 256), lambda b, g, i: (b, i, g)),
            out_shape=jax.ShapeDtypeStruct((bsz, key_step, D_MODEL), F32),
            compiler_params=_cparams("parallel", "parallel", "arbitrary"),
            name=f"nsa_prompt_attn_{n_keys}",
        )(q, gates, slope_feat, k_cmp, v_cmp, ks, vs, kw, vw,
          _sel_map(n_sel, n_cmp_pad, t // CMP_STRIDE - 1), expand))
    return jnp.concatenate(outs, axis=1)


def _alibi_slopes():
    return 2.0 ** (-8.0 * jnp.arange(1, A_HEADS + 1, dtype=F32) / A_HEADS)


def _nsa_weights(w_in, pe, w1, w2, w_out):
    cuts = [int(c) for c in np.cumsum([D_MODEL, A_KV_WIDTH, A_KV_WIDTH, A_KV_WIDTH, 3 * A_HEADS])]
    pieces = jnp.split(w_in.astype(BF16), cuts, axis=-1)
    return dict(w_pieces=pieces, cmp=_compress_weights(pe, w1, w2), w_out=w_out.astype(BF16))


def _nsa_layer_prompt(x, wt, ln_g, ln_b):
    bsz, t, d = x.shape
    x2 = x.reshape(bsz * t, d)
    q, kv_c, kv_s, kv_w, z, kvc4, gates, ks, vs, kw, vw = _nsa_proj(x, wt["w_pieces"])
    kv_c3, kv_s3, kv_w3 = (a.reshape(bsz, t, A_KV_WIDTH) for a in (kv_c, kv_s, kv_w))
    k_cmp, v_cmp = _compress_prompt(kvc4, bsz, wt["cmp"])
    o = _nsa_prompt_attn(q.reshape(bsz, t, d), gates, ks, vs, kw, vw, k_cmp, v_cmp, _alibi_slopes())
    x_new = _out_ln(o.reshape(bsz * t, d), z, x2, wt["w_out"], ln_g, ln_b)
    return x_new.reshape(bsz, t, d), kv_c3, kv_s3, kv_w3


def _token_minor(cache):
    nd = cache.ndim
    return cache.transpose(*range(nd - 4), nd - 3, nd - 2, nd - 1, nd - 4)


def _page_group_copies(cache_ref, layer, pt_ref, q, buf_ref, sem_ref):
    slot = lax.rem(q, 2)
    return [pltpu.make_async_copy(cache_ref.at[layer, pt_ref[q * PAGE_GROUP + i]], buf_ref.at[slot, i],
                                  sem_ref.at[slot]) for i in range(PAGE_GROUP)]


def _sample_compress_kernel(pt_ref, cache_ref, pe_ref, wbig_ref, w2_ref, k_ref, v_ref, pg_ref, xt_ref, sem,
                            *, layer, n_pages):
    b = pl.program_id(0)
    n_grp = n_pages // PAGE_GROUP
    n_total = pl.num_programs(0) * n_grp

    @pl.when(b == 0)
    def _():
        for cp in _page_group_copies(cache_ref, layer, pt_ref, 0, pg_ref, sem):
            cp.start()

    for gi in range(n_grp):
        q = b * n_grp + gi

        @pl.when(q + 1 < n_total)
        def _():
            for cp in _page_group_copies(cache_ref, layer, pt_ref, q + 1, pg_ref, sem):
                cp.start()

        for cp in _page_group_copies(cache_ref, layer, pt_ref, q, pg_ref, sem):
            cp.wait()
        slot = lax.rem(q, 2)

        def page_body(i, carry):
            row0 = pl.multiple_of((gi * PAGE_GROUP + i) * PAGE_SIZE, PAGE_SIZE)
            for c in range(2):
                for pair in range(2):
                    tiles = pg_ref[slot, i, c, pl.ds(2 * pair, 2)]
                    xt_ref[c, pair, pl.ds(row0, PAGE_SIZE), :] = tiles.reshape(2 * A_HEAD_DIM, PAGE_SIZE).T
            return carry

        lax.fori_loop(0, PAGE_GROUP, page_body, 0)

    n_chunk = n_pages * PAGE_SIZE // CMP_STRIDE
    load_tap = lambda c, pair, s: xt_ref[c, pair, pl.ds(s, n_chunk, stride=CMP_STRIDE), :]
    acc = _chunk_proj(load_tap, wbig_ref)
    _compress_finish(acc, pe_ref, w2_ref, k_ref, v_ref, n_chunk)


def _sample_compress(cache, layer, page_table, cw):
    bsz, n_pages = page_table.shape
    assert n_pages % PAGE_GROUP == 0
    n_tok = n_pages * PAGE_SIZE
    n_chunk = n_tok // CMP_STRIDE
    out = jax.ShapeDtypeStruct((bsz, A_KV_HEADS, n_chunk, A_HEAD_DIM), F32)
    out_spec = pl.BlockSpec((None, A_KV_HEADS, n_chunk, A_HEAD_DIM), lambda b, pt: (b, 0, 0, 0))
    full = lambda a: pl.BlockSpec(a.shape, lambda b, pt: (0,) * a.ndim)
    grid_spec = pltpu.PrefetchScalarGridSpec(
        num_scalar_prefetch=1, grid=(bsz,),
        in_specs=[pl.BlockSpec(memory_space=pl.ANY), full(cw["pe_term"]), full(cw["w_big"]), full(cw["w2"])],
        out_specs=[out_spec, out_spec],
        scratch_shapes=[pltpu.VMEM((2, PAGE_GROUP, 2, A_KV_HEADS, A_HEAD_DIM, PAGE_SIZE), F32),
                        pltpu.VMEM((2, 2, n_tok, 2 * A_HEAD_DIM), F32), pltpu.SemaphoreType.DMA((2,))])
    return pl.pallas_call(
        functools.partial(_sample_compress_kernel, layer=layer, n_pages=n_pages),
        grid_spec=grid_spec, out_shape=[out, out],
        compiler_params=_cparams("arbitrary"),
        name="sample_compress",
    )(page_table.reshape(-1), _token_minor(cache), cw["pe_term"], cw["w_big"], cw["w2"])


def _sample_cmp_kernel(q_ref, slope_ref, kc_ref, vc_ref, map_ref, oc_ref, idx_ref, *, past, n_cand):
    n_pad = kc_ref.shape[1]
    n_cmp = n_pad - 1
    n_idx = lax.broadcasted_iota(jnp.int32, (A_GROUP, n_pad), 1)
    dist = (past - (CMP_BLOCK - 1)) - n_idx * CMP_STRIDE
    valid = (dist >= 0) & (n_idx < n_cmp)
    q = q_ref[...] * (A_HEAD_DIM ** -0.5)
    n_keep = idx_ref.shape[1]
    width = map_ref.shape[1]
    lane = lax.broadcasted_iota(jnp.int32, (1, width), 1)
    row_i = lax.broadcasted_iota(jnp.int32, (width, width), 0)
    col_j = lax.broadcasted_iota(jnp.int32, (width, width), 1)
    for g in range(A_KV_HEADS):
        hs = slice(g * A_GROUP, (g + 1) * A_GROUP)
        s = _dot_nt(q[hs].astype(BF16), kc_ref[g].astype(BF16)) - slope_ref[hs] * dist.astype(F32)
        p = _softmax_rows(s, valid)
        oc_ref[hs, :] = _dot(p.astype(BF16), vc_ref[g].astype(BF16))
        p_grp = jnp.sum(p, axis=0, keepdims=True)
        p_slc = _dot3(p_grp, map_ref[...])
        forced = (lane == 0) | (lane == n_cand - 1)
        score = jnp.where(forced, 1e30, jnp.where(lane < n_cand, p_slc, -1.0))
        s_j = jnp.broadcast_to(score, (width, width))
        s_i = s_j.T
        tie_ahead = jnp.where(row_i < col_j, 1.0, 0.0)
        ahead = jnp.where(s_i > s_j, 1.0, jnp.where(s_i == s_j, tie_ahead, 0.0))
        rank = jnp.sum(ahead, axis=0, keepdims=True)
        want = lax.broadcasted_iota(jnp.int32, (n_keep, width), 0).astype(F32)
        picked = jnp.where(rank == want, lane.astype(F32), 0.0)
        idx_ref[g] = jnp.sum(picked, axis=-1, keepdims=True).astype(jnp.int32)


def _sample_cmp_topk(q, k_cmp, v_cmp, slopes, past):
    bsz = q.shape[0]
    n_pad = k_cmp.shape[2]
    n_cand = past // SEL_BLOCK
    width = -(-n_cand // 128) * 128
    j = np.arange(width)[None, :]
    n = np.arange(n_pad)[:, None]
    sel_map = (n >= SEL_RATIO * j - (CMP_RATIO - 1)) & (n <= SEL_RATIO * j + SEL_RATIO - 1) & (n < n_pad - 1)
    cmp_spec = pl.BlockSpec((None, A_KV_HEADS, n_pad, A_HEAD_DIM), lambda b: (b, 0, 0, 0))
    return pl.pallas_call(
        functools.partial(_sample_cmp_kernel, past=past, n_cand=n_cand),
        grid=(bsz,),
        in_specs=[pl.BlockSpec((None, A_HEADS, A_HEAD_DIM), lambda b: (b, 0, 0)),
                  _full_spec((A_HEADS, 1)), cmp_spec, cmp_spec, _full_spec((n_pad, width))],
        out_specs=[pl.BlockSpec((None, A_HEADS, A_HEAD_DIM), lambda b: (b, 0, 0)),
                   pl.BlockSpec((None, A_KV_HEADS, SEL_TOPK, 1), lambda b: (b, 0, 0, 0))],
        out_shape=[jax.ShapeDtypeStruct((bsz, A_HEADS, A_HEAD_DIM), F32),
                   jax.ShapeDtypeStruct((bsz, A_KV_HEADS, SEL_TOPK, 1), jnp.int32)],
        compiler_params=_cparams("parallel"),
        name="sample_cmp_topk",
    )(q, slopes.reshape(A_HEADS, 1), k_cmp, v_cmp, jnp.asarray(sel_map, BF16))


def _sel_tile_copies(cache_ref, layer, pt_ref, idx_ref, b, g, r, n_pages, buf_ref, sem):
    blk = idx_ref[(b * A_KV_HEADS + g) * SEL_TOPK + r]
    page = pt_ref[b * n_pages + blk // (PAGE_SIZE // SEL_BLOCK)]
    return [pltpu.make_async_copy(cache_ref.at[layer, page, c, g], buf_ref.at[g, r, c], sem) for c in range(2)]


def _sample_attn_kernel(pt_ref, idx_ref, cache_ref, q_ref, gate_ref, slope_ref, oc_ref, ks_new_ref,
                        kw_new_ref, win_ref, o_ref, buf_ref, sem, *, layer, n_pages, past):
    b = pl.program_id(0)
    n_blk = SEL_TOPK - 1
    for g in range(A_KV_HEADS):
        for r in range(n_blk):
            for cp in _sel_tile_copies(cache_ref, layer, pt_ref, idx_ref, b, g, r, n_pages, buf_ref, sem):
                cp.start()

    q = q_ref[...] * (A_HEAD_DIM ** -0.5)
    gate = jax.nn.sigmoid(gate_ref[...])
    n_buf = win_ref.shape[-1]
    tok = lax.broadcasted_iota(jnp.int32, (1, PAGE_SIZE), 1)
    dist_w = (n_buf - lax.broadcasted_iota(jnp.int32, (1, n_buf), 1)).astype(F32)

    def attend(qg, slope, keys_t, vals_t, dist, valid, k_new, v_new):
        s = _dot(qg.astype(BF16), keys_t.astype(BF16)) - slope * dist
        if valid is not None:
            s = jnp.where(valid, s, NEG_BIG)
        s_new = jnp.sum(qg * k_new, axis=-1, keepdims=True)
        m = jnp.maximum(jnp.max(s, axis=-1, keepdims=True), s_new)
        e = jnp.exp(s - m)
        e_new = jnp.exp(s_new - m)
        l = jnp.sum(e, axis=-1, keepdims=True) + e_new
        return (_dot_nt(e.astype(BF16), vals_t.astype(BF16)) + e_new * v_new) / l

    def new_row(ref, g):
        ksl = slice(g * A_HEAD_DIM, (g + 1) * A_HEAD_DIM)
        vsl = slice(A_KV_WIDTH // 2 + g * A_HEAD_DIM, A_KV_WIDTH // 2 + (g + 1) * A_HEAD_DIM)
        return ref[:, ksl], ref[:, vsl]

    out_w = []
    for g in range(A_KV_HEADS):
        hs = slice(g * A_GROUP, (g + 1) * A_GROUP)
        out_w.append(attend(q[hs], slope_ref[hs], win_ref[0, g], win_ref[1, g], dist_w, None,
                            *new_row(kw_new_ref, g)))

    for g in range(A_KV_HEADS):
        for r in range(n_blk):
            for cp in _sel_tile_copies(cache_ref, layer, pt_ref, idx_ref, b, g, r, n_pages, buf_ref, sem):
                cp.wait()

    blocks_per_page = PAGE_SIZE // SEL_BLOCK
    for g in range(A_KV_HEADS):
        hs = slice(g * A_GROUP, (g + 1) * A_GROUP)
        keys_t = jnp.concatenate([buf_ref[g, r, 0] for r in range(n_blk)], axis=1)
        vals_t = jnp.concatenate([buf_ref[g, r, 1] for r in range(n_blk)], axis=1)
        dist, valid = [], []
        for r in range(n_blk):
            blk = idx_ref[(b * A_KV_HEADS + g) * SEL_TOPK + r]
            dist.append(past - ((blk // blocks_per_page) * PAGE_SIZE + tok))
            valid.append(tok // SEL_BLOCK == blk % blocks_per_page)
        dist = jnp.concatenate(dist, axis=1).astype(F32)
        valid = jnp.concatenate(valid, axis=1)
        o_s = attend(q[hs], slope_ref[hs], keys_t, vals_t, dist, valid, *new_row(ks_new_ref, g))
        gt = gate[hs]
        o_ref[hs, :] = gt[:, 0:1] * oc_ref[hs, :] + gt[:, 1:2] * o_s + gt[:, 2:3] * out_w[g]


def _sample_attn(page_table, idx, cache_sel, layer, q, gate, slopes, o_c, kv_s_new, kv_w_new, cache_win, past):
    bsz, n_pages = page_table.shape
    n_buf = cache_win.shape[2]
    head_spec = pl.BlockSpec((None, A_HEADS, A_HEAD_DIM), lambda b, pt, ix: (b, 0, 0))
    new_spec = pl.BlockSpec((None, 1, A_KV_WIDTH), lambda b, pt, ix: (b, 0, 0))
    grid_spec = pltpu.PrefetchScalarGridSpec(
        num_scalar_prefetch=2, grid=(bsz,),
        in_specs=[pl.BlockSpec(memory_space=pl.ANY), head_spec,
                  pl.BlockSpec((None, A_HEADS, 3), lambda b, pt, ix: (b, 0, 0)),
                  pl.BlockSpec((A_HEADS, 1), lambda b, pt, ix: (0, 0)),
                  head_spec, new_spec, new_spec,
                  pl.BlockSpec((None, None, 2, A_KV_HEADS, A_HEAD_DIM, n_buf),
                               lambda b, pt, ix: (layer, b, 0, 0, 0, 0))],
        out_specs=head_spec,
        scratch_shapes=[pltpu.VMEM((A_KV_HEADS, SEL_TOPK - 1, 2, A_HEAD_DIM, PAGE_SIZE), F32),
                        pltpu.SemaphoreType.DMA(())])
    return pl.pallas_call(
        functools.partial(_sample_attn_kernel, layer=layer, n_pages=n_pages, past=past),
        grid_spec=grid_spec,
        out_shape=jax.ShapeDtypeStruct((bsz, A_HEADS, A_HEAD_DIM), F32),
        compiler_params=_cparams("arbitrary"),
        name="sample_attn",
    )(page_table.reshape(-1), idx.reshape(-1), _token_minor(cache_sel), q, gate, slopes.reshape(A_HEADS, 1),
      o_c, kv_s_new, kv_w_new, _token_minor(cache_win))


def _nsa_layer_sample(x, wt, ln_g, ln_b, layer, cache_cmp, cache_sel, cache_win, page_table):
    bsz, _, d = x.shape
    n_pages = page_table.shape[1]
    past = n_pages * PAGE_SIZE
    x2 = x.reshape(bsz, d)
    q, kv_c, kv_s, kv_w, gate, z = _multi_proj(x2, wt["w_pieces"])
    slopes = _alibi_slopes()
    k_cmp, v_cmp = _sample_compress(cache_cmp, layer, page_table, wt["cmp"])
    q3 = q.reshape(bsz, A_HEADS, A_HEAD_DIM)
    o_c, idx = _sample_cmp_topk(q3, k_cmp, v_cmp, slopes, past)
    gate3 = gate.reshape(bsz, 3, A_HEADS).transpose(0, 2, 1)
    kv_s3, kv_w3 = kv_s.reshape(bsz, 1, A_KV_WIDTH), kv_w.reshape(bsz, 1, A_KV_WIDTH)
    o = _sample_attn(page_table, idx, cache_sel, layer, q3, gate3, slopes, o_c, kv_s3, kv_w3, cache_win, past)
    x_new = _out_ln(o.reshape(bsz, d), z, x2, wt["w_out"], ln_g, ln_b)
    n_buf = cache_win.shape[2]
    n_keep = min(WINDOW, n_buf + 1)
    win_old = cache_win[layer, :, n_buf + 1 - n_keep:].reshape(bsz, n_keep - 1, A_KV_WIDTH)
    win_new = jnp.concatenate([win_old, kv_w3], axis=1)
    return x_new.reshape(bsz, 1, d), kv_c.reshape(bsz, 1, A_KV_WIDTH), kv_s3, win_new


def _s5_param_kernel(log_dt_ref, ar_ref, ai_ref, br_ref, bi_ref, tile_ref,
                     abr_ref, abi_ref, bbr_ref, bbi_ref):
    dt = jnp.exp(log_dt_ref[...])
    ar, ai = ar_ref[...], ai_ref[...]
    mag = jnp.exp(dt * ar)
    abr, abi = mag * jnp.cos(dt * ai), mag * jnp.sin(dt * ai)
    den = ar * ar + ai * ai
    num_re, num_im = abr - 1.0, abi
    zoh_re = (num_re * ar + num_im * ai) / den
    zoh_im = (num_im * ar - num_re * ai) / den
    abr_ref[...] = abr
    abi_ref[...] = abi
    zr = _dot3(zoh_re, tile_ref[...])
    zi = _dot3(zoh_im, tile_ref[...])
    br, bi = br_ref[...], bi_ref[...]
    bbr_ref[...] = zr * br - zi * bi
    bbi_ref[...] = zr * bi + zi * br


def _s5_weights(w_in, log_dt, a_re, a_im, b_re, b_im, c_re, c_im, d_skip, w_glu, b_glu, w_out):
    g, p, c = S5_GROUPS, S5_STATE, S5_GROUP_CH
    tile = jnp.asarray(np.tile(np.eye(p, dtype=np.float32), (1, c)), BF16)
    to_gcp = lambda b: b.transpose(0, 2, 1).reshape(g, c * p)
    shapes = [jax.ShapeDtypeStruct((g, p), F32)] * 2 + [jax.ShapeDtypeStruct((g, c * p), F32)] * 2
    abr, abi, bbr, bbi = pl.pallas_call(_s5_param_kernel, out_shape=shapes, name="s5_params")(
        log_dt.reshape(g, 1), a_re, a_im, to_gcp(b_re), to_gcp(b_im), tile)
    eye = jnp.eye(S5_GROUPS // S5_SETS, dtype=F32)
    gl = S5_GROUPS // S5_SETS

    def in_blockdiag(bb):
        bb = bb.reshape(S5_SETS, gl, c, p)
        return jnp.einsum('sgcp,gh->sgchp', bb, eye).reshape(S5_SETS, gl * c, gl * p).astype(BF16)

    def out_blockdiag(cc):
        cc = cc.reshape(S5_SETS, gl, c, p)
        return jnp.einsum('sgcp,gh->sgphc', cc, eye).reshape(S5_SETS, gl * p, gl * c).astype(BF16)

    w_u, w_z = jnp.split(w_in.astype(BF16), 2, axis=-1)
    return dict(w_pieces=[w_u, w_z], abr=abr.reshape(1, g * p), abi=abi.reshape(1, g * p),
                wb_re=in_blockdiag(bbr), wb_im=in_blockdiag(bbi),
                wc_re=out_blockdiag(c_re), wc_im=out_blockdiag(c_im),
                d_skip=d_skip.reshape(1, -1), w_glu=w_glu.astype(BF16), b_glu=b_glu.reshape(1, -1),
                w_out=w_out.astype(BF16))


def _cmul(ar, ai, br, bi):
    return ar * br - ai * bi, ar * bi + ai * br


def _s5_row_scan(bur, bui, ar_ref, ai_ref, hr_c, hi_c, tc):
    n_set = S5_GROUPS * S5_STATE // S5_SETS
    for s in range(S5_SETS):
        sl = pl.ds(s * n_set, n_set)
        ar, ai = ar_ref[:, sl], ai_ref[:, sl]

        def step(t, carry):
            hr, hi = carry
            pr, pi = _cmul(ar, ai, hr, hi)
            nr, ni = pr + bur[pl.ds(t, 1), sl], pi + bui[pl.ds(t, 1), sl]
            bur[pl.ds(t, 1), sl] = nr
            bui[pl.ds(t, 1), sl] = ni
            return nr, ni

        hr, hi = lax.fori_loop(0, tc, step, (hr_c[:, sl], hi_c[:, sl]))
        hr_c[:, sl] = hr
        hi_c[:, sl] = hi


def _s5_tile_scan(bur, bui, ar_ref, ai_ref, hr_c, hi_c, tc):
    width = SCAN_LANES
    row = lax.broadcasted_iota(jnp.int32, (SCAN_TILE, width), 0)
    for s in range(S5_GROUPS * S5_STATE // width):
        sl = pl.ds(s * width, width)
        ar, ai = ar_ref[:, sl], ai_ref[:, sl]
        pows = [(ar, ai)]
        for _ in range(SCAN_TILE - 1):
            pows.append(_cmul(*pows[-1], ar, ai))
        pw_r = jnp.concatenate([p[0] for p in pows], axis=0)
        pw_i = jnp.concatenate([p[1] for p in pows], axis=0)
        steps = []
        for k in (1, 2, 4):
            keep = row >= k
            steps.append((k, jnp.where(keep, pows[k - 1][0], 0.0), jnp.where(keep, pows[k - 1][1], 0.0)))

        def tile(i, carry):
            cr, ci = carry
            rows = pl.ds(pl.multiple_of(i * SCAN_TILE, SCAN_TILE), SCAN_TILE)
            yr, yi = bur[rows, sl], bui[rows, sl]
            for k, kr, ki in steps:
                pr, pi = _cmul(kr, ki, pltpu.roll(yr, k, 0), pltpu.roll(yi, k, 0))
                yr, yi = yr + pr, yi + pi
            pr, pi = _cmul(pw_r, pw_i, cr, ci)
            yr, yi = yr + pr, yi + pi
            bur[rows, sl] = yr
            bui[rows, sl] = yi
            return yr[SCAN_TILE - 1:, :], yi[SCAN_TILE - 1:, :]

        hr, hi = lax.fori_loop(0, tc // SCAN_TILE, tile, (hr_c[:, sl], hi_c[:, sl]), unroll=4)
        hr_c[:, sl] = hr
        hi_c[:, sl] = hi


def _s5_scan_kernel(u_ref, wbr_ref, wbi_ref, ar_ref, ai_ref, wcr_ref, wci_ref, h0r_ref, h0i_ref,
                    y_ref, hr_out, hi_out, bur, bui, hr_c, hi_c, *, tc):
    @pl.when(pl.program_id(1) == 0)
    def _():
        hr_c[...] = h0r_ref[...]
        hi_c[...] = h0i_ref[...]

    n_set = S5_GROUPS * S5_STATE // S5_SETS
    n_ch = S5_GROUPS * S5_GROUP_CH // S5_SETS
    ub = u_ref[...].astype(BF16)
    for s in range(S5_SETS):
        us = ub[:, s * n_ch:(s + 1) * n_ch]
        bur[:, s * n_set:(s + 1) * n_set] = _dot(us, wbr_ref[s])
        bui[:, s * n_set:(s + 1) * n_set] = _dot(us, wbi_ref[s])

    if tc % SCAN_TILE == 0:
        _s5_tile_scan(bur, bui, ar_ref, ai_ref, hr_c, hi_c, tc)
    else:
        _s5_row_scan(bur, bui, ar_ref, ai_ref, hr_c, hi_c, tc)

    for s in range(S5_SETS):
        h_re = bur[:, s * n_set:(s + 1) * n_set].astype(BF16)
        h_im = bui[:, s * n_set:(s + 1) * n_set].astype(BF16)
        y_ref[:, s * n_ch:(s + 1) * n_ch] = _dot(h_re, wcr_ref[s]) - _dot(h_im, wci_ref[s])
    hr_out[...] = hr_c[...]
    hi_out[...] = hi_c[...]


def _s5_scan(u, h0_re, h0_im, wt, tc=256):
    bsz, t, d = u.shape
    tc = min(tc, t)
    n_state = S5_GROUPS * S5_STATE
    st_spec = pl.BlockSpec((None, 1, n_state), lambda b, j: (b, 0, 0))
    full = lambda a: pl.BlockSpec(a.shape, lambda b, j: (0,) * a.ndim)
    st_shape = jax.ShapeDtypeStruct((bsz, 1, n_state), F32)
    return pl.pallas_call(
        functools.partial(_s5_scan_kernel, tc=tc),
        grid=(bsz, t // tc),
        in_specs=[pl.BlockSpec((None, tc, d), lambda b, j: (b, j, 0)),
                  full(wt["wb_re"]), full(wt["wb_im"]), full(wt["abr"]), full(wt["abi"]),
                  full(wt["wc_re"]), full(wt["wc_im"]), st_spec, st_spec],
        out_specs=[pl.BlockSpec((None, tc, d), lambda b, j: (b, j, 0)), st_spec, st_spec],
        out_shape=[jax.ShapeDtypeStruct((bsz, t, d), F32), st_shape, st_shape],
        scratch_shapes=[pltpu.VMEM((tc, n_state), F32), pltpu.VMEM((tc, n_state), F32),
                        pltpu.VMEM((1, n_state), F32), pltpu.VMEM((1, n_state), F32)],
        compiler_params=_cparams("parallel", "arbitrary"),
        name="s5_scan",
    )(u, wt["wb_re"], wt["wb_im"], wt["abr"], wt["abi"], wt["wc_re"], wt["wc_im"], h0_re, h0_im)


def _s5_out_kernel(y_ref, u_ref, z_ref, x_ref, d_ref, wg_ref, bg_ref, w_ref, g_ref, b_ref, out_ref):
    y = jax.nn.gelu(y_ref[...] + d_ref[...] * u_ref[...])
    y = y * jax.nn.sigmoid(_dot(y.astype(BF16), wg_ref[...]) + bg_ref[...])
    out = _dot((y * _silu(z_ref[...])).astype(BF16), w_ref[...])
    out_ref[...] = _deepnorm(x_ref[...], out, g_ref[...], b_ref[...])


def _s5_out(y, u, z, x, wt, g, b, tm=256):
    m, d = x.shape
    tm = min(tm, m)
    vec = _full_spec((1, d))
    return pl.pallas_call(
        _s5_out_kernel,
        grid=(m // tm,),
        in_specs=[_row_spec(tm, d)] * 4 + [vec, _full_spec((d, d)), vec, _full_spec((d, d)), vec, vec],
        out_specs=_row_spec(tm, d),
        out_shape=jax.ShapeDtypeStruct((m, d), F32),
        compiler_params=_cparams("parallel"),
        name="s5_out",
    )(y, u, z, x, wt["d_skip"], wt["w_glu"], wt["b_glu"], wt["w_out"], g.reshape(1, d), b.reshape(1, d))


def _s5_layer(x, h0_re, h0_im, wt, ln_g, ln_b):
    bsz, t, d = x.shape
    x2 = x.reshape(bsz * t, d)
    u, z = _multi_proj(x2, wt["w_pieces"])
    n_state = S5_GROUPS * S5_STATE
    y, hr, hi = _s5_scan(u.reshape(bsz, t, d), h0_re.reshape(bsz, 1, n_state),
                         h0_im.reshape(bsz, 1, n_state), wt)
    x_new = _s5_out(y.reshape(bsz * t, d), u, z, x2, wt, ln_g, ln_b)
    st = lambda h: h.reshape(bsz, S5_GROUPS, S5_STATE)
    return x_new.reshape(bsz, t, d), st(hr), st(hi)


def _head_sum(x, seg):
    w = seg.shape[0]
    return jnp.concatenate([_dot3(x[:, j:j + w], seg) for j in range(0, x.shape[1], w)], axis=1)


def _softplus(x):
    return jnp.maximum(x, 0.0) + jnp.log(1.0 + jnp.exp(-jnp.abs(x)))


def _rwkv_proj_kernel(x_ref, xp_ref, mu_ref, wr_ref, wwl_ref, wk_ref, wv_ref, wal_ref, wz_ref,
                      w0_ref, w2_ref, a0_ref, a2_ref, kk_ref, ka_ref, seg_ref,
                      r_ref, lw_ref, k_ref, v_ref, kkn_ref, a_ref, z_ref):
    x = x_ref[...]
    dx = xp_ref[...] - x
    mix = lambda m: (x + dx * mu_ref[m:m + 1, :]).astype(BF16)
    r_ref[...] = _dot(mix(0), wr_ref[...])
    w_lo = _dot(mix(1), wwl_ref[...])
    k = _dot(mix(2), wk_ref[...])
    v_ref[...] = _dot(mix(3), wv_ref[...])
    a_lo = _dot(mix(4), wal_ref[...])
    z_ref[...] = _dot(mix(5), wz_ref[...])
    w_log = -_softplus(-(w0_ref[...] + _dot(jnp.tanh(w_lo).astype(BF16), w2_ref[...]))) - 0.5
    lw_ref[...] = -jnp.exp(w_log)
    a = jax.nn.sigmoid(a0_ref[...] + _dot(a_lo.astype(BF16), a2_ref[...]))
    a_ref[...] = a
    kk = k * kk_ref[...]
    norm = jnp.sqrt(_head_sum(kk * kk, seg_ref[...]))
    kkn_ref[...] = kk / jnp.maximum(norm, 1e-12)
    k_ref[...] = k * (1.0 + (a - 1.0) * ka_ref[...])


def _head_seg(width=4 * C_HEAD_DIM):
    lane = np.arange(width) // C_HEAD_DIM
    return jnp.asarray(lane[:, None] == lane[None, :], BF16)


def _rwkv_weights(mu, w_in, w0, w2, a0, a2, k_k, k_a, r_k, lnx_g, lnx_b, w_out):
    d = D_MODEL
    lora_w, lora_a = w2.shape[0], a2.shape[0]
    cuts = [int(c) for c in np.cumsum([d, lora_w, d, d, lora_a])]
    pieces = jnp.split(w_in.astype(BF16), cuts, axis=-1)
    row = lambda a: a.reshape(1, d)
    return dict(mu=mu, pieces=pieces, w0=row(w0), w2=w2.astype(BF16), a0=row(a0), a2=a2.astype(BF16),
                k_k=row(k_k), k_a=row(k_a), r_k=row(r_k), lnx_g=row(lnx_g), lnx_b=row(lnx_b),
                w_out=w_out.astype(BF16), seg=_head_seg())


def _rwkv_proj(x, x_prev, wt, tm=256):
    m, d = x.shape
    tm = min(tm, m)
    ins = [x, x_prev, wt["mu"], *wt["pieces"], wt["w0"], wt["w2"], wt["a0"], wt["a2"],
           wt["k_k"], wt["k_a"], wt["seg"]]
    in_specs = [_row_spec(tm, d), _row_spec(tm, d)] + [_full_spec(a.shape) for a in ins[2:]]
    return pl.pallas_call(
        _rwkv_proj_kernel,
        grid=(m // tm,),
        in_specs=in_specs,
        out_specs=[_row_spec(tm, d)] * 7,
        out_shape=[jax.ShapeDtypeStruct((m, d), F32)] * 7,
        compiler_params=_cparams("parallel"),
        name="rwkv_proj",
    )(*ins)


def _dot_tn(a, b):
    return lax.dot_general(a, b, (((0,), (0,)), ((), ())), preferred_element_type=F32)


def _mm(a, w, dot=_dot):
    return dot(a.astype(BF16), w)


def _wkv_chunk_kernel(r_ref, lw_ref, k_ref, v_ref, kkn_ref, a_ref, y_ref, s_out, s_ref, *, chunk):
    n = chunk
    gw = 4 * C_HEAD_DIM
    n_grp = C_HEADS // 4

    @pl.when(pl.program_id(1) == 0)
    def _():
        s_ref[...] = jnp.zeros(s_ref.shape, F32)

    lw = lw_ref[...]
    row = lax.broadcasted_iota(jnp.int32, (n, n), 0)
    col = lax.broadcasted_iota(jnp.int32, (n, n), 1)
    cum = _dot_hi(jnp.where(row >= col, 1.0, 0.0), lw)
    tot = cum[n - 1:n, :]
    kkn = kkn_ref[...]
    b_vec = kkn * a_ref[...]
    e_neg = jnp.exp(-cum)
    e_tail = jnp.exp(tot - cum)
    a_t = -kkn * jnp.exp(cum - lw)
    r_t = r_ref[...] * jnp.exp(cum)
    k = k_ref[...]
    b_t, k_t = b_vec * e_neg, k * e_neg
    b_h, k_h = b_vec * e_tail, k * e_tail
    p_tot = jnp.exp(tot)
    v = v_ref[...]

    t_idx = lax.broadcasted_iota(jnp.int32, (n, gw), 0)
    s_idx = lax.rem(lax.broadcasted_iota(jnp.int32, (n, gw), 1), C_HEAD_DIM)
    strict = jnp.where(s_idx < t_idx, 1.0, 0.0)
    incl = jnp.where(s_idx <= t_idx, 1.0, 0.0)
    eye = jnp.where(s_idx == t_idx, 1.0, 0.0)
    same_head = (lax.broadcasted_iota(jnp.int32, (gw, gw), 0) // C_HEAD_DIM
                 == lax.broadcasted_iota(jnp.int32, (gw, gw), 1) // C_HEAD_DIM)
    head_mask = jnp.where(same_head, 1.0, 0.0)
    head_mask_bf = head_mask.astype(BF16)

    def blockdiag(x):
        return jnp.concatenate([x.astype(BF16)] * 4, axis=0) * head_mask_bf

    grp = range(n_grp)
    sls = [slice(g * gw, (g + 1) * gw) for g in grp]
    ar = [jnp.concatenate([a_t[:, sl], r_t[:, sl]], axis=0) for sl in sls]
    ab = [_mm(ar[g], blockdiag(b_t[:, sls[g]]), _dot_nt) for g in grp]
    ak = [_mm(ar[g], blockdiag(k_t[:, sls[g]]), _dot_nt) for g in grp]
    s0 = [s_ref[g] for g in grp]
    xs = [_mm(ar[g], s0[g].astype(BF16), _dot_nt) for g in grp]
    v_bd = [blockdiag(v[:, sl]) for sl in sls]
    a_ab = [ab[g][:n] * strict for g in grp]
    inv = [eye + a_ab[g] for g in grp]
    pw = a_ab
    for _ in range(int(math.log2(n)) - 1):
        pw = [_mm(pw[g], blockdiag(pw[g])) for g in grp]
        inv = [inv[g] + _mm(inv[g], blockdiag(pw[g])) for g in grp]
    x = [xs[g][:n] + _mm(ak[g][:n] * strict, v_bd[g]) for g in grp]
    u = [_mm(inv[g], blockdiag(x[g])) for g in grp]
    for g in grp:
        y_ref[:, sls[g]] = (xs[g][n:] + _mm(ab[g][n:] * incl, blockdiag(u[g]))
                            + _mm(ak[g][n:] * incl, v_bd[g]))
    for g in grp:
        uv = jnp.concatenate([u[g], v[:, sls[g]]], axis=0)
        bk = jnp.concatenate([b_h[:, sls[g]], k_h[:, sls[g]]], axis=0)
        s_ref[g] = s0[g] * p_tot[:, sls[g]] + _mm(uv, bk.astype(BF16), _dot_tn) * head_mask

    @pl.when(pl.program_id(1) == pl.num_programs(1) - 1)
    def _():
        for h in range(C_HEADS):
            g, hl = divmod(h, 4)
            blk = slice(hl * C_HEAD_DIM, (hl + 1) * C_HEAD_DIM)
            s_out[h] = s_ref[g, blk, blk]


def _wkv_chunked(r, lw, k, v, kkn, a, chunk=WKV_CHUNK):
    bsz, t, d = r.shape
    spec = pl.BlockSpec((None, chunk, d), lambda b, j: (b, j, 0))
    s_spec = pl.BlockSpec((None, C_HEADS, C_HEAD_DIM, C_HEAD_DIM), lambda b, j: (b, 0, 0, 0))
    return pl.pallas_call(
        functools.partial(_wkv_chunk_kernel, chunk=chunk),
        grid=(bsz, t // chunk),
        in_specs=[spec] * 6,
        out_specs=[spec, s_spec],
        out_shape=[jax.ShapeDtypeStruct((bsz, t, d), F32),
                   jax.ShapeDtypeStruct((bsz, C_HEADS, C_HEAD_DIM, C_HEAD_DIM), F32)],
        scratch_shapes=[pltpu.VMEM((C_HEADS // 4, 4 * C_HEAD_DIM, 4 * C_HEAD_DIM), F32)],
        compiler_params=_cparams("parallel", "arbitrary"),
        name="wkv_chunked",
    )(r, lw, k, v, kkn, a)


def _wkv_step_kernel(s_ref, r_ref, lw_ref, k_ref, kkn_ref, a_ref, v_ref, y_ref, s_out):
    s0 = s_ref[...]
    kkn = kkn_ref[...]
    sa = jnp.sum(s0 * (-kkn), axis=-1, keepdims=True)
    s1 = s0 * jnp.exp(lw_ref[...]) + sa * (kkn * a_ref[...]) + v_ref[...] * k_ref[...]
    s_out[...] = s1
    y_ref[...] = jnp.sum(s1 * r_ref[...], axis=-1, keepdims=True)


def _wkv_step(s0, r, lw, k, v, kkn, a):
    bsz = s0.shape[0]
    rowv = lambda x: x.reshape(bsz, C_HEADS, 1, C_HEAD_DIM)
    colv = lambda x: x.reshape(bsz, C_HEADS, C_HEAD_DIM, 1)
    s_spec = pl.BlockSpec((None, C_HEADS, C_HEAD_DIM, C_HEAD_DIM), lambda b: (b, 0, 0, 0))
    r_spec = pl.BlockSpec((None, C_HEADS, 1, C_HEAD_DIM), lambda b: (b, 0, 0, 0))
    c_spec = pl.BlockSpec((None, C_HEADS, C_HEAD_DIM, 1), lambda b: (b, 0, 0, 0))
    y, s1 = pl.pallas_call(
        _wkv_step_kernel,
        grid=(bsz,),
        in_specs=[s_spec] + [r_spec] * 5 + [c_spec],
        out_specs=[c_spec, s_spec],
        out_shape=[jax.ShapeDtypeStruct((bsz, C_HEADS, C_HEAD_DIM, 1), F32),
                   jax.ShapeDtypeStruct(s0.shape, F32)],
        compiler_params=_cparams("parallel"),
        name="wkv_step",
    )(s0, rowv(r), rowv(lw), rowv(k), rowv(kkn), rowv(a), colv(v))
    return y.reshape(bsz, D_MODEL), s1


def _rwkv_out_kernel(y_ref, r_ref, k_ref, v_ref, z_ref, x_ref, rk_ref, lg_ref, lb_ref, seg_ref,
                     w_ref, g_ref, b_ref, out_ref):
    seg = seg_ref[...]
    y = y_ref[...]
    inv_n = 1.0 / C_HEAD_DIM
    yc = y - _head_sum(y, seg) * inv_n
    var = _head_sum(yc * yc, seg) * inv_n
    yn = yc * lax.rsqrt(var + C_LN_EPS) * lg_ref[...] + lb_ref[...]
    v = v_ref[...]
    yy = yn + _head_sum(r_ref[...] * k_ref[...] * rk_ref[...], seg) * v
    out = _dot((yy * _silu(z_ref[...])).astype(BF16), w_ref[...])
    out_ref[...] = _deepnorm(x_ref[...], out, g_ref[...], b_ref[...])


def _rwkv_out(y, r, k, v, z, x, wt, g, b, tm=256):
    m, d = x.shape
    tm = min(tm, m)
    vec = _full_spec((1, d))
    return pl.pallas_call(
        _rwkv_out_kernel,
        grid=(m // tm,),
        in_specs=[_row_spec(tm, d)] * 6 + [vec, vec, vec, _full_spec(wt["seg"].shape),
                                           _full_spec((d, d)), vec, vec],
        out_specs=_row_spec(tm, d),
        out_shape=jax.ShapeDtypeStruct((m, d), F32),
        compiler_params=_cparams("parallel"),
        name="rwkv_out",
    )(y, r, k, v, z, x, wt["r_k"], wt["lnx_g"], wt["lnx_b"], wt["seg"], wt["w_out"],
      g.reshape(1, d), b.reshape(1, d))


def _rwkv_layer(x, s0, x_last, wt, ln_g, ln_b):
    bsz, t, d = x.shape
    x_prev = jnp.concatenate([x_last[:, None, :], x[:, :-1]], axis=1)
    x2 = x.reshape(bsz * t, d)
    r, lw, k, v, kkn, a, z = _rwkv_proj(x2, x_prev.reshape(bsz * t, d), wt)
    if s0 is None:
        b3 = lambda u: u.reshape(bsz, t, d)
        y, s1 = _wkv_chunked(b3(r), b3(lw), b3(k), b3(v), b3(kkn), b3(a))
        y = y.reshape(bsz * t, d)
    else:
        y, s1 = _wkv_step(s0, r, lw, k, v, kkn, a)
    x_new = _rwkv_out(y, r, k, v, z, x2, wt, ln_g, ln_b)
    return x_new.reshape(bsz, t, d), s1, x[:, -1]


def _run_trunk(x, ln_g, ln_b, a_wts, b_wts, c_wts, past=None):
    bsz = x.shape[0]
    new = {k: [] for k in ("cmp", "sel", "win", "re", "im", "wkv", "shift")}
    for i in range(DEPTH):
        j, kind = divmod(i, N_MIXERS)
        if kind == 0:
            if past is None:
                x, kc, ks, kw = _nsa_layer_prompt(x, a_wts[j], ln_g[i], ln_b[i])
                kw = kw[:, kw.shape[1] - min(WINDOW, kw.shape[1]):]
            else:
                x, kc, ks, kw = _nsa_layer_sample(x, a_wts[j], ln_g[i], ln_b[i], j, past["cmp"],
                                                  past["sel"], past["win"], past["page_table"])
            new["cmp"].append(kc)
            new["sel"].append(ks)
            new["win"].append(kw)
        elif kind == 1:
            if past is None:
                h0r = h0i = jnp.zeros((bsz, S5_GROUPS, S5_STATE), F32)
            else:
                h0r, h0i = past["s5_re"][j], past["s5_im"][j]
            x, hr, hi = _s5_layer(x, h0r, h0i, b_wts[j], ln_g[i], ln_b[i])
            new["re"].append(hr)
            new["im"].append(hi)
        else:
            if past is None:
                s0, x_last = None, jnp.zeros((bsz, D_MODEL), F32)
            else:
                s0, x_last = past["wkv"][j], past["shift"][j]
            x, s1, xl = _rwkv_layer(x, s0, x_last, c_wts[j], ln_g[i], ln_b[i])
            new["wkv"].append(s1)
            new["shift"].append(xl)
    kv_tail = (2, A_KV_HEADS, A_HEAD_DIM)
    kv = lambda rows: jnp.stack(rows).reshape((len(rows),) + rows[0].shape[:2] + kv_tail)
    return (x, kv(new["cmp"]), kv(new["sel"]), kv(new["win"]), jnp.stack(new["re"]), jnp.stack(new["im"]),
            jnp.stack(new["wkv"]), jnp.stack(new["shift"]))


def kernel(x_prompt, x_sample, cache_cmp_kv, cache_sel_kv, cache_win_kv, state_s5_re, state_s5_im,
           state_wkv, state_shift, page_table, ln_g, ln_b,
           a_w_in, a_cmp_pe, a_cmp_w1, a_cmp_w2, a_w_out,
           b_w_in, b_log_dt, b_a_re, b_a_im, b_b_re, b_b_im, b_c_re, b_c_im, b_d, b_w_glu, b_b_glu, b_w_out,
           c_mu, c_w_in, c_w0, c_w2, c_a0, c_a2, c_k_k, c_k_a, c_r_k, c_lnx_g, c_lnx_b, c_w_out):
    a_par = (a_w_in, a_cmp_pe, a_cmp_w1, a_cmp_w2, a_w_out)
    b_par = (b_w_in, b_log_dt, b_a_re, b_a_im, b_b_re, b_b_im, b_c_re, b_c_im, b_d, b_w_glu, b_b_glu, b_w_out)
    c_par = (c_mu, c_w_in, c_w0, c_w2, c_a0, c_a2, c_k_k, c_k_a, c_r_k, c_lnx_g, c_lnx_b, c_w_out)
    a_wts = [_nsa_weights(*[p[j] for p in a_par]) for j in range(a_w_in.shape[0])]
    b_wts = [_s5_weights(*[p[j] for p in b_par]) for j in range(b_w_in.shape[0])]
    c_wts = [_rwkv_weights(*[p[j] for p in c_par]) for j in range(c_w_in.shape[0])]

    (y_p, p_cmp, p_sel, p_win, p_re, p_im, p_wkv, p_shift) = _run_trunk(
        x_prompt, ln_g, ln_b, a_wts, b_wts, c_wts)
    past = dict(cmp=cache_cmp_kv, sel=cache_sel_kv, win=cache_win_kv, page_table=page_table,
                s5_re=state_s5_re, s5_im=state_s5_im, wkv=state_wkv, shift=state_shift)
    (y_s, s_cmp, s_sel, s_win, s_re, s_im, s_wkv, s_shift) = _run_trunk(
        x_sample, ln_g, ln_b, a_wts, b_wts, c_wts, past)
    return (y_p, y_s, p_cmp, s_cmp, p_sel, s_sel, p_win, s_win,
            p_re, s_re, p_im, s_im, p_wkv, s_wkv, p_shift, s_shift)
```

```python
import functools
import math

import numpy as np
import jax
import jax.numpy as jnp
from jax import lax
from jax.experimental import pallas as pl
from jax.experimental.pallas import tpu as pltpu

F32 = jnp.float32
BF16 = jnp.bfloat16
HIGHEST = lax.Precision.HIGHEST

D_MODEL = 1024
DEPTH = 4
N_MIXERS = 3
PAGE_SIZE = 128
PAGE_GROUP = 16

A_HEADS = 16
A_KV_HEADS = 4
A_HEAD_DIM = 64
A_GROUP = 4
CMP_BLOCK = 32
CMP_STRIDE = 16
CMP_RATIO = 2
CMP_HIDDEN = 128
SEL_BLOCK = 64
SEL_RATIO = 4
SEL_TOPK = 16
WINDOW = 512
A_KV_WIDTH = 2 * A_KV_HEADS * A_HEAD_DIM

S5_GROUP_CH = 16
S5_GROUPS = 64
S5_STATE = 64
S5_SETS = 4
SCAN_TILE = 8
SCAN_LANES = 512

C_HEAD_DIM = 64
C_HEADS = 16
C_LN_EPS = 64e-5
WKV_CHUNK = 64

LN_EPS = 1e-5
DEEPNORM_ALPHA = (2 * DEPTH) ** 0.25

NEG_BIG = -1e30
LOG2E = 1.4426950408889634
POS_FEATURES = 6
MAX_POS = 64 * 256
VMEM_LIMIT = 56 * 1024 * 1024


def _cparams(*sem):
    return pltpu.CompilerParams(dimension_semantics=sem, vmem_limit_bytes=VMEM_LIMIT)


def _dot(a, b):
    return jnp.dot(a, b, preferred_element_type=F32)


def _dot_nt(a, b):
    return lax.dot_general(a, b, (((1,), (1,)), ((), ())), preferred_element_type=F32)


def _dot_hi(a, b):
    return jnp.dot(a, b, preferred_element_type=F32, precision=HIGHEST)


def _dot_nt_hi(a, b):
    return lax.dot_general(a, b, (((1,), (1,)), ((), ())), preferred_element_type=F32, precision=HIGHEST)


def _split3(x):
    hi = x.astype(BF16)
    r1 = x - hi.astype(F32)
    mid = r1.astype(BF16)
    lo = (r1 - mid.astype(F32)).astype(BF16)
    return hi, mid, lo


def _dot3(x, m01):
    hi, mid, lo = _split3(x)
    return _dot(hi, m01) + _dot(mid, m01) + _dot(lo, m01)


def _silu(z):
    return z * jax.nn.sigmoid(z)


def _proj_kernel(x_ref, *refs, n_out):
    xb = x_ref[...].astype(BF16)
    for w_ref, o_ref in zip(refs[:n_out], refs[n_out:]):
        o_ref[...] = _dot(xb, w_ref[...])


def _multi_proj(x, ws, tm=256):
    m, k = x.shape
    tm = min(tm, m)
    n_out = len(ws)
    return pl.pallas_call(
        functools.partial(_proj_kernel, n_out=n_out),
        grid=(m // tm,),
        in_specs=[pl.BlockSpec((tm, k), lambda i: (i, 0))]
        + [pl.BlockSpec(w.shape, lambda i: (0, 0)) for w in ws],
        out_specs=[pl.BlockSpec((tm, w.shape[1]), lambda i: (i, 0)) for w in ws],
        out_shape=[jax.ShapeDtypeStruct((m, w.shape[1]), F32) for w in ws],
        compiler_params=_cparams("parallel"),
        name="multi_proj",
    )(x, *ws)


def _deepnorm(x, y, g, b):
    v = DEEPNORM_ALPHA * x + y
    mu = jnp.mean(v, axis=-1, keepdims=True)
    var = jnp.mean(jnp.square(v - mu), axis=-1, keepdims=True)
    return (v - mu) * lax.rsqrt(var + LN_EPS) * g + b


def _out_ln_kernel(o_ref, z_ref, x_ref, w_ref, g_ref, b_ref, out_ref):
    gated = (o_ref[...] * _silu(z_ref[...])).astype(BF16)
    y = _dot(gated, w_ref[...])
    out_ref[...] = _deepnorm(x_ref[...], y, g_ref[...], b_ref[...])


def _row_spec(tm, n):
    return pl.BlockSpec((tm, n), lambda i: (i, 0))


def _full_spec(shape):
    return pl.BlockSpec(shape, lambda i: (0,) * len(shape))


def _out_ln(o, z, x, w_out, g, b, tm=256):
    m, d = x.shape
    tm = min(tm, m)
    return pl.pallas_call(
        _out_ln_kernel,
        grid=(m // tm,),
        in_specs=[_row_spec(tm, d), _row_spec(tm, d), _row_spec(tm, d),
                  _full_spec(w_out.shape), _full_spec((1, d)), _full_spec((1, d))],
        out_specs=_row_spec(tm, d),
        out_shape=jax.ShapeDtypeStruct((m, d), F32),
        compiler_params=_cparams("parallel"),
        name="out_ln",
    )(o, z, x, w_out, g.reshape(1, d), b.reshape(1, d))


def _pe_term_kernel(pe_ref, w1_ref, o_ref):
    for c in range(2):
        o_ref[c] = _dot_hi(pe_ref[c], w1_ref[c])


def _compress_weights(pe, w1, w2):
    pe_term = pl.pallas_call(
        _pe_term_kernel, out_shape=jax.ShapeDtypeStruct((2, 1, CMP_HIDDEN), F32), name="cmp_pe_term",
    )(pe.reshape(2, 1, CMP_BLOCK * A_HEAD_DIM), w1)
    w1r = w1.astype(BF16).reshape(2, CMP_RATIO, CMP_STRIDE // 2, 2, A_HEAD_DIM, CMP_HIDDEN)
    w_big = w1r.transpose(0, 2, 3, 4, 1, 5).reshape(2, CMP_STRIDE // 2, 2 * A_HEAD_DIM, CMP_RATIO * CMP_HIDDEN)
    return dict(pe_term=pe_term, w_big=w_big, w2=w2.astype(BF16))


def _chunk_proj(load_tap, wbig_ref):
    acc = [[None] * A_KV_HEADS for _ in range(2)]
    for c in range(2):
        for pair in range(2):
            for tp in range(CMP_STRIDE // 2):
                tap0, tap1 = load_tap(c, pair, 2 * tp), load_tap(c, pair, 2 * tp + 1)
                for half in range(2):
                    g = 2 * pair + half
                    hs = slice(half * A_HEAD_DIM, (half + 1) * A_HEAD_DIM)
                    xs = jnp.concatenate([tap0[:, hs], tap1[:, hs]], axis=1)
                    part = _dot(xs.astype(BF16), wbig_ref[c, tp])
                    acc[c][g] = part if acc[c][g] is None else acc[c][g] + part
    return acc


def _compress_finish(acc, pe_ref, w2_ref, k_ref, v_ref, n_chunk):
    for c, o_ref in ((0, k_ref), (1, v_ref)):
        for g in range(A_KV_HEADS):
            r0 = acc[c][g][:, :CMP_HIDDEN]
            r1 = acc[c][g][:, CMP_HIDDEN:]
            h = r0 + pltpu.roll(r1, n_chunk - 1, 0) + pe_ref[c]
            o_ref[g] = _dot(_silu(h).astype(BF16), w2_ref[c])


def _compress_kernel(x_ref, pe_ref, wbig_ref, w2_ref, k_ref, v_ref, *, n_chunk):
    load_tap = lambda c, pair, s: x_ref[c * 2 + pair, pl.ds(s, n_chunk, stride=CMP_STRIDE), :]
    acc = _chunk_proj(load_tap, wbig_ref)
    _compress_finish(acc, pe_ref, w2_ref, k_ref, v_ref, n_chunk)


def _compress_prompt(kvc4, bsz, cw):
    n_blk, m, _ = kvc4.shape
    t = m // bsz
    n_chunk = t // CMP_STRIDE
    out = jax.ShapeDtypeStruct((bsz, A_KV_HEADS, n_chunk, A_HEAD_DIM), F32)
    out_spec = pl.BlockSpec((None, A_KV_HEADS, n_chunk, A_HEAD_DIM), lambda b: (b, 0, 0, 0))
    return pl.pallas_call(
        functools.partial(_compress_kernel, n_chunk=n_chunk),
        grid=(bsz,),
        in_specs=[pl.BlockSpec((n_blk, t, 128), lambda b: (0, b, 0)), _full_spec(cw["pe_term"].shape),
                  _full_spec(cw["w_big"].shape), _full_spec(cw["w2"].shape)],
        out_specs=[out_spec, out_spec],
        out_shape=[out, out],
        compiler_params=_cparams("parallel"),
        name="compress_prompt",
    )(kvc4, cw["pe_term"], cw["w_big"], cw["w2"])


def _softmax_rows(s, valid):
    s = jnp.where(valid, s, NEG_BIG)
    m = jnp.max(s, axis=-1, keepdims=True)
    e = jnp.where(valid, jnp.exp(s - m), 0.0)
    return e / jnp.maximum(jnp.sum(e, axis=-1, keepdims=True), 1e-30)


def _topk_mask(score_t, n_keep):
    n_blk = score_t.shape[0]
    blk = lax.broadcasted_iota(jnp.int32, score_t.shape, 0)
    rank = jnp.zeros(score_t.shape, F32)
    for i in range(n_blk):
        row = score_t[i:i + 1, :]
        tie_ahead = jnp.where(blk > i, 1.0, 0.0)
        rank = rank + jnp.where(row > score_t, 1.0, jnp.where(row == score_t, tie_ahead, 0.0))
    return jnp.where(rank < n_keep, 1.0, 0.0)


def _tile_heads(x):
    return jnp.concatenate([x] * A_GROUP, axis=0)


def _nsa_prompt_kernel(q_ref, gate_ref, sf_ref, kc_ref, vc_ref, ks_ref, vs_ref, kw_ref, vw_ref,
                       sel_map_ref, expand_ref, o_ref, *, qb, i_base):
    t0 = (i_base + pl.program_id(2)) * qb
    n_keys = ks_ref.shape[0]
    n_cmp_pad = kc_ref.shape[0]
    n_sel = n_keys // SEL_BLOCK

    q = q_ref[...] * (A_HEAD_DIM ** -0.5 * LOG2E)
    qg = jnp.concatenate([q[:, r * 64:(r + 1) * 64] for r in range(A_GROUP)], axis=0)
    qa = jnp.concatenate([qg, sf_ref[...]], axis=1).astype(BF16)
    t_q = t0 + lax.broadcasted_iota(jnp.int32, (qb, 1), 0)

    n_idx = lax.broadcasted_iota(jnp.int32, (qb, n_cmp_pad), 1)
    ok_c = n_idx * CMP_STRIDE + (CMP_BLOCK - 1) <= t_q
    s_c = _dot_nt(qa, kc_ref[...]) + _tile_heads(jnp.where(ok_c, 0.0, NEG_BIG))
    m_c = jnp.max(s_c, axis=-1, keepdims=True)
    e_c = jnp.where(s_c > 0.5 * NEG_BIG, jnp.exp2(s_c - m_c), 0.0)
    p_c = e_c / jnp.maximum(jnp.sum(e_c, axis=-1, keepdims=True), 1e-30)
    o_c = _dot(p_c.astype(BF16), vc_ref[...])

    p_grp = p_c[0:qb] + p_c[qb:2 * qb] + p_c[2 * qb:3 * qb] + p_c[3 * qb:4 * qb]
    sel_map = sel_map_ref[...]
    hi, mid, lo = _split3(p_grp)
    p_slc_t = _dot_nt(sel_map, hi) + _dot_nt(sel_map, mid) + _dot_nt(sel_map, lo)
    blk = lax.broadcasted_iota(jnp.int32, (n_sel, qb), 0)
    cur = (t0 + lax.broadcasted_iota(jnp.int32, (n_sel, qb), 1)) // SEL_BLOCK
    forced = (blk == 0) | (blk == cur) | (blk == cur - 1)
    score_t = jnp.where(forced, 1e30, jnp.where(blk <= cur, p_slc_t, -1.0))
    keep_t = jnp.where(blk <= cur, _topk_mask(score_t, SEL_TOPK), 0.0)
    keep = keep_t.T.astype(BF16)

    k_pos = lax.broadcasted_iota(jnp.int32, (qb, n_keys), 1)
    bias = jnp.where(k_pos > t_q, NEG_BIG, (_dot(keep, expand_ref[...]) - 1.0) * (-NEG_BIG))
    s_s = _dot_nt(qa, ks_ref[...]) + _tile_heads(bias)
    p_s = jnp.exp2(s_s - jnp.max(s_s, axis=-1, keepdims=True))
    o_s = _dot(p_s.astype(BF16), vs_ref[...]) / jnp.sum(p_s, axis=-1, keepdims=True)

    n_win = min(WINDOW + qb, n_keys)
    w0 = pl.multiple_of(jnp.maximum(t0 + qb - n_win, 0), 8)
    w_pos = w0 + lax.broadcasted_iota(jnp.int32, (qb, n_win), 1)
    ok_w = (w_pos <= t_q) & (t_q - w_pos <= WINDOW)
    s_w = _dot_nt(qa, kw_ref[pl.ds(w0, n_win), :]) + _tile_heads(jnp.where(ok_w, 0.0, NEG_BIG))
    p_w = jnp.exp2(s_w - jnp.max(s_w, axis=-1, keepdims=True))
    o_w = _dot(p_w.astype(BF16), vw_ref[pl.ds(w0, n_win), :]) / jnp.sum(p_w, axis=-1, keepdims=True)

    gate = jax.nn.sigmoid(gate_ref[...])
    for r in range(A_GROUP):
        rs = slice(r * qb, (r + 1) * qb)
        o_ref[:, r * 64:(r + 1) * 64] = (gate[:, r:r + 1] * o_c[rs]
                                         + gate[:, A_GROUP + r:A_GROUP + r + 1] * o_s[rs]
                                         + gate[:, 2 * A_GROUP + r:2 * A_GROUP + r + 1] * o_w[rs])


def _nsa_proj_kernel(x_ref, wq_ref, wc_ref, ws_ref, ww_ref, wg_ref, wz_ref,
                     q_ref, kvc_ref, kvs_ref, kvw_ref, z_ref, kvc4_ref, gates_ref,
                     ks_ref, vs_ref, kw_ref, vw_ref, *, tiles_per_seq):
    xb = x_ref[...].astype(BF16)
    tm = xb.shape[0]
    q_ref[...] = _dot(xb, wq_ref[...])
    z_ref[...] = _dot(xb, wz_ref[...])
    kvc = _dot(xb, wc_ref[...])
    kvc_ref[...] = kvc
    for j in range(A_KV_WIDTH // 128):
        kvc4_ref[j] = kvc[:, j * 128:(j + 1) * 128]
    gate = _dot(xb, wg_ref[...])
    t0 = lax.rem(pl.program_id(0), tiles_per_seq) * tm
    pos = t0 + lax.broadcasted_iota(jnp.int32, (tm, A_HEAD_DIM), 0)
    lane = lax.broadcasted_iota(jnp.int32, (tm, A_HEAD_DIM), 1)
    feat = jnp.where(lane < POS_FEATURES, jnp.where(lane % 2 == 0, pos // 64, pos % 64), 0).astype(F32)
    half = A_KV_WIDTH // 2
    for w_ref, full_ref, k_ref, v_ref in ((ws_ref, kvs_ref, ks_ref, vs_ref), (ww_ref, kvw_ref, kw_ref, vw_ref)):
        kv = _dot(xb, w_ref[...])
        full_ref[...] = kv
        for g in range(A_KV_HEADS):
            keys = kv[:, g * A_HEAD_DIM:(g + 1) * A_HEAD_DIM]
            k_ref[g] = jnp.concatenate([keys, feat], axis=1).astype(BF16)
            v_ref[g] = kv[:, half + g * A_HEAD_DIM:half + (g + 1) * A_HEAD_DIM].astype(BF16)
    for g in range(A_KV_HEADS):
        gates_ref[g] = jnp.concatenate(
            [gate[:, br * A_HEADS + g * A_GROUP:br * A_HEADS + (g + 1) * A_GROUP] for br in range(3)], axis=1)


def _nsa_proj(x, ws, tm=256):
    bsz, t, d = x.shape
    assert t < MAX_POS
    m = bsz * t
    tm = min(tm, t)
    tps = t // tm
    rows = lambda n: pl.BlockSpec((tm, n), lambda i: (i, 0))
    heads = lambda n: pl.BlockSpec((None, A_KV_HEADS, tm, n), lambda i: (i // tps, 0, i % tps, 0))
    f32 = lambda *shape: jax.ShapeDtypeStruct(shape, F32)
    bf16 = lambda *shape: jax.ShapeDtypeStruct(shape, BF16)
    hd = A_HEAD_DIM
    return pl.pallas_call(
        functools.partial(_nsa_proj_kernel, tiles_per_seq=tps),
        grid=(m // tm,),
        in_specs=[rows(d)] + [_full_spec(w.shape) for w in ws],
        out_specs=[rows(d), rows(A_KV_WIDTH), rows(A_KV_WIDTH), rows(A_KV_WIDTH), rows(d),
                   pl.BlockSpec((A_KV_WIDTH // 128, tm, 128), lambda i: (0, i, 0)), heads(3 * A_GROUP),
                   heads(2 * hd), heads(hd), heads(2 * hd), heads(hd)],
        out_shape=[f32(m, d), f32(m, A_KV_WIDTH), f32(m, A_KV_WIDTH), f32(m, A_KV_WIDTH), f32(m, d),
                   f32(A_KV_WIDTH // 128, m, 128), f32(bsz, A_KV_HEADS, t, 3 * A_GROUP),
                   bf16(bsz, A_KV_HEADS, t, 2 * hd), bf16(bsz, A_KV_HEADS, t, hd),
                   bf16(bsz, A_KV_HEADS, t, 2 * hd), bf16(bsz, A_KV_HEADS, t, hd)],
        compiler_params=_cparams("parallel"),
        name="nsa_proj",
    )(x.reshape(m, d), *ws)


def _sel_map(n_sel, n_cmp_pad, n_cmp):
    j = np.arange(n_sel)[:, None]
    n = np.arange(n_cmp_pad)[None, :]
    m = (n >= SEL_RATIO * j - (CMP_RATIO - 1)) & (n <= SEL_RATIO * j + SEL_RATIO - 1) & (n < n_cmp)
    return jnp.asarray(m, BF16)


def _slope_features(slopes, qb):
    s = slopes * LOG2E
    s1 = s.astype(BF16).astype(F32)
    s2 = (s - s1).astype(BF16).astype(F32)
    s3 = (s - s1 - s2).astype(BF16).astype(F32)
    feat = jnp.stack([64.0 * s1, s1, 64.0 * s2, s2, 64.0 * s3, s3], axis=-1)
    feat = jnp.pad(feat, ((0, 0), (0, A_HEAD_DIM - POS_FEATURES)))
    return jnp.repeat(feat.reshape(A_KV_HEADS, A_GROUP, A_HEAD_DIM), qb, axis=1)


def _with_pos(keys, pos):
    assert int(pos.max()) < MAX_POS
    a, b = pos // 64, pos % 64
    feat = np.zeros((pos.shape[0], A_HEAD_DIM), np.float32)
    feat[:, 0:POS_FEATURES:2] = a[:, None]
    feat[:, 1:POS_FEATURES:2] = b[:, None]
    feat = jnp.broadcast_to(jnp.asarray(feat, BF16), keys.shape)
    return jnp.concatenate([keys.astype(BF16), feat], axis=-1)


def _nsa_prompt_attn(q, gates, ks, vs, kw, vw, k_cmp, v_cmp, slopes, qb=256, key_step=256):
    bsz, t, _ = q.shape
    key_step = min(key_step, t)
    qb = min(qb, key_step)
    k_cmp = _with_pos(k_cmp, np.arange(k_cmp.shape[2]) * CMP_STRIDE + (CMP_BLOCK - 1))
    v_cmp = v_cmp.astype(BF16)
    slope_feat = _slope_features(slopes, qb)
    rows = A_GROUP * qb
    steps = key_step // qb
    outs = []
    for seg in range(t // key_step):
        n_keys = (seg + 1) * key_step
        n_cmp_pad = n_keys // CMP_STRIDE
        n_sel = n_keys // SEL_BLOCK
        expand = jnp.asarray(np.arange(n_keys)[None, :] // SEL_BLOCK == np.arange(n_sel)[:, None], BF16)
        i_base = seg * steps
        head = lambda rows_, width: pl.BlockSpec((None, None, rows_, width), lambda b, g, i: (b, g, 0, 0))
        outs.append(pl.pallas_call(
            functools.partial(_nsa_prompt_kernel, qb=qb, i_base=i_base),
            grid=(bsz, A_KV_HEADS, steps),
            in_specs=[pl.BlockSpec((None, qb, 256), lambda b, g, i, i_base=i_base: (b, i_base + i, g)),
                      pl.BlockSpec((None, None, qb, 12), lambda b, g, i, i_base=i_base: (b, g, i_base + i, 0)),
                      pl.BlockSpec((None, rows, A_HEAD_DIM), lambda b, g, i: (g, 0, 0)),
                      head(n_cmp_pad, 2 * A_HEAD_DIM), head(n_cmp_pad, A_HEAD_DIM),
                      head(n_keys, 2 * A_HEAD_DIM), head(n_keys, A_HEAD_DIM),
                      head(n_keys, 2 * A_HEAD_DIM), head(n_keys, A_HEAD_DIM),
                      pl.BlockSpec((n_sel, n_cmp_pad), lambda b, g, i: (0, 0)),
                      pl.BlockSpec((n_sel, n_keys), lambda b, g, i: (0, 0))],
            out_specs=pl.BlockSpec((None, qb, 256), lambda b, g, i: (b, i, g)),
            out_shape=jax.ShapeDtypeStruct((bsz, key_step, D_MODEL), F32),
            compiler_params=_cparams("parallel", "parallel", "arbitrary"),
            name=f"nsa_prompt_attn_{n_keys}",
        )(q, gates, slope_feat, k_cmp, v_cmp, ks, vs, kw, vw,
          _sel_map(n_sel, n_cmp_pad, t // CMP_STRIDE - 1), expand))
    return jnp.concatenate(outs, axis=1)


def _alibi_slopes():
    return 2.0 ** (-8.0 * jnp.arange(1, A_HEADS + 1, dtype=F32) / A_HEADS)


def _nsa_weights(w_in, pe, w1, w2, w_out):
    cuts = [int(c) for c in np.cumsum([D_MODEL, A_KV_WIDTH, A_KV_WIDTH, A_KV_WIDTH, 3 * A_HEADS])]
    pieces = jnp.split(w_in.astype(BF16), cuts, axis=-1)
    return dict(w_pieces=pieces, cmp=_compress_weights(pe, w1, w2), w_out=w_out.astype(BF16))


def _nsa_layer_prompt(x, wt, ln_g, ln_b):
    bsz, t, d = x.shape
    x2 = x.reshape(bsz * t, d)
    q, kv_c, kv_s, kv_w, z, kvc4, gates, ks, vs, kw, vw = _nsa_proj(x, wt["w_pieces"])
    kv_c3, kv_s3, kv_w3 = (a.reshape(bsz, t, A_KV_WIDTH) for a in (kv_c, kv_s, kv_w))
    k_cmp, v_cmp = _compress_prompt(kvc4, bsz, wt["cmp"])
    o = _nsa_prompt_attn(q.reshape(bsz, t, d), gates, ks, vs, kw, vw, k_cmp, v_cmp, _alibi_slopes())
    x_new = _out_ln(o.reshape(bsz * t, d), z, x2, wt["w_out"], ln_g, ln_b)
    return x_new.reshape(bsz, t, d), kv_c3, kv_s3, kv_w3


def _token_minor(cache):
    nd = cache.ndim
    return cache.transpose(*range(nd - 4), nd - 3, nd - 2, nd - 1, nd - 4)


def _page_group_copies(cache_ref, layer, pt_ref, q, buf_ref, sem_ref):
    slot = lax.rem(q, 2)
    return [pltpu.make_async_copy(cache_ref.at[layer, pt_ref[q * PAGE_GROUP + i]], buf_ref.at[slot, i],
                                  sem_ref.at[slot]) for i in range(PAGE_GROUP)]


def _sample_compress_kernel(pt_ref, cache_ref, pe_ref, wbig_ref, w2_ref, k_ref, v_ref, pg_ref, xt_ref, sem,
                            *, layer, n_pages):
    b = pl.program_id(0)
    n_grp = n_pages // PAGE_GROUP
    n_total = pl.num_programs(0) * n_grp

    @pl.when(b == 0)
    def _():
        for cp in _page_group_copies(cache_ref, layer, pt_ref, 0, pg_ref, sem):
            cp.start()

    for gi in range(n_grp):
        q = b * n_grp + gi

        @pl.when(q + 1 < n_total)
        def _():
            for cp in _page_group_copies(cache_ref, layer, pt_ref, q + 1, pg_ref, sem):
                cp.start()

        for cp in _page_group_copies(cache_ref, layer, pt_ref, q, pg_ref, sem):
            cp.wait()
        slot = lax.rem(q, 2)

        def page_body(i, carry):
            row0 = pl.multiple_of((gi * PAGE_GROUP + i) * PAGE_SIZE, PAGE_SIZE)
            for c in range(2):
                for pair in range(2):
                    tiles = pg_ref[slot, i, c, pl.ds(2 * pair, 2)]
                    xt_ref[c, pair, pl.ds(row0, PAGE_SIZE), :] = tiles.reshape(2 * A_HEAD_DIM, PAGE_SIZE).T
            return carry

        lax.fori_loop(0, PAGE_GROUP, page_body, 0, unroll=PAGE_GROUP // 2)

    n_chunk = n_pages * PAGE_SIZE // CMP_STRIDE
    load_tap = lambda c, pair, s: xt_ref[c, pair, pl.ds(s, n_chunk, stride=CMP_STRIDE), :]
    acc = _chunk_proj(load_tap, wbig_ref)
    _compress_finish(acc, pe_ref, w2_ref, k_ref, v_ref, n_chunk)


def _sample_compress(cache, layer, page_table, cw):
    bsz, n_pages = page_table.shape
    assert n_pages % PAGE_GROUP == 0
    n_tok = n_pages * PAGE_SIZE
    n_chunk = n_tok // CMP_STRIDE
    out = jax.ShapeDtypeStruct((bsz, A_KV_HEADS, n_chunk, A_HEAD_DIM), F32)
    out_spec = pl.BlockSpec((None, A_KV_HEADS, n_chunk, A_HEAD_DIM), lambda b, pt: (b, 0, 0, 0))
    full = lambda a: pl.BlockSpec(a.shape, lambda b, pt: (0,) * a.ndim)
    grid_spec = pltpu.PrefetchScalarGridSpec(
        num_scalar_prefetch=1, grid=(bsz,),
        in_specs=[pl.BlockSpec(memory_space=pl.ANY), full(cw["pe_term"]), full(cw["w_big"]), full(cw["w2"])],
        out_specs=[out_spec, out_spec],
        scratch_shapes=[pltpu.VMEM((2, PAGE_GROUP, 2, A_KV_HEADS, A_HEAD_DIM, PAGE_SIZE), F32),
                        pltpu.VMEM((2, 2, n_tok, 2 * A_HEAD_DIM), F32), pltpu.SemaphoreType.DMA((2,))])
    return pl.pallas_call(
        functools.partial(_sample_compress_kernel, layer=layer, n_pages=n_pages),
        grid_spec=grid_spec, out_shape=[out, out],
        compiler_params=_cparams("arbitrary"),
        name="sample_compress",
    )(page_table.reshape(-1), _token_minor(cache), cw["pe_term"], cw["w_big"], cw["w2"])


def _sample_cmp_kernel(q_ref, slope_ref, kc_ref, vc_ref, map_ref, oc_ref, idx_ref, *, past, n_cand):
    n_pad = kc_ref.shape[1]
    n_cmp = n_pad - 1
    n_idx = lax.broadcasted_iota(jnp.int32, (A_GROUP, n_pad), 1)
    dist = (past - (CMP_BLOCK - 1)) - n_idx * CMP_STRIDE
    valid = (dist >= 0) & (n_idx < n_cmp)
    q = q_ref[...] * (A_HEAD_DIM ** -0.5)
    n_keep = idx_ref.shape[1]
    width = map_ref.shape[1]
    lane = lax.broadcasted_iota(jnp.int32, (1, width), 1)
    row_i = lax.broadcasted_iota(jnp.int32, (width, width), 0)
    col_j = lax.broadcasted_iota(jnp.int32, (width, width), 1)
    for g in range(A_KV_HEADS):
        hs = slice(g * A_GROUP, (g + 1) * A_GROUP)
        s = _dot_nt(q[hs].astype(BF16), kc_ref[g].astype(BF16)) - slope_ref[hs] * dist.astype(F32)
        p = _softmax_rows(s, valid)
        oc_ref[hs, :] = _dot(p.astype(BF16), vc_ref[g].astype(BF16))
        p_grp = jnp.sum(p, axis=0, keepdims=True)
        p_slc = _dot3(p_grp, map_ref[...])
        forced = (lane == 0) | (lane == n_cand - 1)
        score = jnp.where(forced, 1e30, jnp.where(lane < n_cand, p_slc, -1.0))
        s_j = jnp.broadcast_to(score, (width, width))
        s_i = s_j.T
        tie_ahead = jnp.where(row_i < col_j, 1.0, 0.0)
        ahead = jnp.where(s_i > s_j, 1.0, jnp.where(s_i == s_j, tie_ahead, 0.0))
        rank = jnp.sum(ahead, axis=0, keepdims=True)
        want = lax.broadcasted_iota(jnp.int32, (n_keep, width), 0).astype(F32)
        picked = jnp.where(rank == want, lane.astype(F32), 0.0)
        idx_ref[g] = jnp.sum(picked, axis=-1, keepdims=True).astype(jnp.int32)


def _sample_cmp_topk(q, k_cmp, v_cmp, slopes, past):
    bsz = q.shape[0]
    n_pad = k_cmp.shape[2]
    n_cand = past // SEL_BLOCK
    width = -(-n_cand // 128) * 128
    j = np.arange(width)[None, :]
    n = np.arange(n_pad)[:, None]
    sel_map = (n >= SEL_RATIO * j - (CMP_RATIO - 1)) & (n <= SEL_RATIO * j + SEL_RATIO - 1) & (n < n_pad - 1)
    cmp_spec = pl.BlockSpec((None, A_KV_HEADS, n_pad, A_HEAD_DIM), lambda b: (b, 0, 0, 0))
    return pl.pallas_call(
        functools.partial(_sample_cmp_kernel, past=past, n_cand=n_cand),
        grid=(bsz,),
        in_specs=[pl.BlockSpec((None, A_HEADS, A_HEAD_DIM), lambda b: (b, 0, 0)),
                  _full_spec((A_HEADS, 1)), cmp_spec, cmp_spec, _full_spec((n_pad, width))],
        out_specs=[pl.BlockSpec((None, A_HEADS, A_HEAD_DIM), lambda b: (b, 0, 0)),
                   pl.BlockSpec((None, A_KV_HEADS, SEL_TOPK, 1), lambda b: (b, 0, 0, 0))],
        out_shape=[jax.ShapeDtypeStruct((bsz, A_HEADS, A_HEAD_DIM), F32),
                   jax.ShapeDtypeStruct((bsz, A_KV_HEADS, SEL_TOPK, 1), jnp.int32)],
        compiler_params=_cparams("parallel"),
        name="sample_cmp_topk",
    )(q, slopes.reshape(A_HEADS, 1), k_cmp, v_cmp, jnp.asarray(sel_map, BF16))


def _sel_tile_copies(cache_ref, layer, pt_ref, idx_ref, b, g, r, n_pages, buf_ref, sem):
    blk = idx_ref[(b * A_KV_HEADS + g) * SEL_TOPK + r]
    page = pt_ref[b * n_pages + blk // (PAGE_SIZE // SEL_BLOCK)]
    return [pltpu.make_async_copy(cache_ref.at[layer, page, c, g], buf_ref.at[g, r, c], sem) for c in range(2)]


def _sample_attn_kernel(pt_ref, idx_ref, cache_ref, q_ref, gate_ref, slope_ref, oc_ref, ks_new_ref,
                        kw_new_ref, win_ref, o_ref, buf_ref, sem, *, layer, n_pages, past):
    b = pl.program_id(0)
    n_blk = SEL_TOPK - 1
    for g in range(A_KV_HEADS):
        for r in range(n_blk):
            for cp in _sel_tile_copies(cache_ref, layer, pt_ref, idx_ref, b, g, r, n_pages, buf_ref, sem):
                cp.start()

    q = q_ref[...] * (A_HEAD_DIM ** -0.5)
    gate = jax.nn.sigmoid(gate_ref[...])
    n_buf = win_ref.shape[-1]
    tok = lax.broadcasted_iota(jnp.int32, (1, PAGE_SIZE), 1)
    dist_w = (n_buf - lax.broadcasted_iota(jnp.int32, (1, n_buf), 1)).astype(F32)

    def attend(qg, slope, keys_t, vals_t, dist, valid, k_new, v_new):
        s = _dot(qg.astype(BF16), keys_t.astype(BF16)) - slope * dist
        if valid is not None:
            s = jnp.where(valid, s, NEG_BIG)
        s_new = jnp.sum(qg * k_new, axis=-1, keepdims=True)
        m = jnp.maximum(jnp.max(s, axis=-1, keepdims=True), s_new)
        e = jnp.exp(s - m)
        e_new = jnp.exp(s_new - m)
        l = jnp.sum(e, axis=-1, keepdims=True) + e_new
        return (_dot_nt(e.astype(BF16), vals_t.astype(BF16)) + e_new * v_new) / l

    def new_row(ref, g):
        ksl = slice(g * A_HEAD_DIM, (g + 1) * A_HEAD_DIM)
        vsl = slice(A_KV_WIDTH // 2 + g * A_HEAD_DIM, A_KV_WIDTH // 2 + (g + 1) * A_HEAD_DIM)
        return ref[:, ksl], ref[:, vsl]

    out_w = []
    for g in range(A_KV_HEADS):
        hs = slice(g * A_GROUP, (g + 1) * A_GROUP)
        out_w.append(attend(q[hs], slope_ref[hs], win_ref[0, g], win_ref[1, g], dist_w, None,
                            *new_row(kw_new_ref, g)))

    for g in range(A_KV_HEADS):
        for r in range(n_blk):
            for cp in _sel_tile_copies(cache_ref, layer, pt_ref, idx_ref, b, g, r, n_pages, buf_ref, sem):
                cp.wait()

    blocks_per_page = PAGE_SIZE // SEL_BLOCK
    for g in range(A_KV_HEADS):
        hs = slice(g * A_GROUP, (g + 1) * A_GROUP)
        keys_t = jnp.concatenate([buf_ref[g, r, 0] for r in range(n_blk)], axis=1)
        vals_t = jnp.concatenate([buf_ref[g, r, 1] for r in range(n_blk)], axis=1)
        dist, valid = [], []
        for r in range(n_blk):
            blk = idx_ref[(b * A_KV_HEADS + g) * SEL_TOPK + r]
            dist.append(past - ((blk // blocks_per_page) * PAGE_SIZE + tok))
            valid.append(tok // SEL_BLOCK == blk % blocks_per_page)
        dist = jnp.concatenate(dist, axis=1).astype(F32)
        valid = jnp.concatenate(valid, axis=1)
        o_s = attend(q[hs], slope_ref[hs], keys_t, vals_t, dist, valid, *new_row(ks_new_ref, g))
        gt = gate[hs]
        o_ref[hs, :] = gt[:, 0:1] * oc_ref[hs, :] + gt[:, 1:2] * o_s + gt[:, 2:3] * out_w[g]


def _sample_attn(page_table, idx, cache_sel, layer, q, gate, slopes, o_c, kv_s_new, kv_w_new, cache_win, past):
    bsz, n_pages = page_table.shape
    n_buf = cache_win.shape[2]
    head_spec = pl.BlockSpec((None, A_HEADS, A_HEAD_DIM), lambda b, pt, ix: (b, 0, 0))
    new_spec = pl.BlockSpec((None, 1, A_KV_WIDTH), lambda b, pt, ix: (b, 0, 0))
    grid_spec = pltpu.PrefetchScalarGridSpec(
        num_scalar_prefetch=2, grid=(bsz,),
        in_specs=[pl.BlockSpec(memory_space=pl.ANY), head_spec,
                  pl.BlockSpec((None, A_HEADS, 3), lambda b, pt, ix: (b, 0, 0)),
                  pl.BlockSpec((A_HEADS, 1), lambda b, pt, ix: (0, 0)),
                  head_spec, new_spec, new_spec,
                  pl.BlockSpec((None, None, 2, A_KV_HEADS, A_HEAD_DIM, n_buf),
                               lambda b, pt, ix: (layer, b, 0, 0, 0, 0))],
        out_specs=head_spec,
        scratch_shapes=[pltpu.VMEM((A_KV_HEADS, SEL_TOPK - 1, 2, A_HEAD_DIM, PAGE_SIZE), F32),
                        pltpu.SemaphoreType.DMA(())])
    return pl.pallas_call(
        functools.partial(_sample_attn_kernel, layer=layer, n_pages=n_pages, past=past),
        grid_spec=grid_spec,
        out_shape=jax.ShapeDtypeStruct((bsz, A_HEADS, A_HEAD_DIM), F32),
        compiler_params=_cparams("arbitrary"),
        name="sample_attn",
    )(page_table.reshape(-1), idx.reshape(-1), _token_minor(cache_sel), q, gate, slopes.reshape(A_HEADS, 1),
      o_c, kv_s_new, kv_w_new, _token_minor(cache_win))


def _nsa_layer_sample(x, wt, ln_g, ln_b, layer, cache_cmp, cache_sel, cache_win, page_table):
    bsz, _, d = x.shape
    n_pages = page_table.shape[1]
    past = n_pages * PAGE_SIZE
    x2 = x.reshape(bsz, d)
    q, kv_c, kv_s, kv_w, gate, z = _multi_proj(x2, wt["w_pieces"])
    slopes = _alibi_slopes()
    k_cmp, v_cmp = _sample_compress(cache_cmp, layer, page_table, wt["cmp"])
    q3 = q.reshape(bsz, A_HEADS, A_HEAD_DIM)
    o_c, idx = _sample_cmp_topk(q3, k_cmp, v_cmp, slopes, past)
    gate3 = gate.reshape(bsz, 3, A_HEADS).transpose(0, 2, 1)
    kv_s3, kv_w3 = kv_s.reshape(bsz, 1, A_KV_WIDTH), kv_w.reshape(bsz, 1, A_KV_WIDTH)
    o = _sample_attn(page_table, idx, cache_sel, layer, q3, gate3, slopes, o_c, kv_s3, kv_w3, cache_win, past)
    x_new = _out_ln(o.reshape(bsz, d), z, x2, wt["w_out"], ln_g, ln_b)
    n_buf = cache_win.shape[2]
    n_keep = min(WINDOW, n_buf + 1)
    win_old = cache_win[layer, :, n_buf + 1 - n_keep:].reshape(bsz, n_keep - 1, A_KV_WIDTH)
    win_new = jnp.concatenate([win_old, kv_w3], axis=1)
    return x_new.reshape(bsz, 1, d), kv_c.reshape(bsz, 1, A_KV_WIDTH), kv_s3, win_new


def _s5_param_kernel(log_dt_ref, ar_ref, ai_ref, br_ref, bi_ref, tile_ref,
                     abr_ref, abi_ref, bbr_ref, bbi_ref):
    dt = jnp.exp(log_dt_ref[...])
    ar, ai = ar_ref[...], ai_ref[...]
    mag = jnp.exp(dt * ar)
    abr, abi = mag * jnp.cos(dt * ai), mag * jnp.sin(dt * ai)
    den = ar * ar + ai * ai
    num_re, num_im = abr - 1.0, abi
    zoh_re = (num_re * ar + num_im * ai) / den
    zoh_im = (num_im * ar - num_re * ai) / den
    abr_ref[...] = abr
    abi_ref[...] = abi
    zr = _dot3(zoh_re, tile_ref[...])
    zi = _dot3(zoh_im, tile_ref[...])
    br, bi = br_ref[...], bi_ref[...]
    bbr_ref[...] = zr * br - zi * bi
    bbi_ref[...] = zr * bi + zi * br


def _s5_weights(w_in, log_dt, a_re, a_im, b_re, b_im, c_re, c_im, d_skip, w_glu, b_glu, w_out):
    g, p, c = S5_GROUPS, S5_STATE, S5_GROUP_CH
    tile = jnp.asarray(np.tile(np.eye(p, dtype=np.float32), (1, c)), BF16)
    to_gcp = lambda b: b.transpose(0, 2, 1).reshape(g, c * p)
    shapes = [jax.ShapeDtypeStruct((g, p), F32)] * 2 + [jax.ShapeDtypeStruct((g, c * p), F32)] * 2
    abr, abi, bbr, bbi = pl.pallas_call(_s5_param_kernel, out_shape=shapes, name="s5_params")(
        log_dt.reshape(g, 1), a_re, a_im, to_gcp(b_re), to_gcp(b_im), tile)
    eye = jnp.eye(S5_GROUPS // S5_SETS, dtype=F32)
    gl = S5_GROUPS // S5_SETS

    def in_blockdiag(bb):
        bb = bb.reshape(S5_SETS, gl, c, p)
        return jnp.einsum('sgcp,gh->sgchp', bb, eye).reshape(S5_SETS, gl * c, gl * p).astype(BF16)

    def out_blockdiag(cc):
        cc = cc.reshape(S5_SETS, gl, c, p)
        return jnp.einsum('sgcp,gh->sgphc', cc, eye).reshape(S5_SETS, gl * p, gl * c).astype(BF16)

    w_u, w_z = jnp.split(w_in.astype(BF16), 2, axis=-1)
    return dict(w_pieces=[w_u, w_z], abr=abr.reshape(1, g * p), abi=abi.reshape(1, g * p),
                wb_re=in_blockdiag(bbr), wb_im=in_blockdiag(bbi),
                wc_re=out_blockdiag(c_re), wc_im=out_blockdiag(c_im),
                d_skip=d_skip.reshape(1, -1), w_glu=w_glu.astype(BF16), b_glu=b_glu.reshape(1, -1),
                w_out=w_out.astype(BF16))


def _cmul(ar, ai, br, bi):
    return ar * br - ai * bi, ar * bi + ai * br


def _s5_row_scan(bur, bui, ar_ref, ai_ref, hr_c, hi_c, tc):
    n_set = S5_GROUPS * S5_STATE // S5_SETS
    for s in range(S5_SETS):
        sl = pl.ds(s * n_set, n_set)
        ar, ai = ar_ref[:, sl], ai_ref[:, sl]

        def step(t, carry):
            hr, hi = carry
            pr, pi = _cmul(ar, ai, hr, hi)
            nr, ni = pr + bur[pl.ds(t, 1), sl], pi + bui[pl.ds(t, 1), sl]
            bur[pl.ds(t, 1), sl] = nr
            bui[pl.ds(t, 1), sl] = ni
            return nr, ni

        hr, hi = lax.fori_loop(0, tc, step, (hr_c[:, sl], hi_c[:, sl]))
        hr_c[:, sl] = hr
        hi_c[:, sl] = hi


def _s5_tile_scan(bur, bui, ar_ref, ai_ref, hr_c, hi_c, tc):
    width = SCAN_LANES
    row = lax.broadcasted_iota(jnp.int32, (SCAN_TILE, width), 0)
    for s in range(S5_GROUPS * S5_STATE // width):
        sl = pl.ds(s * width, width)
        ar, ai = ar_ref[:, sl], ai_ref[:, sl]
        pows = [(ar, ai)]
        for _ in range(SCAN_TILE - 1):
            pows.append(_cmul(*pows[-1], ar, ai))
        pw_r = jnp.concatenate([p[0] for p in pows], axis=0)
        pw_i = jnp.concatenate([p[1] for p in pows], axis=0)
        steps = []
        for k in (1, 2, 4):
            keep = row >= k
            steps.append((k, jnp.where(keep, pows[k - 1][0], 0.0), jnp.where(keep, pows[k - 1][1], 0.0)))

        def tile(i, carry):
            cr, ci = carry
            rows = pl.ds(pl.multiple_of(i * SCAN_TILE, SCAN_TILE), SCAN_TILE)
            yr, yi = bur[rows, sl], bui[rows, sl]
            for k, kr, ki in steps:
                pr, pi = _cmul(kr, ki, pltpu.roll(yr, k, 0), pltpu.roll(yi, k, 0))
                yr, yi = yr + pr, yi + pi
            pr, pi = _cmul(pw_r, pw_i, cr, ci)
            yr, yi = yr + pr, yi + pi
            bur[rows, sl] = yr
            bui[rows, sl] = yi
            return yr[SCAN_TILE - 1:, :], yi[SCAN_TILE - 1:, :]

        hr, hi = lax.fori_loop(0, tc // SCAN_TILE, tile, (hr_c[:, sl], hi_c[:, sl]), unroll=4)
        hr_c[:, sl] = hr
        hi_c[:, sl] = hi


def _s5_scan_kernel(u_ref, wbr_ref, wbi_ref, ar_ref, ai_ref, wcr_ref, wci_ref, h0r_ref, h0i_ref,
                    y_ref, hr_out, hi_out, bur, bui, hr_c, hi_c, *, tc):
    @pl.when(pl.program_id(1) == 0)
    def _():
        hr_c[...] = h0r_ref[...]
        hi_c[...] = h0i_ref[...]

    n_set = S5_GROUPS * S5_STATE // S5_SETS
    n_ch = S5_GROUPS * S5_GROUP_CH // S5_SETS
    ub = u_ref[...].astype(BF16)
    for s in range(S5_SETS):
        us = ub[:, s * n_ch:(s + 1) * n_ch]
        bur[:, s * n_set:(s + 1) * n_set] = _dot(us, wbr_ref[s])
        bui[:, s * n_set:(s + 1) * n_set] = _dot(us, wbi_ref[s])

    if tc % SCAN_TILE == 0:
        _s5_tile_scan(bur, bui, ar_ref, ai_ref, hr_c, hi_c, tc)
    else:
        _s5_row_scan(bur, bui, ar_ref, ai_ref, hr_c, hi_c, tc)

    for s in range(S5_SETS):
        h_re = bur[:, s * n_set:(s + 1) * n_set].astype(BF16)
        h_im = bui[:, s * n_set:(s + 1) * n_set].astype(BF16)
        y_ref[:, s * n_ch:(s + 1) * n_ch] = _dot(h_re, wcr_ref[s]) - _dot(h_im, wci_ref[s])
    hr_out[...] = hr_c[...]
    hi_out[...] = hi_c[...]


def _s5_scan(u, h0_re, h0_im, wt, tc=256):
    bsz, t, d = u.shape
    tc = min(tc, t)
    n_state = S5_GROUPS * S5_STATE
    st_spec = pl.BlockSpec((None, 1, n_state), lambda b, j: (b, 0, 0))
    full = lambda a: pl.BlockSpec(a.shape, lambda b, j: (0,) * a.ndim)
    st_shape = jax.ShapeDtypeStruct((bsz, 1, n_state), F32)
    return pl.pallas_call(
        functools.partial(_s5_scan_kernel, tc=tc),
        grid=(bsz, t // tc),
        in_specs=[pl.BlockSpec((None, tc, d), lambda b, j: (b, j, 0)),
                  full(wt["wb_re"]), full(wt["wb_im"]), full(wt["abr"]), full(wt["abi"]),
                  full(wt["wc_re"]), full(wt["wc_im"]), st_spec, st_spec],
        out_specs=[pl.BlockSpec((None, tc, d), lambda b, j: (b, j, 0)), st_spec, st_spec],
        out_shape=[jax.ShapeDtypeStruct((bsz, t, d), F32), st_shape, st_shape],
        scratch_shapes=[pltpu.VMEM((tc, n_state), F32), pltpu.VMEM((tc, n_state), F32),
                        pltpu.VMEM((1, n_state), F32), pltpu.VMEM((1, n_state), F32)],
        compiler_params=_cparams("parallel", "arbitrary"),
        name="s5_scan",
    )(u, wt["wb_re"], wt["wb_im"], wt["abr"], wt["abi"], wt["wc_re"], wt["wc_im"], h0_re, h0_im)


def _s5_out_kernel(y_ref, u_ref, z_ref, x_ref, d_ref, wg_ref, bg_ref, w_ref, g_ref, b_ref, out_ref):
    y = jax.nn.gelu(y_ref[...] + d_ref[...] * u_ref[...])
    y = y * jax.nn.sigmoid(_dot(y.astype(BF16), wg_ref[...]) + bg_ref[...])
    out = _dot((y * _silu(z_ref[...])).astype(BF16), w_ref[...])
    out_ref[...] = _deepnorm(x_ref[...], out, g_ref[...], b_ref[...])


def _s5_out(y, u, z, x, wt, g, b, tm=256):
    m, d = x.shape
    tm = min(tm, m)
    vec = _full_spec((1, d))
    return pl.pallas_call(
        _s5_out_kernel,
        grid=(m // tm,),
        in_specs=[_row_spec(tm, d)] * 4 + [vec, _full_spec((d, d)), vec, _full_spec((d, d)), vec, vec],
        out_specs=_row_spec(tm, d),
        out_shape=jax.ShapeDtypeStruct((m, d), F32),
        compiler_params=_cparams("parallel"),
        name="s5_out",
    )(y, u, z, x, wt["d_skip"], wt["w_glu"], wt["b_glu"], wt["w_out"], g.reshape(1, d), b.reshape(1, d))


def _s5_layer(x, h0_re, h0_im, wt, ln_g, ln_b):
    bsz, t, d = x.shape
    x2 = x.reshape(bsz * t, d)
    u, z = _multi_proj(x2, wt["w_pieces"])
    n_state = S5_GROUPS * S5_STATE
    y, hr, hi = _s5_scan(u.reshape(bsz, t, d), h0_re.reshape(bsz, 1, n_state),
                         h0_im.reshape(bsz, 1, n_state), wt)
    x_new = _s5_out(y.reshape(bsz * t, d), u, z, x2, wt, ln_g, ln_b)
    st = lambda h: h.reshape(bsz, S5_GROUPS, S5_STATE)
    return x_new.reshape(bsz, t, d), st(hr), st(hi)


def _head_sum(x, seg):
    w = seg.shape[0]
    return jnp.concatenate([_dot3(x[:, j:j + w], seg) for j in range(0, x.shape[1], w)], axis=1)


def _softplus(x):
    return jnp.maximum(x, 0.0) + jnp.log(1.0 + jnp.exp(-jnp.abs(x)))


def _rwkv_proj_kernel(x_ref, xp_ref, mu_ref, wr_ref, wwl_ref, wk_ref, wv_ref, wal_ref, wz_ref,
                      w0_ref, w2_ref, a0_ref, a2_ref, kk_ref, ka_ref, seg_ref,
                      r_ref, lw_ref, k_ref, v_ref, kkn_ref, a_ref, z_ref):
    x = x_ref[...]
    dx = xp_ref[...] - x
    mix = lambda m: (x + dx * mu_ref[m:m + 1, :]).astype(BF16)
    r_ref[...] = _dot(mix(0), wr_ref[...])
    w_lo = _dot(mix(1), wwl_ref[...])
    k = _dot(mix(2), wk_ref[...])
    v_ref[...] = _dot(mix(3), wv_ref[...])
    a_lo = _dot(mix(4), wal_ref[...])
    z_ref[...] = _dot(mix(5), wz_ref[...])
    w_log = -_softplus(-(w0_ref[...] + _dot(jnp.tanh(w_lo).astype(BF16), w2_ref[...]))) - 0.5
    lw_ref[...] = -jnp.exp(w_log)
    a = jax.nn.sigmoid(a0_ref[...] + _dot(a_lo.astype(BF16), a2_ref[...]))
    a_ref[...] = a
    kk = k * kk_ref[...]
    norm = jnp.sqrt(_head_sum(kk * kk, seg_ref[...]))
    kkn_ref[...] = kk / jnp.maximum(norm, 1e-12)
    k_ref[...] = k * (1.0 + (a - 1.0) * ka_ref[...])


def _head_seg(width=4 * C_HEAD_DIM):
    lane = np.arange(width) // C_HEAD_DIM
    return jnp.asarray(lane[:, None] == lane[None, :], BF16)


def _rwkv_weights(mu, w_in, w0, w2, a0, a2, k_k, k_a, r_k, lnx_g, lnx_b, w_out):
    d = D_MODEL
    lora_w, lora_a = w2.shape[0], a2.shape[0]
    cuts = [int(c) for c in np.cumsum([d, lora_w, d, d, lora_a])]
    pieces = jnp.split(w_in.astype(BF16), cuts, axis=-1)
    row = lambda a: a.reshape(1, d)
    return dict(mu=mu, pieces=pieces, w0=row(w0), w2=w2.astype(BF16), a0=row(a0), a2=a2.astype(BF16),
                k_k=row(k_k), k_a=row(k_a), r_k=row(r_k), lnx_g=row(lnx_g), lnx_b=row(lnx_b),
                w_out=w_out.astype(BF16), seg=_head_seg())


def _rwkv_proj(x, x_prev, wt, tm=256):
    m, d = x.shape
    tm = min(tm, m)
    ins = [x, x_prev, wt["mu"], *wt["pieces"], wt["w0"], wt["w2"], wt["a0"], wt["a2"],
           wt["k_k"], wt["k_a"], wt["seg"]]
    in_specs = [_row_spec(tm, d), _row_spec(tm, d)] + [_full_spec(a.shape) for a in ins[2:]]
    return pl.pallas_call(
        _rwkv_proj_kernel,
        grid=(m // tm,),
        in_specs=in_specs,
        out_specs=[_row_spec(tm, d)] * 7,
        out_shape=[jax.ShapeDtypeStruct((m, d), F32)] * 7,
        compiler_params=_cparams("parallel"),
        name="rwkv_proj",
    )(*ins)


def _dot_tn(a, b):
    return lax.dot_general(a, b, (((0,), (0,)), ((), ())), preferred_element_type=F32)


def _mm(a, w, dot=_dot):
    return dot(a.astype(BF16), w)


def _wkv_chunk_kernel(r_ref, lw_ref, k_ref, v_ref, kkn_ref, a_ref, y_ref, s_out, s_ref, *, chunk):
    n = chunk
    gw = 4 * C_HEAD_DIM
    n_grp = C_HEADS // 4

    @pl.when(pl.program_id(1) == 0)
    def _():
        s_ref[...] = jnp.zeros(s_ref.shape, F32)

    lw = lw_ref[...]
    row = lax.broadcasted_iota(jnp.int32, (n, n), 0)
    col = lax.broadcasted_iota(jnp.int32, (n, n), 1)
    cum = _dot_hi(jnp.where(row >= col, 1.0, 0.0), lw)
    tot = cum[n - 1:n, :]
    kkn = kkn_ref[...]
    b_vec = kkn * a_ref[...]
    e_neg = jnp.exp(-cum)
    e_tail = jnp.exp(tot - cum)
    a_t = -kkn * jnp.exp(cum - lw)
    r_t = r_ref[...] * jnp.exp(cum)
    k = k_ref[...]
    b_t, k_t = b_vec * e_neg, k * e_neg
    b_h, k_h = b_vec * e_tail, k * e_tail
    p_tot = jnp.exp(tot)
    v = v_ref[...]

    t_idx = lax.broadcasted_iota(jnp.int32, (n, gw), 0)
    s_idx = lax.rem(lax.broadcasted_iota(jnp.int32, (n, gw), 1), C_HEAD_DIM)
    strict = jnp.where(s_idx < t_idx, 1.0, 0.0)
    incl = jnp.where(s_idx <= t_idx, 1.0, 0.0)
    eye = jnp.where(s_idx == t_idx, 1.0, 0.0)
    same_head = (lax.broadcasted_iota(jnp.int32, (gw, gw), 0) // C_HEAD_DIM
                 == lax.broadcasted_iota(jnp.int32, (gw, gw), 1) // C_HEAD_DIM)
    head_mask = jnp.where(same_head, 1.0, 0.0)
    head_mask_bf = head_mask.astype(BF16)

    def blockdiag(x):
        return jnp.concatenate([x.astype(BF16)] * 4, axis=0) * head_mask_bf

    grp = range(n_grp)
    sls = [slice(g * gw, (g + 1) * gw) for g in grp]
    ar = [jnp.concatenate([a_t[:, sl], r_t[:, sl]], axis=0) for sl in sls]
    ab = [_mm(ar[g], blockdiag(b_t[:, sls[g]]), _dot_nt) for g in grp]
    ak = [_mm(ar[g], blockdiag(k_t[:, sls[g]]), _dot_nt) for g in grp]
    s0 = [s_ref[g] for g in grp]
    xs = [_mm(ar[g], s0[g].astype(BF16), _dot_nt) for g in grp]
    v_bd = [blockdiag(v[:, sl]) for sl in sls]
    a_ab = [ab[g][:n] * strict for g in grp]
    inv = [eye + a_ab[g] for g in grp]
    pw = a_ab
    for _ in range(int(math.log2(n)) - 1):
        pw = [_mm(pw[g], blockdiag(pw[g])) for g in grp]
        inv = [inv[g] + _mm(inv[g], blockdiag(pw[g])) for g in grp]
    x = [xs[g][:n] + _mm(ak[g][:n] * strict, v_bd[g]) for g in grp]
    u = [_mm(inv[g], blockdiag(x[g])) for g in grp]
    for g in grp:
        y_ref[:, sls[g]] = (xs[g][n:] + _mm(ab[g][n:] * incl, blockdiag(u[g]))
                            + _mm(ak[g][n:] * incl, v_bd[g]))
    for g in grp:
        uv = jnp.concatenate([u[g], v[:, sls[g]]], axis=0)
        bk = jnp.concatenate([b_h[:, sls[g]], k_h[:, sls[g]]], axis=0)
        s_ref[g] = s0[g] * p_tot[:, sls[g]] + _mm(uv, bk.astype(BF16), _dot_tn) * head_mask

    @pl.when(pl.program_id(1) == pl.num_programs(1) - 1)
    def _():
        for h in range(C_HEADS):
            g, hl = divmod(h, 4)
            blk = slice(hl * C_HEAD_DIM, (hl + 1) * C_HEAD_DIM)
            s_out[h] = s_ref[g, blk, blk]


def _wkv_chunked(r, lw, k, v, kkn, a, chunk=WKV_CHUNK):
    bsz, t, d = r.shape
    spec = pl.BlockSpec((None, chunk, d), lambda b, j: (b, j, 0))
    s_spec = pl.BlockSpec((None, C_HEADS, C_HEAD_DIM, C_HEAD_DIM), lambda b, j: (b, 0, 0, 0))
    return pl.pallas_call(
        functools.partial(_wkv_chunk_kernel, chunk=chunk),
        grid=(bsz, t // chunk),
        in_specs=[spec] * 6,
        out_specs=[spec, s_spec],
        out_shape=[jax.ShapeDtypeStruct((bsz, t, d), F32),
                   jax.ShapeDtypeStruct((bsz, C_HEADS, C_HEAD_DIM, C_HEAD_DIM), F32)],
        scratch_shapes=[pltpu.VMEM((C_HEADS // 4, 4 * C_HEAD_DIM, 4 * C_HEAD_DIM), F32)],
        compiler_params=_cparams("parallel", "arbitrary"),
        name="wkv_chunked",
    )(r, lw, k, v, kkn, a)


def _wkv_step_kernel(s_ref, r_ref, lw_ref, k_ref, kkn_ref, a_ref, v_ref, y_ref, s_out):
    s0 = s_ref[...]
    kkn = kkn_ref[...]
    sa = jnp.sum(s0 * (-kkn), axis=-1, keepdims=True)
    s1 = s0 * jnp.exp(lw_ref[...]) + sa * (kkn * a_ref[...]) + v_ref[...] * k_ref[...]
    s_out[...] = s1
    y_ref[...] = jnp.sum(s1 * r_ref[...], axis=-1, keepdims=True)


def _wkv_step(s0, r, lw, k, v, kkn, a):
    bsz = s0.shape[0]
    rowv = lambda x: x.reshape(bsz, C_HEADS, 1, C_HEAD_DIM)
    colv = lambda x: x.reshape(bsz, C_HEADS, C_HEAD_DIM, 1)
    s_spec = pl.BlockSpec((None, C_HEADS, C_HEAD_DIM, C_HEAD_DIM), lambda b: (b, 0, 0, 0))
    r_spec = pl.BlockSpec((None, C_HEADS, 1, C_HEAD_DIM), lambda b: (b, 0, 0, 0))
    c_spec = pl.BlockSpec((None, C_HEADS, C_HEAD_DIM, 1), lambda b: (b, 0, 0, 0))
    y, s1 = pl.pallas_call(
        _wkv_step_kernel,
        grid=(bsz,),
        in_specs=[s_spec] + [r_spec] * 5 + [c_spec],
        out_specs=[c_spec, s_spec],
        out_shape=[jax.ShapeDtypeStruct((bsz, C_HEADS, C_HEAD_DIM, 1), F32),
                   jax.ShapeDtypeStruct(s0.shape, F32)],
        compiler_params=_cparams("parallel"),
        name="wkv_step",
    )(s0, rowv(r), rowv(lw), rowv(k), rowv(kkn), rowv(a), colv(v))
    return y.reshape(bsz, D_MODEL), s1


def _rwkv_out_kernel(y_ref, r_ref, k_ref, v_ref, z_ref, x_ref, rk_ref, lg_ref, lb_ref, seg_ref,
                     w_ref, g_ref, b_ref, out_ref):
    seg = seg_ref[...]
    y = y_ref[...]
    inv_n = 1.0 / C_HEAD_DIM
    yc = y - _head_sum(y, seg) * inv_n
    var = _head_sum(yc * yc, seg) * inv_n
    yn = yc * lax.rsqrt(var + C_LN_EPS) * lg_ref[...] + lb_ref[...]
    v = v_ref[...]
    yy = yn + _head_sum(r_ref[...] * k_ref[...] * rk_ref[...], seg) * v
    out = _dot((yy * _silu(z_ref[...])).astype(BF16), w_ref[...])
    out_ref[...] = _deepnorm(x_ref[...], out, g_ref[...], b_ref[...])


def _rwkv_out(y, r, k, v, z, x, wt, g, b, tm=256):
    m, d = x.shape
    tm = min(tm, m)
    vec = _full_spec((1, d))
    return pl.pallas_call(
        _rwkv_out_kernel,
        grid=(m // tm,),
        in_specs=[_row_spec(tm, d)] * 6 + [vec, vec, vec, _full_spec(wt["seg"].shape),
                                           _full_spec((d, d)), vec, vec],
        out_specs=_row_spec(tm, d),
        out_shape=jax.ShapeDtypeStruct((m, d), F32),
        compiler_params=_cparams("parallel"),
        name="rwkv_out",
    )(y, r, k, v, z, x, wt["r_k"], wt["lnx_g"], wt["lnx_b"], wt["seg"], wt["w_out"],
      g.reshape(1, d), b.reshape(1, d))


def _rwkv_layer(x, s0, x_last, wt, ln_g, ln_b):
    bsz, t, d = x.shape
    x_prev = jnp.concatenate([x_last[:, None, :], x[:, :-1]], axis=1)
    x2 = x.reshape(bsz * t, d)
    r, lw, k, v, kkn, a, z = _rwkv_proj(x2, x_prev.reshape(bsz * t, d), wt)
    if s0 is None:
        b3 = lambda u: u.reshape(bsz, t, d)
        y, s1 = _wkv_chunked(b3(r), b3(lw), b3(k), b3(v), b3(kkn), b3(a))
        y = y.reshape(bsz * t, d)
    else:
        y, s1 = _wkv_step(s0, r, lw, k, v, kkn, a)
    x_new = _rwkv_out(y, r, k, v, z, x2, wt, ln_g, ln_b)
    return x_new.reshape(bsz, t, d), s1, x[:, -1]


def _run_trunk(x, ln_g, ln_b, a_wts, b_wts, c_wts, past=None):
    bsz = x.shape[0]
    new = {k: [] for k in ("cmp", "sel", "win", "re", "im", "wkv", "shift")}
    for i in range(DEPTH):
        j, kind = divmod(i, N_MIXERS)
        if kind == 0:
            if past is None:
                x, kc, ks, kw = _nsa_layer_prompt(x, a_wts[j], ln_g[i], ln_b[i])
                kw = kw[:, kw.shape[1] - min(WINDOW, kw.shape[1]):]
            else:
                x, kc, ks, kw = _nsa_layer_sample(x, a_wts[j], ln_g[i], ln_b[i], j, past["cmp"],
                                                  past["sel"], past["win"], past["page_table"])
            new["cmp"].append(kc)
            new["sel"].append(ks)
            new["win"].append(kw)
        elif kind == 1:
            if past is None:
                h0r = h0i = jnp.zeros((bsz, S5_GROUPS, S5_STATE), F32)
            else:
                h0r, h0i = past["s5_re"][j], past["s5_im"][j]
            x, hr, hi = _s5_layer(x, h0r, h0i, b_wts[j], ln_g[i], ln_b[i])
            new["re"].append(hr)
            new["im"].append(hi)
        else:
            if past is None:
                s0, x_last = None, jnp.zeros((bsz, D_MODEL), F32)
            else:
                s0, x_last = past["wkv"][j], past["shift"][j]
            x, s1, xl = _rwkv_layer(x, s0, x_last, c_wts[j], ln_g[i], ln_b[i])
            new["wkv"].append(s1)
            new["shift"].append(xl)
    kv_tail = (2, A_KV_HEADS, A_HEAD_DIM)
    kv = lambda rows: jnp.stack(rows).reshape((len(rows),) + rows[0].shape[:2] + kv_tail)
    return (x, kv(new["cmp"]), kv(new["sel"]), kv(new["win"]), jnp.stack(new["re"]), jnp.stack(new["im"]),
            jnp.stack(new["wkv"]), jnp.stack(new["shift"]))


def kernel(x_prompt, x_sample, cache_cmp_kv, cache_sel_kv, cache_win_kv, state_s5_re, state_s5_im,
           state_wkv, state_shift, page_table, ln_g, ln_b,
           a_w_in, a_cmp_pe, a_cmp_w1, a_cmp_w2, a_w_out,
           b_w_in, b_log_dt, b_a_re, b_a_im, b_b_re, b_b_im, b_c_re, b_c_im, b_d, b_w_glu, b_b_glu, b_w_out,
           c_mu, c_w_in, c_w0, c_w2, c_a0, c_a2, c_k_k, c_k_a, c_r_k, c_lnx_g, c_lnx_b, c_w_out):
    a_par = (a_w_in, a_cmp_pe, a_cmp_w1, a_cmp_w2, a_w_out)
    b_par = (b_w_in, b_log_dt, b_a_re, b_a_im, b_b_re, b_b_im, b_c_re, b_c_im, b_d, b_w_glu, b_b_glu, b_w_out)
    c_par = (c_mu, c_w_in, c_w0, c_w2, c_a0, c_a2, c_k_k, c_k_a, c_r_k, c_lnx_g, c_lnx_b, c_w_out)
    a_wts = [_nsa_weights(*[p[j] for p in a_par]) for j in range(a_w_in.shape[0])]
    b_wts = [_s5_weights(*[p[j] for p in b_par]) for j in range(b_w_in.shape[0])]
    c_wts = [_rwkv_weights(*[p[j] for p in c_par]) for j in range(c_w_in.shape[0])]

    (y_p, p_cmp, p_sel, p_win, p_re, p_im, p_wkv, p_shift) = _run_trunk(
        x_prompt, ln_g, ln_b, a_wts, b_wts, c_wts)
    past = dict(cmp=cache_cmp_kv, sel=cache_sel_kv, win=cache_win_kv, page_table=page_table,
                s5_re=state_s5_re, s5_im=state_s5_im, wkv=state_wkv, shift=state_shift)
    (y_s, s_cmp, s_sel, s_win, s_re, s_im, s_wkv, s_shift) = _run_trunk(
        x_sample, ln_g, ln_b, a_wts, b_wts, c_wts, past)
    return (y_p, y_s, p_cmp, s_cmp, p_sel, s_sel, p_win, s_win,
            p_re, s_re, p_im, s_im, p_wkv, s_wkv, p_shift, s_shift)
```

```python
import functools
import math

import numpy as np
import jax
import jax.numpy as jnp
from jax import lax
from jax.experimental import pallas as pl
from jax.experimental.pallas import tpu as pltpu

F32 = jnp.float32
BF16 = jnp.bfloat16
HIGHEST = lax.Precision.HIGHEST

D_MODEL = 1024
DEPTH = 4
N_MIXERS = 3
PAGE_SIZE = 128
PAGE_GROUP = 16

A_HEADS = 16
A_KV_HEADS = 4
A_HEAD_DIM = 64
A_GROUP = 4
CMP_BLOCK = 32
CMP_STRIDE = 16
CMP_RATIO = 2
CMP_HIDDEN = 128
SEL_BLOCK = 64
SEL_RATIO = 4
SEL_TOPK = 16
WINDOW = 512
A_KV_WIDTH = 2 * A_KV_HEADS * A_HEAD_DIM

S5_GROUP_CH = 16
S5_GROUPS = 64
S5_STATE = 64
S5_SETS = 4
SCAN_TILE = 8
SCAN_LANES = 512

C_HEAD_DIM = 64
C_HEADS = 16
C_LN_EPS = 64e-5
WKV_CHUNK = 64

LN_EPS = 1e-5
DEEPNORM_ALPHA = (2 * DEPTH) ** 0.25

NEG_BIG = -1e30
LOG2E = 1.4426950408889634
POS_FEATURES = 6
MAX_POS = 64 * 256
VMEM_LIMIT = 56 * 1024 * 1024


def _cparams(*sem):
    return pltpu.CompilerParams(dimension_semantics=sem, vmem_limit_bytes=VMEM_LIMIT)


def _dot(a, b):
    return jnp.dot(a, b, preferred_element_type=F32)


def _dot_nt(a, b):
    return lax.dot_general(a, b, (((1,), (1,)), ((), ())), preferred_element_type=F32)


def _dot_hi(a, b):
    return jnp.dot(a, b, preferred_element_type=F32, precision=HIGHEST)


def _dot_nt_hi(a, b):
    return lax.dot_general(a, b, (((1,), (1,)), ((), ())), preferred_element_type=F32, precision=HIGHEST)


def _split3(x):
    hi = x.astype(BF16)
    r1 = x - hi.astype(F32)
    mid = r1.astype(BF16)
    lo = (r1 - mid.astype(F32)).astype(BF16)
    return hi, mid, lo


def _dot3(x, m01):
    hi, mid, lo = _split3(x)
    return _dot(hi, m01) + _dot(mid, m01) + _dot(lo, m01)


def _silu(z):
    return z * jax.nn.sigmoid(z)


def _proj_kernel(x_ref, *refs, n_out):
    xb = x_ref[...].astype(BF16)
    for w_ref, o_ref in zip(refs[:n_out], refs[n_out:]):
        o_ref[...] = _dot(xb, w_ref[...])


def _multi_proj(x, ws, tm=256):
    m, k = x.shape
    tm = min(tm, m)
    n_out = len(ws)
    return pl.pallas_call(
        functools.partial(_proj_kernel, n_out=n_out),
        grid=(m // tm,),
        in_specs=[pl.BlockSpec((tm, k), lambda i: (i, 0))]
        + [pl.BlockSpec(w.shape, lambda i: (0, 0)) for w in ws],
        out_specs=[pl.BlockSpec((tm, w.shape[1]), lambda i: (i, 0)) for w in ws],
        out_shape=[jax.ShapeDtypeStruct((m, w.shape[1]), F32) for w in ws],
        compiler_params=_cparams("parallel"),
        name="multi_proj",
    )(x, *ws)


def _deepnorm(x, y, g, b):
    v = DEEPNORM_ALPHA * x + y
    mu = jnp.mean(v, axis=-1, keepdims=True)
    var = jnp.mean(jnp.square(v - mu), axis=-1, keepdims=True)
    return (v - mu) * lax.rsqrt(var + LN_EPS) * g + b


def _out_ln_kernel(o_ref, z_ref, x_ref, w_ref, g_ref, b_ref, out_ref):
    gated = (o_ref[...] * _silu(z_ref[...])).astype(BF16)
    y = _dot(gated, w_ref[...])
    out_ref[...] = _deepnorm(x_ref[...], y, g_ref[...], b_ref[...])


def _row_spec(tm, n):
    return pl.BlockSpec((tm, n), lambda i: (i, 0))


def _full_spec(shape):
    return pl.BlockSpec(shape, lambda i: (0,) * len(shape))


def _out_ln(o, z, x, w_out, g, b, tm=256):
    m, d = x.shape
    tm = min(tm, m)
    return pl.pallas_call(
        _out_ln_kernel,
        grid=(m // tm,),
        in_specs=[_row_spec(tm, d), _row_spec(tm, d), _row_spec(tm, d),
                  _full_spec(w_out.shape), _full_spec((1, d)), _full_spec((1, d))],
        out_specs=_row_spec(tm, d),
        out_shape=jax.ShapeDtypeStruct((m, d), F32),
        compiler_params=_cparams("parallel"),
        name="out_ln",
    )(o, z, x, w_out, g.reshape(1, d), b.reshape(1, d))


def _pe_term_kernel(pe_ref, w1_ref, o_ref):
    for c in range(2):
        o_ref[c] = _dot_hi(pe_ref[c], w1_ref[c])


def _compress_weights(pe, w1, w2):
    pe_term = pl.pallas_call(
        _pe_term_kernel, out_shape=jax.ShapeDtypeStruct((2, 1, CMP_HIDDEN), F32), name="cmp_pe_term",
    )(pe.reshape(2, 1, CMP_BLOCK * A_HEAD_DIM), w1)
    w1r = w1.astype(BF16).reshape(2, CMP_RATIO, CMP_STRIDE // 2, 2, A_HEAD_DIM, CMP_HIDDEN)
    w_big = w1r.transpose(0, 2, 3, 4, 1, 5).reshape(2, CMP_STRIDE // 2, 2 * A_HEAD_DIM, CMP_RATIO * CMP_HIDDEN)
    return dict(pe_term=pe_term, w_big=w_big, w2=w2.astype(BF16))


def _chunk_proj(load_tap, wbig_ref):
    acc = [[None] * A_KV_HEADS for _ in range(2)]
    for c in range(2):
        for pair in range(2):
            for tp in range(CMP_STRIDE // 2):
                tap0, tap1 = load_tap(c, pair, 2 * tp), load_tap(c, pair, 2 * tp + 1)
                for half in range(2):
                    g = 2 * pair + half
                    hs = slice(half * A_HEAD_DIM, (half + 1) * A_HEAD_DIM)
                    xs = jnp.concatenate([tap0[:, hs], tap1[:, hs]], axis=1)
                    part = _dot(xs.astype(BF16), wbig_ref[c, tp])
                    acc[c][g] = part if acc[c][g] is None else acc[c][g] + part
    return acc


def _compress_finish(acc, pe_ref, w2_ref, k_ref, v_ref, n_chunk):
    for c, o_ref in ((0, k_ref), (1, v_ref)):
        for g in range(A_KV_HEADS):
            r0 = acc[c][g][:, :CMP_HIDDEN]
            r1 = acc[c][g][:, CMP_HIDDEN:]
            h = r0 + pltpu.roll(r1, n_chunk - 1, 0) + pe_ref[c]
            o_ref[g] = _dot(_silu(h).astype(BF16), w2_ref[c])


def _compress_kernel(x_ref, pe_ref, wbig_ref, w2_ref, k_ref, v_ref, *, n_chunk):
    load_tap = lambda c, pair, s: x_ref[c * 2 + pair, pl.ds(s, n_chunk, stride=CMP_STRIDE), :]
    acc = _chunk_proj(load_tap, wbig_ref)
    _compress_finish(acc, pe_ref, w2_ref, k_ref, v_ref, n_chunk)


def _compress_prompt(kvc4, bsz, cw):
    n_blk, m, _ = kvc4.shape
    t = m // bsz
    n_chunk = t // CMP_STRIDE
    out = jax.ShapeDtypeStruct((bsz, A_KV_HEADS, n_chunk, A_HEAD_DIM), F32)
    out_spec = pl.BlockSpec((None, A_KV_HEADS, n_chunk, A_HEAD_DIM), lambda b: (b, 0, 0, 0))
    return pl.pallas_call(
        functools.partial(_compress_kernel, n_chunk=n_chunk),
        grid=(bsz,),
        in_specs=[pl.BlockSpec((n_blk, t, 128), lambda b: (0, b, 0)), _full_spec(cw["pe_term"].shape),
                  _full_spec(cw["w_big"].shape), _full_spec(cw["w2"].shape)],
        out_specs=[out_spec, out_spec],
        out_shape=[out, out],
        compiler_params=_cparams("parallel"),
        name="compress_prompt",
    )(kvc4, cw["pe_term"], cw["w_big"], cw["w2"])


def _softmax_rows(s, valid):
    s = jnp.where(valid, s, NEG_BIG)
    m = jnp.max(s, axis=-1, keepdims=True)
    e = jnp.where(valid, jnp.exp(s - m), 0.0)
    return e / jnp.maximum(jnp.sum(e, axis=-1, keepdims=True), 1e-30)


def _topk_mask(score_t, n_keep):
    n_blk = score_t.shape[0]
    blk = lax.broadcasted_iota(jnp.int32, score_t.shape, 0)
    rank = jnp.zeros(score_t.shape, F32)
    for i in range(n_blk):
        row = score_t[i:i + 1, :]
        tie_ahead = jnp.where(blk > i, 1.0, 0.0)
        rank = rank + jnp.where(row > score_t, 1.0, jnp.where(row == score_t, tie_ahead, 0.0))
    return jnp.where(rank < n_keep, 1.0, 0.0)


def _tile_heads(x):
    return jnp.concatenate([x] * A_GROUP, axis=0)


def _nsa_prompt_kernel(q_ref, gate_ref, sf_ref, kc_ref, vc_ref, ks_ref, vs_ref, kw_ref, vw_ref,
                       sel_map_ref, expand_ref, o_ref, *, qb, i_base):
    t0 = (i_base + pl.program_id(2)) * qb
    n_keys = ks_ref.shape[0]
    n_cmp_pad = kc_ref.shape[0]
    n_sel = n_keys // SEL_BLOCK

    q = q_ref[...] * (A_HEAD_DIM ** -0.5 * LOG2E)
    qg = jnp.concatenate([q[:, r * 64:(r + 1) * 64] for r in range(A_GROUP)], axis=0)
    qa = jnp.concatenate([qg, sf_ref[...]], axis=1).astype(BF16)
    t_q = t0 + lax.broadcasted_iota(jnp.int32, (qb, 1), 0)

    n_idx = lax.broadcasted_iota(jnp.int32, (qb, n_cmp_pad), 1)
    ok_c = n_idx * CMP_STRIDE + (CMP_BLOCK - 1) <= t_q
    s_c = _dot_nt(qa, kc_ref[...]) + _tile_heads(jnp.where(ok_c, 0.0, NEG_BIG))
    m_c = jnp.max(s_c, axis=-1, keepdims=True)
    e_c = jnp.where(s_c > 0.5 * NEG_BIG, jnp.exp2(s_c - m_c), 0.0)
    p_c = e_c / jnp.maximum(jnp.sum(e_c, axis=-1, keepdims=True), 1e-30)
    o_c = _dot(p_c.astype(BF16), vc_ref[...])

    p_grp = p_c[0:qb] + p_c[qb:2 * qb] + p_c[2 * qb:3 * qb] + p_c[3 * qb:4 * qb]
    sel_map = sel_map_ref[...]
    hi, mid, lo = _split3(p_grp)
    p_slc_t = _dot_nt(sel_map, hi) + _dot_nt(sel_map, mid) + _dot_nt(sel_map, lo)
    blk = lax.broadcasted_iota(jnp.int32, (n_sel, qb), 0)
    cur = (t0 + lax.broadcasted_iota(jnp.int32, (n_sel, qb), 1)) // SEL_BLOCK
    forced = (blk == 0) | (blk == cur) | (blk == cur - 1)
    score_t = jnp.where(forced, 1e30, jnp.where(blk <= cur, p_slc_t, -1.0))
    keep_t = jnp.where(blk <= cur, _topk_mask(score_t, SEL_TOPK), 0.0)
    keep = keep_t.T.astype(BF16)

    k_pos = lax.broadcasted_iota(jnp.int32, (qb, n_keys), 1)
    bias = jnp.where(k_pos > t_q, NEG_BIG, (_dot(keep, expand_ref[...]) - 1.0) * (-NEG_BIG))
    s_s = _dot_nt(qa, ks_ref[...]) + _tile_heads(bias)
    p_s = jnp.exp2(s_s - jnp.max(s_s, axis=-1, keepdims=True))
    o_s = _dot(p_s.astype(BF16), vs_ref[...]) / jnp.sum(p_s, axis=-1, keepdims=True)

    n_win = min(WINDOW + qb, n_keys)
    w0 = pl.multiple_of(jnp.maximum(t0 + qb - n_win, 0), 8)
    w_pos = w0 + lax.broadcasted_iota(jnp.int32, (qb, n_win), 1)
    ok_w = (w_pos <= t_q) & (t_q - w_pos <= WINDOW)
    s_w = _dot_nt(qa, kw_ref[pl.ds(w0, n_win), :]) + _tile_heads(jnp.where(ok_w, 0.0, NEG_BIG))
    p_w = jnp.exp2(s_w - jnp.max(s_w, axis=-1, keepdims=True))
    o_w = _dot(p_w.astype(BF16), vw_ref[pl.ds(w0, n_win), :]) / jnp.sum(p_w, axis=-1, keepdims=True)

    gate = jax.nn.sigmoid(gate_ref[...])
    for r in range(A_GROUP):
        rs = slice(r * qb, (r + 1) * qb)
        o_ref[:, r * 64:(r + 1) * 64] = (gate[:, r:r + 1] * o_c[rs]
                                         + gate[:, A_GROUP + r:A_GROUP + r + 1] * o_s[rs]
                                         + gate[:, 2 * A_GROUP + r:2 * A_GROUP + r + 1] * o_w[rs])


def _nsa_proj_kernel(x_ref, wq_ref, wc_ref, ws_ref, ww_ref, wg_ref, wz_ref,
                     q_ref, kvc_ref, kvs_ref, kvw_ref, z_ref, kvc4_ref, gates_ref,
                     ks_ref, vs_ref, kw_ref, vw_ref, *, tiles_per_seq):
    xb = x_ref[...].astype(BF16)
    tm = xb.shape[0]
    q_ref[...] = _dot(xb, wq_ref[...])
    z_ref[...] = _dot(xb, wz_ref[...])
    kvc = _dot(xb, wc_ref[...])
    kvc_ref[...] = kvc
    for j in range(A_KV_WIDTH // 128):
        kvc4_ref[j] = kvc[:, j * 128:(j + 1) * 128]
    gate = _dot(xb, wg_ref[...])
    t0 = lax.rem(pl.program_id(0), tiles_per_seq) * tm
    pos = t0 + lax.broadcasted_iota(jnp.int32, (tm, A_HEAD_DIM), 0)
    lane = lax.broadcasted_iota(jnp.int32, (tm, A_HEAD_DIM), 1)
    feat = jnp.where(lane < POS_FEATURES, jnp.where(lane % 2 == 0, pos // 64, pos % 64), 0).astype(F32)
    half = A_KV_WIDTH // 2
    for w_ref, full_ref, k_ref, v_ref in ((ws_ref, kvs_ref, ks_ref, vs_ref), (ww_ref, kvw_ref, kw_ref, vw_ref)):
        kv = _dot(xb, w_ref[...])
        full_ref[...] = kv
        for g in range(A_KV_HEADS):
            keys = kv[:, g * A_HEAD_DIM:(g + 1) * A_HEAD_DIM]
            k_ref[g] = jnp.concatenate([keys, feat], axis=1).astype(BF16)
            v_ref[g] = kv[:, half + g * A_HEAD_DIM:half + (g + 1) * A_HEAD_DIM].astype(BF16)
    for g in range(A_KV_HEADS):
        gates_ref[g] = jnp.concatenate(
            [gate[:, br * A_HEADS + g * A_GROUP:br * A_HEADS + (g + 1) * A_GROUP] for br in range(3)], axis=1)


def _nsa_proj(x, ws, tm=256):
    bsz, t, d = x.shape
    assert t < MAX_POS
    m = bsz * t
    tm = min(tm, t)
    tps = t // tm
    rows = lambda n: pl.BlockSpec((tm, n), lambda i: (i, 0))
    heads = lambda n: pl.BlockSpec((None, A_KV_HEADS, tm, n), lambda i: (i // tps, 0, i % tps, 0))
    f32 = lambda *shape: jax.ShapeDtypeStruct(shape, F32)
    bf16 = lambda *shape: jax.ShapeDtypeStruct(shape, BF16)
    hd = A_HEAD_DIM
    return pl.pallas_call(
        functools.partial(_nsa_proj_kernel, tiles_per_seq=tps),
        grid=(m // tm,),
        in_specs=[rows(d)] + [_full_spec(w.shape) for w in ws],
        out_specs=[rows(d), rows(A_KV_WIDTH), rows(A_KV_WIDTH), rows(A_KV_WIDTH), rows(d),
                   pl.BlockSpec((A_KV_WIDTH // 128, tm, 128), lambda i: (0, i, 0)), heads(3 * A_GROUP),
                   heads(2 * hd), heads(hd), heads(2 * hd), heads(hd)],
        out_shape=[f32(m, d), f32(m, A_KV_WIDTH), f32(m, A_KV_WIDTH), f32(m, A_KV_WIDTH), f32(m, d),
                   f32(A_KV_WIDTH // 128, m, 128), f32(bsz, A_KV_HEADS, t, 3 * A_GROUP),
                   bf16(bsz, A_KV_HEADS, t, 2 * hd), bf16(bsz, A_KV_HEADS, t, hd),
                   bf16(bsz, A_KV_HEADS, t, 2 * hd), bf16(bsz, A_KV_HEADS, t, hd)],
        compiler_params=_cparams("parallel"),
        name="nsa_proj",
    )(x.reshape(m, d), *ws)


def _sel_map(n_sel, n_cmp_pad, n_cmp):
    j = np.arange(n_sel)[:, None]
    n = np.arange(n_cmp_pad)[None, :]
    m = (n >= SEL_RATIO * j - (CMP_RATIO - 1)) & (n <= SEL_RATIO * j + SEL_RATIO - 1) & (n < n_cmp)
    return jnp.asarray(m, BF16)


def _slope_features(slopes, qb):
    s = slopes * LOG2E
    s1 = s.astype(BF16).astype(F32)
    s2 = (s - s1).astype(BF16).astype(F32)
    s3 = (s - s1 - s2).astype(BF16).astype(F32)
    feat = jnp.stack([64.0 * s1, s1, 64.0 * s2, s2, 64.0 * s3, s3], axis=-1)
    feat = jnp.pad(feat, ((0, 0), (0, A_HEAD_DIM - POS_FEATURES)))
    return jnp.repeat(feat.reshape(A_KV_HEADS, A_GROUP, A_HEAD_DIM), qb, axis=1)


def _with_pos(keys, pos):
    assert int(pos.max()) < MAX_POS
    a, b = pos // 64, pos % 64
    feat = np.zeros((pos.shape[0], A_HEAD_DIM), np.float32)
    feat[:, 0:POS_FEATURES:2] = a[:, None]
    feat[:, 1:POS_FEATURES:2] = b[:, None]
    feat = jnp.broadcast_to(jnp.asarray(feat, BF16), keys.shape)
    return jnp.concatenate([keys.astype(BF16), feat], axis=-1)


def _nsa_prompt_attn(q, gates, ks, vs, kw, vw, k_cmp, v_cmp, slopes, qb=256, key_step=256):
    bsz, t, _ = q.shape
    key_step = min(key_step, t)
    qb = min(qb, key_step)
    k_cmp = _with_pos(k_cmp, np.arange(k_cmp.shape[2]) * CMP_STRIDE + (CMP_BLOCK - 1))
    v_cmp = v_cmp.astype(BF16)
    slope_feat = _slope_features(slopes, qb)
    rows = A_GROUP * qb
    steps = key_step // qb
    outs = []
    for seg in range(t // key_step):
        n_keys = (seg + 1) * key_step
        n_cmp_pad = n_keys // CMP_STRIDE
        n_sel = n_keys // SEL_BLOCK
        expand = jnp.asarray(np.arange(n_keys)[None, :] // SEL_BLOCK == np.arange(n_sel)[:, None], BF16)
        i_base = seg * steps
        head = lambda rows_, width: pl.BlockSpec((None, None, rows_, width), lambda b, g, i: (b, g, 0, 0))
        outs.append(pl.pallas_call(
            functools.partial(_nsa_prompt_kernel, qb=qb, i_base=i_base),
            grid=(bsz, A_KV_HEADS, steps),
            in_specs=[pl.BlockSpec((None, qb, 256), lambda b, g, i, i_base=i_base: (b, i_base + i, g)),
                      pl.BlockSpec((None, None, qb, 12), lambda b, g, i, i_base=i_base: (b, g, i_base + i, 0)),
                      pl.BlockSpec((None, rows, A_HEAD_DIM), lambda b, g, i: (g, 0, 0)),
                      head(n_cmp_pad, 2 * A_HEAD_DIM), head(n_cmp_pad, A_HEAD_DIM),
                      head(n_keys, 2 * A_HEAD_DIM), head(n_keys, A_HEAD_DIM),
                      head(n_keys, 2 * A_HEAD_DIM), head(n_keys, A_HEAD_DIM),
                      pl.BlockSpec((n_sel, n_cmp_pad), lambda b, g, i: (0, 0)),
                      pl.BlockSpec((n_sel, n_keys), lambda b, g, i: (0, 0))],
            out_specs=pl.BlockSpec((None, qb, 256), lambda b, g, i: (b, i, g)),
            out_shape=jax.ShapeDtypeStruct((bsz, key_step, D_MODEL), F32),
            compiler_params=_cparams("parallel", "parallel", "arbitrary"),
            name=f"nsa_prompt_attn_{n_keys}",
        )(q, gates, slope_feat, k_cmp, v_cmp, ks, vs, kw, vw,
          _sel_map(n_sel, n_cmp_pad, t // CMP_STRIDE - 1), expand))
    return jnp.concatenate(outs, axis=1)


def _alibi_slopes():
    return 2.0 ** (-8.0 * jnp.arange(1, A_HEADS + 1, dtype=F32) / A_HEADS)


def _nsa_weights(w_in, pe, w1, w2, w_out):
    cuts = [int(c) for c in np.cumsum([D_MODEL, A_KV_WIDTH, A_KV_WIDTH, A_KV_WIDTH, 3 * A_HEADS])]
    pieces = jnp.split(w_in.astype(BF16), cuts, axis=-1)
    return dict(w_pieces=pieces, cmp=_compress_weights(pe, w1, w2), w_out=w_out.astype(BF16))


def _nsa_layer_prompt(x, wt, ln_g, ln_b):
    bsz, t, d = x.shape
    x2 = x.reshape(bsz * t, d)
    q, kv_c, kv_s, kv_w, z, kvc4, gates, ks, vs, kw, vw = _nsa_proj(x, wt["w_pieces"])
    kv_c3, kv_s3, kv_w3 = (a.reshape(bsz, t, A_KV_WIDTH) for a in (kv_c, kv_s, kv_w))
    k_cmp, v_cmp = _compress_prompt(kvc4, bsz, wt["cmp"])
    o = _nsa_prompt_attn(q.reshape(bsz, t, d), gates, ks, vs, kw, vw, k_cmp, v_cmp, _alibi_slopes())
    x_new = _out_ln(o.reshape(bsz * t, d), z, x2, wt["w_out"], ln_g, ln_b)
    return x_new.reshape(bsz, t, d), kv_c3, kv_s3, kv_w3


def _token_minor(cache):
    nd = cache.ndim
    return cache.transpose(*range(nd - 4), nd - 3, nd - 2, nd - 1, nd - 4)


def _page_group_copies(cache_ref, layer, pt_ref, q, buf_ref, sem_ref):
    slot = lax.rem(q, 2)
    return [pltpu.make_async_copy(cache_ref.at[layer, pt_ref[q * PAGE_GROUP + i]], buf_ref.at[slot, i],
                                  sem_ref.at[slot]) for i in range(PAGE_GROUP)]


def _sample_compress_kernel(pt_ref, cache_ref, pe_ref, wbig_ref, w2_ref, k_ref, v_ref, pg_ref, xt_ref, sem,
                            *, layer, n_pages):
    b = pl.program_id(0)
    n_grp = n_pages // PAGE_GROUP
    n_total = pl.num_programs(0) * n_grp

    @pl.when(b == 0)
    def _():
        for cp in _page_group_copies(cache_ref, layer, pt_ref, 0, pg_ref, sem):
            cp.start()

    for gi in range(n_grp):
        q = b * n_grp + gi

        @pl.when(q + 1 < n_total)
        def _():
            for cp in _page_group_copies(cache_ref, layer, pt_ref, q + 1, pg_ref, sem):
                cp.start()

        for cp in _page_group_copies(cache_ref, layer, pt_ref, q, pg_ref, sem):
            cp.wait()
        slot = lax.rem(q, 2)

        def page_body(i, carry):
            row0 = pl.multiple_of((gi * PAGE_GROUP + i) * PAGE_SIZE, PAGE_SIZE)
            for c in range(2):
                for pair in range(2):
                    tiles = pg_ref[slot, i, c, pl.ds(2 * pair, 2)]
                    xt_ref[c, pair, pl.ds(row0, PAGE_SIZE), :] = tiles.reshape(2 * A_HEAD_DIM, PAGE_SIZE).T
            return carry

        lax.fori_loop(0, PAGE_GROUP, page_body, 0, unroll=PAGE_GROUP // 2)

    n_chunk = n_pages * PAGE_SIZE // CMP_STRIDE
    load_tap = lambda c, pair, s: xt_ref[c, pair, pl.ds(s, n_chunk, stride=CMP_STRIDE), :]
    acc = _chunk_proj(load_tap, wbig_ref)
    _compress_finish(acc, pe_ref, w2_ref, k_ref, v_ref, n_chunk)


def _sample_compress(cache, layer, page_table, cw):
    bsz, n_pages = page_table.shape
    assert n_pages % PAGE_GROUP == 0
    n_tok = n_pages * PAGE_SIZE
    n_chunk = n_tok // CMP_STRIDE
    out = jax.ShapeDtypeStruct((bsz, A_KV_HEADS, n_chunk, A_HEAD_DIM), F32)
    out_spec = pl.BlockSpec((None, A_KV_HEADS, n_chunk, A_HEAD_DIM), lambda b, pt: (b, 0, 0, 0))
    full = lambda a: pl.BlockSpec(a.shape, lambda b, pt: (0,) * a.ndim)
    grid_spec = pltpu.PrefetchScalarGridSpec(
        num_scalar_prefetch=1, grid=(bsz,),
        in_specs=[pl.BlockSpec(memory_space=pl.ANY), full(cw["pe_term"]), full(cw["w_big"]), full(cw["w2"])],
        out_specs=[out_spec, out_spec],
        scratch_shapes=[pltpu.VMEM((2, PAGE_GROUP, 2, A_KV_HEADS, A_HEAD_DIM, PAGE_SIZE), F32),
                        pltpu.VMEM((2, 2, n_tok, 2 * A_HEAD_DIM), F32), pltpu.SemaphoreType.DMA((2,))])
    return pl.pallas_call(
        functools.partial(_sample_compress_kernel, layer=layer, n_pages=n_pages),
        grid_spec=grid_spec, out_shape=[out, out],
        compiler_params=_cparams("arbitrary"),
        name="sample_compress",
    )(page_table.reshape(-1), _token_minor(cache), cw["pe_term"], cw["w_big"], cw["w2"])


def _sample_cmp_kernel(q_ref, slope_ref, kc_ref, vc_ref, map_ref, oc_ref, idx_ref, *, past, n_cand):
    n_pad = kc_ref.shape[1]
    n_cmp = n_pad - 1
    n_idx = lax.broadcasted_iota(jnp.int32, (A_GROUP, n_pad), 1)
    dist = (past - (CMP_BLOCK - 1)) - n_idx * CMP_STRIDE
    valid = (dist >= 0) & (n_idx < n_cmp)
    q = q_ref[...] * (A_HEAD_DIM ** -0.5)
    n_keep = idx_ref.shape[1]
    width = map_ref.shape[1]
    lane = lax.broadcasted_iota(jnp.int32, (1, width), 1)
    row_i = lax.broadcasted_iota(jnp.int32, (width, width), 0)
    col_j = lax.broadcasted_iota(jnp.int32, (width, width), 1)
    for g in range(A_KV_HEADS):
        hs = slice(g * A_GROUP, (g + 1) * A_GROUP)
        s = _dot_nt(q[hs].astype(BF16), kc_ref[g].astype(BF16)) - slope_ref[hs] * dist.astype(F32)
        p = _softmax_rows(s, valid)
        oc_ref[hs, :] = _dot(p.astype(BF16), vc_ref[g].astype(BF16))
        p_grp = jnp.sum(p, axis=0, keepdims=True)
        p_slc = _dot3(p_grp, map_ref[...])
        forced = (lane == 0) | (lane == n_cand - 1)
        score = jnp.where(forced, 1e30, jnp.where(lane < n_cand, p_slc, -1.0))
        s_j = jnp.broadcast_to(score, (width, width))
        s_i = s_j.T
        tie_ahead = jnp.where(row_i < col_j, 1.0, 0.0)
        ahead = jnp.where(s_i > s_j, 1.0, jnp.where(s_i == s_j, tie_ahead, 0.0))
        rank = jnp.sum(ahead, axis=0, keepdims=True)
        want = lax.broadcasted_iota(jnp.int32, (n_keep, width), 0).astype(F32)
        picked = jnp.where(rank == want, lane.astype(F32), 0.0)
        idx_ref[g] = jnp.sum(picked, axis=-1, keepdims=True).astype(jnp.int32)


def _sample_cmp_topk(q, k_cmp, v_cmp, slopes, past):
    bsz = q.shape[0]
    n_pad = k_cmp.shape[2]
    n_cand = past // SEL_BLOCK
    width = -(-n_cand // 128) * 128
    j = np.arange(width)[None, :]
    n = np.arange(n_pad)[:, None]
    sel_map = (n >= SEL_RATIO * j - (CMP_RATIO - 1)) & (n <= SEL_RATIO * j + SEL_RATIO - 1) & (n < n_pad - 1)
    cmp_spec = pl.BlockSpec((None, A_KV_HEADS, n_pad, A_HEAD_DIM), lambda b: (b, 0, 0, 0))
    return pl.pallas_call(
        functools.partial(_sample_cmp_kernel, past=past, n_cand=n_cand),
        grid=(bsz,),
        in_specs=[pl.BlockSpec((None, A_HEADS, A_HEAD_DIM), lambda b: (b, 0, 0)),
                  _full_spec((A_HEADS, 1)), cmp_spec, cmp_spec, _full_spec((n_pad, width))],
        out_specs=[pl.BlockSpec((None, A_HEADS, A_HEAD_DIM), lambda b: (b, 0, 0)),
                   pl.BlockSpec((None, A_KV_HEADS, SEL_TOPK, 1), lambda b: (b, 0, 0, 0))],
        out_shape=[jax.ShapeDtypeStruct((bsz, A_HEADS, A_HEAD_DIM), F32),
                   jax.ShapeDtypeStruct((bsz, A_KV_HEADS, SEL_TOPK, 1), jnp.int32)],
        compiler_params=_cparams("parallel"),
        name="sample_cmp_topk",
    )(q, slopes.reshape(A_HEADS, 1), k_cmp, v_cmp, jnp.asarray(sel_map, BF16))


def _sel_tile_copies(cache_ref, layer, pt_ref, idx_ref, b, n_pages, buf_ref, sem):
    slot = lax.rem(b, 2)
    copies = []
    for g in range(A_KV_HEADS):
        for r in range(SEL_TOPK - 1):
            blk = idx_ref[(b * A_KV_HEADS + g) * SEL_TOPK + r]
            page = pt_ref[b * n_pages + blk // (PAGE_SIZE // SEL_BLOCK)]
            copies += [pltpu.make_async_copy(cache_ref.at[layer, page, c, g], buf_ref.at[slot, g, r, c],
                                             sem.at[slot]) for c in range(2)]
    return copies


def _sample_attn_kernel(pt_ref, idx_ref, cache_ref, q_ref, gate_ref, slope_ref, oc_ref, ks_new_ref,
                        kw_new_ref, win_ref, o_ref, buf_ref, sem, *, layer, n_pages, past):
    b = pl.program_id(0)
    n_blk = SEL_TOPK - 1
    slot = lax.rem(b, 2)

    @pl.when(b == 0)
    def _():
        for cp in _sel_tile_copies(cache_ref, layer, pt_ref, idx_ref, b, n_pages, buf_ref, sem):
            cp.start()

    @pl.when(b + 1 < pl.num_programs(0))
    def _():
        for cp in _sel_tile_copies(cache_ref, layer, pt_ref, idx_ref, b + 1, n_pages, buf_ref, sem):
            cp.start()

    q = q_ref[...] * (A_HEAD_DIM ** -0.5)
    gate = jax.nn.sigmoid(gate_ref[...])
    n_buf = win_ref.shape[-1]
    tok = lax.broadcasted_iota(jnp.int32, (1, PAGE_SIZE), 1)
    dist_w = (n_buf - lax.broadcasted_iota(jnp.int32, (1, n_buf), 1)).astype(F32)

    def attend(qg, slope, keys_t, vals_t, dist, valid, k_new, v_new):
        s = _dot(qg.astype(BF16), keys_t.astype(BF16)) - slope * dist
        if valid is not None:
            s = jnp.where(valid, s, NEG_BIG)
        s_new = jnp.sum(qg * k_new, axis=-1, keepdims=True)
        m = jnp.maximum(jnp.max(s, axis=-1, keepdims=True), s_new)
        e = jnp.exp(s - m)
        e_new = jnp.exp(s_new - m)
        l = jnp.sum(e, axis=-1, keepdims=True) + e_new
        return (_dot_nt(e.astype(BF16), vals_t.astype(BF16)) + e_new * v_new) / l

    def new_row(ref, g):
        ksl = slice(g * A_HEAD_DIM, (g + 1) * A_HEAD_DIM)
        vsl = slice(A_KV_WIDTH // 2 + g * A_HEAD_DIM, A_KV_WIDTH // 2 + (g + 1) * A_HEAD_DIM)
        return ref[:, ksl], ref[:, vsl]

    out_w = []
    for g in range(A_KV_HEADS):
        hs = slice(g * A_GROUP, (g + 1) * A_GROUP)
        out_w.append(attend(q[hs], slope_ref[hs], win_ref[0, g], win_ref[1, g], dist_w, None,
                            *new_row(kw_new_ref, g)))

    for cp in _sel_tile_copies(cache_ref, layer, pt_ref, idx_ref, b, n_pages, buf_ref, sem):
        cp.wait()

    blocks_per_page = PAGE_SIZE // SEL_BLOCK
    for g in range(A_KV_HEADS):
        hs = slice(g * A_GROUP, (g + 1) * A_GROUP)
        keys_t = jnp.concatenate([buf_ref[slot, g, r, 0] for r in range(n_blk)], axis=1)
        vals_t = jnp.concatenate([buf_ref[slot, g, r, 1] for r in range(n_blk)], axis=1)
        dist, valid = [], []
        for r in range(n_blk):
            blk = idx_ref[(b * A_KV_HEADS + g) * SEL_TOPK + r]
            dist.append(past - ((blk // blocks_per_page) * PAGE_SIZE + tok))
            valid.append(tok // SEL_BLOCK == blk % blocks_per_page)
        dist = jnp.concatenate(dist, axis=1).astype(F32)
        valid = jnp.concatenate(valid, axis=1)
        o_s = attend(q[hs], slope_ref[hs], keys_t, vals_t, dist, valid, *new_row(ks_new_ref, g))
        gt = gate[hs]
        o_ref[hs, :] = gt[:, 0:1] * oc_ref[hs, :] + gt[:, 1:2] * o_s + gt[:, 2:3] * out_w[g]


def _sample_attn(page_table, idx, cache_sel, layer, q, gate, slopes, o_c, kv_s_new, kv_w_new, cache_win, past):
    bsz, n_pages = page_table.shape
    n_buf = cache_win.shape[2]
    head_spec = pl.BlockSpec((None, A_HEADS, A_HEAD_DIM), lambda b, pt, ix: (b, 0, 0))
    new_spec = pl.BlockSpec((None, 1, A_KV_WIDTH), lambda b, pt, ix: (b, 0, 0))
    grid_spec = pltpu.PrefetchScalarGridSpec(
        num_scalar_prefetch=2, grid=(bsz,),
        in_specs=[pl.BlockSpec(memory_space=pl.ANY), head_spec,
                  pl.BlockSpec((None, A_HEADS, 3), lambda b, pt, ix: (b, 0, 0)),
                  pl.BlockSpec((A_HEADS, 1), lambda b, pt, ix: (0, 0)),
                  head_spec, new_spec, new_spec,
                  pl.BlockSpec((None, None, 2, A_KV_HEADS, A_HEAD_DIM, n_buf),
                               lambda b, pt, ix: (layer, b, 0, 0, 0, 0))],
        out_specs=head_spec,
        scratch_shapes=[pltpu.VMEM((2, A_KV_HEADS, SEL_TOPK - 1, 2, A_HEAD_DIM, PAGE_SIZE), F32),
                        pltpu.SemaphoreType.DMA((2,))])
    return pl.pallas_call(
        functools.partial(_sample_attn_kernel, layer=layer, n_pages=n_pages, past=past),
        grid_spec=grid_spec,
        out_shape=jax.ShapeDtypeStruct((bsz, A_HEADS, A_HEAD_DIM), F32),
        compiler_params=_cparams("arbitrary"),
        name="sample_attn",
    )(page_table.reshape(-1), idx.reshape(-1), _token_minor(cache_sel), q, gate, slopes.reshape(A_HEADS, 1),
      o_c, kv_s_new, kv_w_new, _token_minor(cache_win))


def _nsa_layer_sample(x, wt, ln_g, ln_b, layer, cache_cmp, cache_sel, cache_win, page_table):
    bsz, _, d = x.shape
    n_pages = page_table.shape[1]
    past = n_pages * PAGE_SIZE
    x2 = x.reshape(bsz, d)
    q, kv_c, kv_s, kv_w, gate, z = _multi_proj(x2, wt["w_pieces"])
    slopes = _alibi_slopes()
    k_cmp, v_cmp = _sample_compress(cache_cmp, layer, page_table, wt["cmp"])
    q3 = q.reshape(bsz, A_HEADS, A_HEAD_DIM)
    o_c, idx = _sample_cmp_topk(q3, k_cmp, v_cmp, slopes, past)
    gate3 = gate.reshape(bsz, 3, A_HEADS).transpose(0, 2, 1)
    kv_s3, kv_w3 = kv_s.reshape(bsz, 1, A_KV_WIDTH), kv_w.reshape(bsz, 1, A_KV_WIDTH)
    o = _sample_attn(page_table, idx, cache_sel, layer, q3, gate3, slopes, o_c, kv_s3, kv_w3, cache_win, past)
    x_new = _out_ln(o.reshape(bsz, d), z, x2, wt["w_out"], ln_g, ln_b)
    n_buf = cache_win.shape[2]
    n_keep = min(WINDOW, n_buf + 1)
    win_old = cache_win[layer, :, n_buf + 1 - n_keep:].reshape(bsz, n_keep - 1, A_KV_WIDTH)
    win_new = jnp.concatenate([win_old, kv_w3], axis=1)
    return x_new.reshape(bsz, 1, d), kv_c.reshape(bsz, 1, A_KV_WIDTH), kv_s3, win_new


def _s5_param_kernel(log_dt_ref, ar_ref, ai_ref, br_ref, bi_ref, tile_ref,
                     abr_ref, abi_ref, bbr_ref, bbi_ref):
    dt = jnp.exp(log_dt_ref[...])
    ar, ai = ar_ref[...], ai_ref[...]
    mag = jnp.exp(dt * ar)
    abr, abi = mag * jnp.cos(dt * ai), mag * jnp.sin(dt * ai)
    den = ar * ar + ai * ai
    num_re, num_im = abr - 1.0, abi
    zoh_re = (num_re * ar + num_im * ai) / den
    zoh_im = (num_im * ar - num_re * ai) / den
    abr_ref[...] = abr
    abi_ref[...] = abi
    zr = _dot3(zoh_re, tile_ref[...])
    zi = _dot3(zoh_im, tile_ref[...])
    br, bi = br_ref[...], bi_ref[...]
    bbr_ref[...] = zr * br - zi * bi
    bbi_ref[...] = zr * bi + zi * br


def _s5_weights(w_in, log_dt, a_re, a_im, b_re, b_im, c_re, c_im, d_skip, w_glu, b_glu, w_out):
    g, p, c = S5_GROUPS, S5_STATE, S5_GROUP_CH
    tile = jnp.asarray(np.tile(np.eye(p, dtype=np.float32), (1, c)), BF16)
    to_gcp = lambda b: b.transpose(0, 2, 1).reshape(g, c * p)
    shapes = [jax.ShapeDtypeStruct((g, p), F32)] * 2 + [jax.ShapeDtypeStruct((g, c * p), F32)] * 2
    abr, abi, bbr, bbi = pl.pallas_call(_s5_param_kernel, out_shape=shapes, name="s5_params")(
        log_dt.reshape(g, 1), a_re, a_im, to_gcp(b_re), to_gcp(b_im), tile)
    eye = jnp.eye(S5_GROUPS // S5_SETS, dtype=F32)
    gl = S5_GROUPS // S5_SETS

    def in_blockdiag(bb):
        bb = bb.reshape(S5_SETS, gl, c, p)
        return jnp.einsum('sgcp,gh->sgchp', bb, eye).reshape(S5_SETS, gl * c, gl * p).astype(BF16)

    def out_blockdiag(cc):
        cc = cc.reshape(S5_SETS, gl, c, p)
        return jnp.einsum('sgcp,gh->sgphc', cc, eye).reshape(S5_SETS, gl * p, gl * c).astype(BF16)

    w_u, w_z = jnp.split(w_in.astype(BF16), 2, axis=-1)
    return dict(w_pieces=[w_u, w_z], abr=abr.reshape(1, g * p), abi=abi.reshape(1, g * p),
                wb_re=in_blockdiag(bbr), wb_im=in_blockdiag(bbi),
                wc_re=out_blockdiag(c_re), wc_im=out_blockdiag(c_im),
                d_skip=d_skip.reshape(1, -1), w_glu=w_glu.astype(BF16), b_glu=b_glu.reshape(1, -1),
                w_out=w_out.astype(BF16))


def _cmul(ar, ai, br, bi):
    return ar * br - ai * bi, ar * bi + ai * br


def _s5_row_scan(bur, bui, ar_ref, ai_ref, hr_c, hi_c, tc):
    n_set = S5_GROUPS * S5_STATE // S5_SETS
    for s in range(S5_SETS):
        sl = pl.ds(s * n_set, n_set)
        ar, ai = ar_ref[:, sl], ai_ref[:, sl]

        def step(t, carry):
            hr, hi = carry
            pr, pi = _cmul(ar, ai, hr, hi)
            nr, ni = pr + bur[pl.ds(t, 1), sl], pi + bui[pl.ds(t, 1), sl]
            bur[pl.ds(t, 1), sl] = nr
            bui[pl.ds(t, 1), sl] = ni
            return nr, ni

        hr, hi = lax.fori_loop(0, tc, step, (hr_c[:, sl], hi_c[:, sl]))
        hr_c[:, sl] = hr
        hi_c[:, sl] = hi


def _s5_tile_scan(bur, bui, ar_ref, ai_ref, hr_c, hi_c, tc):
    width = SCAN_LANES
    row = lax.broadcasted_iota(jnp.int32, (SCAN_TILE, width), 0)
    for s in range(S5_GROUPS * S5_STATE // width):
        sl = pl.ds(s * width, width)
        ar, ai = ar_ref[:, sl], ai_ref[:, sl]
        pows = [(ar, ai)]
        for _ in range(SCAN_TILE - 1):
            pows.append(_cmul(*pows[-1], ar, ai))
        pw_r = jnp.concatenate([p[0] for p in pows], axis=0)
        pw_i = jnp.concatenate([p[1] for p in pows], axis=0)
        steps = []
        for k in (1, 2, 4):
            keep = row >= k
            steps.append((k, jnp.where(keep, pows[k - 1][0], 0.0), jnp.where(keep, pows[k - 1][1], 0.0)))

        def tile(i, carry):
            cr, ci = carry
            rows = pl.ds(pl.multiple_of(i * SCAN_TILE, SCAN_TILE), SCAN_TILE)
            yr, yi = bur[rows, sl], bui[rows, sl]
            for k, kr, ki in steps:
                pr, pi = _cmul(kr, ki, pltpu.roll(yr, k, 0), pltpu.roll(yi, k, 0))
                yr, yi = yr + pr, yi + pi
            pr, pi = _cmul(pw_r, pw_i, cr, ci)
            yr, yi = yr + pr, yi + pi
            bur[rows, sl] = yr
            bui[rows, sl] = yi
            return yr[SCAN_TILE - 1:, :], yi[SCAN_TILE - 1:, :]

        hr, hi = lax.fori_loop(0, tc // SCAN_TILE, tile, (hr_c[:, sl], hi_c[:, sl]), unroll=4)
        hr_c[:, sl] = hr
        hi_c[:, sl] = hi


def _s5_scan_kernel(u_ref, wbr_ref, wbi_ref, ar_ref, ai_ref, wcr_ref, wci_ref, h0r_ref, h0i_ref,
                    y_ref, hr_out, hi_out, bur, bui, hr_c, hi_c, *, tc):
    @pl.when(pl.program_id(1) == 0)
    def _():
        hr_c[...] = h0r_ref[...]
        hi_c[...] = h0i_ref[...]

    n_set = S5_GROUPS * S5_STATE // S5_SETS
    n_ch = S5_GROUPS * S5_GROUP_CH // S5_SETS
    ub = u_ref[...].astype(BF16)
    for s in range(S5_SETS):
        us = ub[:, s * n_ch:(s + 1) * n_ch]
        bur[:, s * n_set:(s + 1) * n_set] = _dot(us, wbr_ref[s])
        bui[:, s * n_set:(s + 1) * n_set] = _dot(us, wbi_ref[s])

    if tc % SCAN_TILE == 0:
        _s5_tile_scan(bur, bui, ar_ref, ai_ref, hr_c, hi_c, tc)
    else:
        _s5_row_scan(bur, bui, ar_ref, ai_ref, hr_c, hi_c, tc)

    for s in range(S5_SETS):
        h_re = bur[:, s * n_set:(s + 1) * n_set].astype(BF16)
        h_im = bui[:, s * n_set:(s + 1) * n_set].astype(BF16)
        y_ref[:, s * n_ch:(s + 1) * n_ch] = _dot(h_re, wcr_ref[s]) - _dot(h_im, wci_ref[s])
    hr_out[...] = hr_c[...]
    hi_out[...] = hi_c[...]


def _s5_scan(u, h0_re, h0_im, wt, tc=256):
    bsz, t, d = u.shape
    tc = min(tc, t)
    n_state = S5_GROUPS * S5_STATE
    st_spec = pl.BlockSpec((None, 1, n_state), lambda b, j: (b, 0, 0))
    full = lambda a: pl.BlockSpec(a.shape, lambda b, j: (0,) * a.ndim)
    st_shape = jax.ShapeDtypeStruct((bsz, 1, n_state), F32)
    return pl.pallas_call(
        functools.partial(_s5_scan_kernel, tc=tc),
        grid=(bsz, t // tc),
        in_specs=[pl.BlockSpec((None, tc, d), lambda b, j: (b, j, 0)),
                  full(wt["wb_re"]), full(wt["wb_im"]), full(wt["abr"]), full(wt["abi"]),
                  full(wt["wc_re"]), full(wt["wc_im"]), st_spec, st_spec],
        out_specs=[pl.BlockSpec((None, tc, d), lambda b, j: (b, j, 0)), st_spec, st_spec],
        out_shape=[jax.ShapeDtypeStruct((bsz, t, d), F32), st_shape, st_shape],
        scratch_shapes=[pltpu.VMEM((tc, n_state), F32), pltpu.VMEM((tc, n_state), F32),
                        pltpu.VMEM((1, n_state), F32), pltpu.VMEM((1, n_state), F32)],
        compiler_params=_cparams("parallel", "arbitrary"),
        name="s5_scan",
    )(u, wt["wb_re"], wt["wb_im"], wt["abr"], wt["abi"], wt["wc_re"], wt["wc_im"], h0_re, h0_im)


def _s5_out_kernel(y_ref, u_ref, z_ref, x_ref, d_ref, wg_ref, bg_ref, w_ref, g_ref, b_ref, out_ref):
    y = jax.nn.gelu(y_ref[...] + d_ref[...] * u_ref[...])
    y = y * jax.nn.sigmoid(_dot(y.astype(BF16), wg_ref[...]) + bg_ref[...])
    out = _dot((y * _silu(z_ref[...])).astype(BF16), w_ref[...])
    out_ref[...] = _deepnorm(x_ref[...], out, g_ref[...], b_ref[...])


def _s5_out(y, u, z, x, wt, g, b, tm=256):
    m, d = x.shape
    tm = min(tm, m)
    vec = _full_spec((1, d))
    return pl.pallas_call(
        _s5_out_kernel,
        grid=(m // tm,),
        in_specs=[_row_spec(tm, d)] * 4 + [vec, _full_spec((d, d)), vec, _full_spec((d, d)), vec, vec],
        out_specs=_row_spec(tm, d),
        out_shape=jax.ShapeDtypeStruct((m, d), F32),
        compiler_params=_cparams("parallel"),
        name="s5_out",
    )(y, u, z, x, wt["d_skip"], wt["w_glu"], wt["b_glu"], wt["w_out"], g.reshape(1, d), b.reshape(1, d))


def _s5_layer(x, h0_re, h0_im, wt, ln_g, ln_b):
    bsz, t, d = x.shape
    x2 = x.reshape(bsz * t, d)
    u, z = _multi_proj(x2, wt["w_pieces"])
    n_state = S5_GROUPS * S5_STATE
    y, hr, hi = _s5_scan(u.reshape(bsz, t, d), h0_re.reshape(bsz, 1, n_state),
                         h0_im.reshape(bsz, 1, n_state), wt)
    x_new = _s5_out(y.reshape(bsz * t, d), u, z, x2, wt, ln_g, ln_b)
    st = lambda h: h.reshape(bsz, S5_GROUPS, S5_STATE)
    return x_new.reshape(bsz, t, d), st(hr), st(hi)


def _head_sum(x, seg):
    w = seg.shape[0]
    return jnp.concatenate([_dot3(x[:, j:j + w], seg) for j in range(0, x.shape[1], w)], axis=1)


def _softplus(x):
    return jnp.maximum(x, 0.0) + jnp.log(1.0 + jnp.exp(-jnp.abs(x)))


def _rwkv_proj_kernel(x_ref, xp_ref, mu_ref, wr_ref, wwl_ref, wk_ref, wv_ref, wal_ref, wz_ref,
                      w0_ref, w2_ref, a0_ref, a2_ref, kk_ref, ka_ref, seg_ref,
                      r_ref, lw_ref, k_ref, v_ref, kkn_ref, a_ref, z_ref):
    x = x_ref[...]
    dx = xp_ref[...] - x
    mix = lambda m: (x + dx * mu_ref[m:m + 1, :]).astype(BF16)
    r_ref[...] = _dot(mix(0), wr_ref[...])
    w_lo = _dot(mix(1), wwl_ref[...])
    k = _dot(mix(2), wk_ref[...])
    v_ref[...] = _dot(mix(3), wv_ref[...])
    a_lo = _dot(mix(4), wal_ref[...])
    z_ref[...] = _dot(mix(5), wz_ref[...])
    w_log = -_softplus(-(w0_ref[...] + _dot(jnp.tanh(w_lo).astype(BF16), w2_ref[...]))) - 0.5
    lw_ref[...] = -jnp.exp(w_log)
    a = jax.nn.sigmoid(a0_ref[...] + _dot(a_lo.astype(BF16), a2_ref[...]))
    a_ref[...] = a
    kk = k * kk_ref[...]
    norm = jnp.sqrt(_head_sum(kk * kk, seg_ref[...]))
    kkn_ref[...] = kk / jnp.maximum(norm, 1e-12)
    k_ref[...] = k * (1.0 + (a - 1.0) * ka_ref[...])


def _head_seg(width=4 * C_HEAD_DIM):
    lane = np.arange(width) // C_HEAD_DIM
    return jnp.asarray(lane[:, None] == lane[None, :], BF16)


def _rwkv_weights(mu, w_in, w0, w2, a0, a2, k_k, k_a, r_k, lnx_g, lnx_b, w_out):
    d = D_MODEL
    lora_w, lora_a = w2.shape[0], a2.shape[0]
    cuts = [int(c) for c in np.cumsum([d, lora_w, d, d, lora_a])]
    pieces = jnp.split(w_in.astype(BF16), cuts, axis=-1)
    row = lambda a: a.reshape(1, d)
    return dict(mu=mu, pieces=pieces, w0=row(w0), w2=w2.astype(BF16), a0=row(a0), a2=a2.astype(BF16),
                k_k=row(k_k), k_a=row(k_a), r_k=row(r_k), lnx_g=row(lnx_g), lnx_b=row(lnx_b),
                w_out=w_out.astype(BF16), seg=_head_seg())


def _rwkv_proj(x, x_prev, wt, tm=256):
    m, d = x.shape
    tm = min(tm, m)
    ins = [x, x_prev, wt["mu"], *wt["pieces"], wt["w0"], wt["w2"], wt["a0"], wt["a2"],
           wt["k_k"], wt["k_a"], wt["seg"]]
    in_specs = [_row_spec(tm, d), _row_spec(tm, d)] + [_full_spec(a.shape) for a in ins[2:]]
    return pl.pallas_call(
        _rwkv_proj_kernel,
        grid=(m // tm,),
        in_specs=in_specs,
        out_specs=[_row_spec(tm, d)] * 7,
        out_shape=[jax.ShapeDtypeStruct((m, d), F32)] * 7,
        compiler_params=_cparams("parallel"),
        name="rwkv_proj",
    )(*ins)


def _dot_tn(a, b):
    return lax.dot_general(a, b, (((0,), (0,)), ((), ())), preferred_element_type=F32)


def _mm(a, w, dot=_dot):
    return dot(a.astype(BF16), w)


def _wkv_chunk_kernel(r_ref, lw_ref, k_ref, v_ref, kkn_ref, a_ref, y_ref, s_out, s_ref, *, chunk):
    n = chunk
    gw = 4 * C_HEAD_DIM
    n_grp = C_HEADS // 4

    @pl.when(pl.program_id(1) == 0)
    def _():
        s_ref[...] = jnp.zeros(s_ref.shape, F32)

    lw = lw_ref[...]
    row = lax.broadcasted_iota(jnp.int32, (n, n), 0)
    col = lax.broadcasted_iota(jnp.int32, (n, n), 1)
    cum = _dot_hi(jnp.where(row >= col, 1.0, 0.0), lw)
    tot = cum[n - 1:n, :]
    kkn = kkn_ref[...]
    b_vec = kkn * a_ref[...]
    e_neg = jnp.exp(-cum)
    e_tail = jnp.exp(tot - cum)
    a_t = -kkn * jnp.exp(cum - lw)
    r_t = r_ref[...] * jnp.exp(cum)
    k = k_ref[...]
    b_t, k_t = b_vec * e_neg, k * e_neg
    b_h, k_h = b_vec * e_tail, k * e_tail
    p_tot = jnp.exp(tot)
    v = v_ref[...]

    t_idx = lax.broadcasted_iota(jnp.int32, (n, gw), 0)
    s_idx = lax.rem(lax.broadcasted_iota(jnp.int32, (n, gw), 1), C_HEAD_DIM)
    strict = jnp.where(s_idx < t_idx, 1.0, 0.0)
    incl = jnp.where(s_idx <= t_idx, 1.0, 0.0)
    eye = jnp.where(s_idx == t_idx, 1.0, 0.0)
    same_head = (lax.broadcasted_iota(jnp.int32, (gw, gw), 0) // C_HEAD_DIM
                 == lax.broadcasted_iota(jnp.int32, (gw, gw), 1) // C_HEAD_DIM)
    head_mask = jnp.where(same_head, 1.0, 0.0)
    head_mask_bf = head_mask.astype(BF16)

    def blockdiag(x):
        return jnp.concatenate([x.astype(BF16)] * 4, axis=0) * head_mask_bf

    grp = range(n_grp)
    sls = [slice(g * gw, (g + 1) * gw) for g in grp]
    ar = [jnp.concatenate([a_t[:, sl], r_t[:, sl]], axis=0) for sl in sls]
    ab = [_mm(ar[g], blockdiag(b_t[:, sls[g]]), _dot_nt) for g in grp]
    ak = [_mm(ar[g], blockdiag(k_t[:, sls[g]]), _dot_nt) for g in grp]
    s0 = [s_ref[g] for g in grp]
    xs = [_mm(ar[g], s0[g].astype(BF16), _dot_nt) for g in grp]
    v_bd = [blockdiag(v[:, sl]) for sl in sls]
    a_ab = [ab[g][:n] * strict for g in grp]
    inv = [eye + a_ab[g] for g in grp]
    pw = a_ab
    for _ in range(int(math.log2(n)) - 1):
        pw = [_mm(pw[g], blockdiag(pw[g])) for g in grp]
        inv = [inv[g] + _mm(inv[g], blockdiag(pw[g])) for g in grp]
    x = [xs[g][:n] + _mm(ak[g][:n] * strict, v_bd[g]) for g in grp]
    u = [_mm(inv[g], blockdiag(x[g])) for g in grp]
    for g in grp:
        y_ref[:, sls[g]] = (xs[g][n:] + _mm(ab[g][n:] * incl, blockdiag(u[g]))
                            + _mm(ak[g][n:] * incl, v_bd[g]))
    for g in grp:
        uv = jnp.concatenate([u[g], v[:, sls[g]]], axis=0)
        bk = jnp.concatenate([b_h[:, sls[g]], k_h[:, sls[g]]], axis=0)
        s_ref[g] = s0[g] * p_tot[:, sls[g]] + _mm(uv, bk.astype(BF16), _dot_tn) * head_mask

    @pl.when(pl.program_id(1) == pl.num_programs(1) - 1)
    def _():
        for h in range(C_HEADS):
            g, hl = divmod(h, 4)
            blk = slice(hl * C_HEAD_DIM, (hl + 1) * C_HEAD_DIM)
            s_out[h] = s_ref[g, blk, blk]


def _wkv_chunked(r, lw, k, v, kkn, a, chunk=WKV_CHUNK):
    bsz, t, d = r.shape
    spec = pl.BlockSpec((None, chunk, d), lambda b, j: (b, j, 0))
    s_spec = pl.BlockSpec((None, C_HEADS, C_HEAD_DIM, C_HEAD_DIM), lambda b, j: (b, 0, 0, 0))
    return pl.pallas_call(
        functools.partial(_wkv_chunk_kernel, chunk=chunk),
        grid=(bsz, t // chunk),
        in_specs=[spec] * 6,
        out_specs=[spec, s_spec],
        out_shape=[jax.ShapeDtypeStruct((bsz, t, d), F32),
                   jax.ShapeDtypeStruct((bsz, C_HEADS, C_HEAD_DIM, C_HEAD_DIM), F32)],
        scratch_shapes=[pltpu.VMEM((C_HEADS // 4, 4 * C_HEAD_DIM, 4 * C_HEAD_DIM), F32)],
        compiler_params=_cparams("parallel", "arbitrary"),
        name="wkv_chunked",
    )(r, lw, k, v, kkn, a)


def _wkv_step_kernel(s_ref, r_ref, lw_ref, k_ref, kkn_ref, a_ref, v_ref, y_ref, s_out):
    s0 = s_ref[...]
    kkn = kkn_ref[...]
    sa = jnp.sum(s0 * (-kkn), axis=-1, keepdims=True)
    s1 = s0 * jnp.exp(lw_ref[...]) + sa * (kkn * a_ref[...]) + v_ref[...] * k_ref[...]
    s_out[...] = s1
    y_ref[...] = jnp.sum(s1 * r_ref[...], axis=-1, keepdims=True)


def _wkv_step(s0, r, lw, k, v, kkn, a):
    bsz = s0.shape[0]
    rowv = lambda x: x.reshape(bsz, C_HEADS, 1, C_HEAD_DIM)
    colv = lambda x: x.reshape(bsz, C_HEADS, C_HEAD_DIM, 1)
    s_spec = pl.BlockSpec((None, C_HEADS, C_HEAD_DIM, C_HEAD_DIM), lambda b: (b, 0, 0, 0))
    r_spec = pl.BlockSpec((None, C_HEADS, 1, C_HEAD_DIM), lambda b: (b, 0, 0, 0))
    c_spec = pl.BlockSpec((None, C_HEADS, C_HEAD_DIM, 1), lambda b: (b, 0, 0, 0))
    y, s1 = pl.pallas_call(
        _wkv_step_kernel,
        grid=(bsz,),
        in_specs=[s_spec] + [r_spec] * 5 + [c_spec],
        out_specs=[c_spec, s_spec],
        out_shape=[jax.ShapeDtypeStruct((bsz, C_HEADS, C_HEAD_DIM, 1), F32),
                   jax.ShapeDtypeStruct(s0.shape, F32)],
        compiler_params=_cparams("parallel"),
        name="wkv_step",
    )(s0, rowv(r), rowv(lw), rowv(k), rowv(kkn), rowv(a), colv(v))
    return y.reshape(bsz, D_MODEL), s1


def _rwkv_out_kernel(y_ref, r_ref, k_ref, v_ref, z_ref, x_ref, rk_ref, lg_ref, lb_ref, seg_ref,
                     w_ref, g_ref, b_ref, out_ref):
    seg = seg_ref[...]
    y = y_ref[...]
    inv_n = 1.0 / C_HEAD_DIM
    yc = y - _head_sum(y, seg) * inv_n
    var = _head_sum(yc * yc, seg) * inv_n
    yn = yc * lax.rsqrt(var + C_LN_EPS) * lg_ref[...] + lb_ref[...]
    v = v_ref[...]
    yy = yn + _head_sum(r_ref[...] * k_ref[...] * rk_ref[...], seg) * v
    out = _dot((yy * _silu(z_ref[...])).astype(BF16), w_ref[...])
    out_ref[...] = _deepnorm(x_ref[...], out, g_ref[...], b_ref[...])


def _rwkv_out(y, r, k, v, z, x, wt, g, b, tm=256):
    m, d = x.shape
    tm = min(tm, m)
    vec = _full_spec((1, d))
    return pl.pallas_call(
        _rwkv_out_kernel,
        grid=(m // tm,),
        in_specs=[_row_spec(tm, d)] * 6 + [vec, vec, vec, _full_spec(wt["seg"].shape),
                                           _full_spec((d, d)), vec, vec],
        out_specs=_row_spec(tm, d),
        out_shape=jax.ShapeDtypeStruct((m, d), F32),
        compiler_params=_cparams("parallel"),
        name="rwkv_out",
    )(y, r, k, v, z, x, wt["r_k"], wt["lnx_g"], wt["lnx_b"], wt["seg"], wt["w_out"],
      g.reshape(1, d), b.reshape(1, d))


def _rwkv_layer(x, s0, x_last, wt, ln_g, ln_b):
    bsz, t, d = x.shape
    x_prev = jnp.concatenate([x_last[:, None, :], x[:, :-1]], axis=1)
    x2 = x.reshape(bsz * t, d)
    r, lw, k, v, kkn, a, z = _rwkv_proj(x2, x_prev.reshape(bsz * t, d), wt)
    if s0 is None:
        b3 = lambda u: u.reshape(bsz, t, d)
        y, s1 = _wkv_chunked(b3(r), b3(lw), b3(k), b3(v), b3(kkn), b3(a))
        y = y.reshape(bsz * t, d)
    else:
        y, s1 = _wkv_step(s0, r, lw, k, v, kkn, a)
    x_new = _rwkv_out(y, r, k, v, z, x2, wt, ln_g, ln_b)
    return x_new.reshape(bsz, t, d), s1, x[:, -1]


def _run_trunk(x, ln_g, ln_b, a_wts, b_wts, c_wts, past=None):
    bsz = x.shape[0]
    new = {k: [] for k in ("cmp", "sel", "win", "re", "im", "wkv", "shift")}
    for i in range(DEPTH):
        j, kind = divmod(i, N_MIXERS)
        if kind == 0:
            if past is None:
                x, kc, ks, kw = _nsa_layer_prompt(x, a_wts[j], ln_g[i], ln_b[i])
                kw = kw[:, kw.shape[1] - min(WINDOW, kw.shape[1]):]
            else:
                x, kc, ks, kw = _nsa_layer_sample(x, a_wts[j], ln_g[i], ln_b[i], j, past["cmp"],
                                                  past["sel"], past["win"], past["page_table"])
            new["cmp"].append(kc)
            new["sel"].append(ks)
            new["win"].append(kw)
        elif kind == 1:
            if past is None:
                h0r = h0i = jnp.zeros((bsz, S5_GROUPS, S5_STATE), F32)
            else:
                h0r, h0i = past["s5_re"][j], past["s5_im"][j]
            x, hr, hi = _s5_layer(x, h0r, h0i, b_wts[j], ln_g[i], ln_b[i])
            new["re"].append(hr)
            new["im"].append(hi)
        else:
            if past is None:
                s0, x_last = None, jnp.zeros((bsz, D_MODEL), F32)
            else:
                s0, x_last = past["wkv"][j], past["shift"][j]
            x, s1, xl = _rwkv_layer(x, s0, x_last, c_wts[j], ln_g[i], ln_b[i])
            new["wkv"].append(s1)
            new["shift"].append(xl)
    kv_tail = (2, A_KV_HEADS, A_HEAD_DIM)
    kv = lambda rows: jnp.stack(rows).reshape((len(rows),) + rows[0].shape[:2] + kv_tail)
    return (x, kv(new["cmp"]), kv(new["sel"]), kv(new["win"]), jnp.stack(new["re"]), jnp.stack(new["im"]),
            jnp.stack(new["wkv"]), jnp.stack(new["shift"]))


def kernel(x_prompt, x_sample, cache_cmp_kv, cache_sel_kv, cache_win_kv, state_s5_re, state_s5_im,
           state_wkv, state_shift, page_table, ln_g, ln_b,
           a_w_in, a_cmp_pe, a_cmp_w1, a_cmp_w2, a_w_out,
           b_w_in, b_log_dt, b_a_re, b_a_im, b_b_re, b_b_im, b_c_re, b_c_im, b_d, b_w_glu, b_b_glu, b_w_out,
           c_mu, c_w_in, c_w0, c_w2, c_a0, c_a2, c_k_k, c_k_a, c_r_k, c_lnx_g, c_lnx_b, c_w_out):
    a_par = (a_w_in, a_cmp_pe, a_cmp_w1, a_cmp_w2, a_w_out)
    b_par = (b_w_in, b_log_dt, b_a_re, b_a_im, b_b_re, b_b_im, b_c_re, b_c_im, b_d, b_w_glu, b_b_glu, b_w_out)
    c_par = (c_mu, c_w_in, c_w0, c_w2, c_a0, c_a2, c_k_k, c_k_a, c_r_k, c_lnx_g, c_lnx_b, c_w_out)
    a_wts = [_nsa_weights(*[p[j] for p in a_par]) for j in range(a_w_in.shape[0])]
    b_wts = [_s5_weights(*[p[j] for p in b_par]) for j in range(b_w_in.shape[0])]
    c_wts = [_rwkv_weights(*[p[j] for p in c_par]) for j in range(c_w_in.shape[0])]

    (y_p, p_cmp, p_sel, p_win, p_re, p_im, p_wkv, p_shift) = _run_trunk(
        x_prompt, ln_g, ln_b, a_wts, b_wts, c_wts)
    past = dict(cmp=cache_cmp_kv, sel=cache_sel_kv, win=cache_win_kv, page_table=page_table,
                s5_re=state_s5_re, s5_im=state_s5_im, wkv=state_wkv, shift=state_shift)
    (y_s, s_cmp, s_sel, s_win, s_re, s_im, s_wkv, s_shift) = _run_trunk(
        x_sample, ln_g, ln_b, a_wts, b_wts, c_wts, past)
    return (y_p, y_s, p_cmp, s_cmp, p_sel, s_sel, p_win, s_win,
            p_re, s_re, p_im, s_im, p_wkv, s_wkv, p_shift, s_shift)
```
